```python
import jax, jax.numpy as jnp
from jax import lax
import numpy as np

D_MODEL = 1024
BATCH = 32
SEQ = 2048
DEPTH = 2

GRID_W = 64
CTX_LEN = 256
EPS = 1e-6

HG_HEADS = 4
HG_KDIM = 128
HG_VDIM = 128
HG_KW = HG_HEADS * HG_KDIM
HG_VW = HG_HEADS * HG_VDIM
HG_CHUNK = 16
POOL_WINDOWS = (2, 4, 8, 16)
POOL_GROUP = 128
POOL_WIDTH = len(POOL_WINDOWS) * POOL_GROUP
EV_SPLITS = (HG_KW, 2 * HG_KW, 3 * HG_KW, 3 * HG_KW + HG_VW, 3 * HG_KW + 2 * HG_VW)
EV_IN = 3 * HG_KW + 2 * HG_VW + POOL_WIDTH
EV_MIX = HG_VW + POOL_WIDTH
CONV_WIDTH = 512
CONV_K = 31
MLA_HEADS = 8
MLA_Q_RANK = 384
MLA_KV_RANK = 256
MLA_NOPE = 64
MLA_ROPE = 32
MLA_V = 64
MLA_QK = MLA_NOPE + MLA_ROPE
ROPE_AXIS_HALF = MLA_ROPE // 4
ROPE_BASE = 10000.0
Q_BLOCK = 128
OD_C1 = 2 * CONV_WIDTH
OD_C2 = OD_C1 + MLA_Q_RANK
OD_C3 = OD_C2 + MLA_KV_RANK
OD_IN = OD_C3 + MLA_ROPE
OD_MIX = CONV_WIDTH + MLA_HEADS * MLA_V
N_EXPERTS = 32
TOP_K = 4
D_FF = 1024
SWIGLU_LIMIT = 7.0
SWIGLU_ALPHA = 1.702
MOE_BLOCK = 256

N_EVEN = (DEPTH + 1) // 2
N_ODD = DEPTH // 2

kernel_name = "hybrid_hgrn2_pool_conformer_mla_moe_dit"


def rmsnorm(h, gain=None):
    h32 = h.astype(jnp.float32)
    y = h32 * lax.rsqrt(jnp.mean(h32 * h32, axis=-1, keepdims=True) + EPS)
    if gain is not None:
        y = y * gain.astype(jnp.float32)
    return y.astype(h.dtype)


def layernorm(h, w, b):
    h32 = h.astype(jnp.float32)
    mu = jnp.mean(h32, axis=-1, keepdims=True)
    d = h32 - mu
    var = jnp.mean(d * d, axis=-1, keepdims=True)
    return (d * lax.rsqrt(var + EPS) * w.astype(jnp.float32) + b.astype(jnp.float32)).astype(h.dtype)


def modulate(h, shift, scale):
    return rmsnorm(h) * (1.0 + scale) + shift


def axial_rope_tables(rows):
    t_row = jnp.repeat(jnp.arange(rows, dtype=jnp.float32), GRID_W)
    t_col = jnp.tile(jnp.arange(GRID_W, dtype=jnp.float32), rows)
    inv = 1.0 / (ROPE_BASE ** (jnp.arange(ROPE_AXIS_HALF, dtype=jnp.float32) / ROPE_AXIS_HALF))
    ang = jnp.stack([t_row[:, None] * inv, t_col[:, None] * inv], axis=1)
    return jnp.cos(ang), jnp.sin(ang)


def rope2d(x, cos, sin):
    shp = x.shape
    xr = x.reshape(shp[:-1] + (2, 2, ROPE_AXIS_HALF))
    x1, x2 = xr[..., 0, :], xr[..., 1, :]
    cos, sin = cos.astype(x.dtype), sin.astype(x.dtype)
    out = jnp.stack([x1 * cos - x2 * sin, x2 * cos + x1 * sin], axis=-2)
    return out.reshape(shp)


def gla_chunkwise(q, k, v, logf, s0):
    B, H, L, K = q.shape
    V = v.shape[-1]
    C = HG_CHUNK
    N = L // C
    q, k, logf = (a.reshape(B, H, N, C, K) for a in (q, k, logf))
    v = v.reshape(B, H, N, C, V)
    b = jnp.cumsum(logf, axis=3)
    b_last = b[:, :, :, -1:, :]
    q_dec = q * jnp.exp(b)
    k_inv = k * jnp.exp(-b)
    k_end = k * jnp.exp(b_last - b)
    mask = jnp.tril(jnp.ones((C, C), dtype=bool))
    att = jnp.where(mask, jnp.einsum('bhnck,bhnsk->bhncs', q_dec, k_inv), 0.0)
    o_intra = jnp.einsum('bhncs,bhnsv->bhncv', att, v)
    chunk_decay = jnp.exp(b_last[:, :, :, 0, :])

    def step(S, inp):
        qd, ke, vv, dec = inp
        o = jnp.einsum('bhck,bhkv->bhcv', qd, S)
        S = S * dec[..., None] + jnp.einsum('bhck,bhcv->bhkv', ke, vv)
        return S, o

    xs = (jnp.moveaxis(q_dec, 2, 0), jnp.moveaxis(k_end, 2, 0),
          jnp.moveaxis(v, 2, 0), jnp.moveaxis(chunk_decay, 2, 0))
    s_final, o_inter = lax.scan(step, s0, xs)
    o = o_intra + jnp.moveaxis(o_inter, 0, 2)
    return o.reshape(B, H, L, V), s_final


def to_heads(a):
    B, L, _ = a.shape
    return a.astype(jnp.float32).reshape(B, L, HG_HEADS, -1).transpose(0, 2, 1, 3)


def hgrn2_direction(qc, zc, vc, ql, zl, vl, lb, reverse):
    lb = lb.reshape(HG_HEADS, 1, HG_KDIM)

    def gates(z):
        f = lb + (1.0 - lb) * jax.nn.sigmoid(z)
        return 1.0 - f, jnp.log(f)

    flip = (lambda a: jnp.flip(a, axis=2)) if reverse else (lambda a: a)
    kc, gc = gates(zc)
    kl, gl = gates(zl)
    s0 = jnp.zeros(qc.shape[:2] + (HG_KDIM, HG_VDIM), jnp.float32)
    oc, s_ctx = gla_chunkwise(flip(qc), flip(kc), flip(vc), flip(gc), s0)
    ol, _ = gla_chunkwise(flip(ql), flip(kl), flip(vl), flip(gl), s_ctx)
    return flip(ol), flip(oc)


def hgrn2_bidir(ctx_parts, lat_parts, lb_f, lb_b):
    qc, zfc, zbc, vc = (to_heads(a) for a in ctx_parts)
    ql, zfl, zbl, vl = (to_heads(a) for a in lat_parts)
    qc, ql = jax.nn.silu(qc), jax.nn.silu(ql)
    ol_f, oc_f = hgrn2_direction(qc, zfc, vc, ql, zfl, vl, lb_f, False)
    ol_b, oc_b = hgrn2_direction(qc, zbc, vc, ql, zbl, vl, lb_b, True)
    return ol_f + ol_b, oc_f + oc_b


def hgrn_readout(o, og, norm_w):
    B, H, L, V = o.shape
    o = rmsnorm(o.transpose(0, 2, 1, 3), norm_w)
    gate = og.reshape(B, L, H, V)
    return (o.astype(og.dtype) * jax.nn.silu(gate)).reshape(B, L, H * V)


def multiscale_pool(u, pool_w, pool_scale):
    B, L, _ = u.shape
    u32 = u.astype(jnp.float32)
    cs = jnp.pad(jnp.cumsum(u32, axis=1), ((0, 0), (1, 0), (0, 0)))
    t = jnp.arange(L)
    outs = []
    for g, w in enumerate(POOL_WINDOWS):
        sl = slice(g * POOL_GROUP, (g + 1) * POOL_GROUP)
        lo = jnp.clip(t - w // 2, 0, L)
        hi = jnp.clip(t + w - w // 2, 0, L)
        seg = cs[..., sl]
        cnt = (hi - lo).astype(jnp.float32)[None, :, None]
        outs.append((seg[:, hi] - seg[:, lo]) / cnt - u32[..., sl])
    d = jnp.stack(outs, axis=2).astype(u.dtype)
    y = jnp.einsum('blgc,gcd->blgd', d, pool_w)
    return y.reshape(B, L, POOL_WIDTH) * pool_scale


def even_mixer(u, uc, w_in, lb_f, lb_b, norm_w, pool_w, pool_scale, w_out, need_ctx):
    q, zf, zb, v, og, pin = jnp.split(u @ w_in, EV_SPLITS, axis=-1)
    qc, zfc, zbc, vc, ogc, pinc = jnp.split(uc @ w_in, EV_SPLITS, axis=-1)
    o, oc = hgrn2_bidir((qc, zfc, zbc, vc), (q, zf, zb, v), lb_f, lb_b)
    y = jnp.concatenate([hgrn_readout(o, og, norm_w), multiscale_pool(pin, pool_w, pool_scale)], axis=-1) @ w_out
    if not need_ctx:
        return y, None
    yc = jnp.concatenate([hgrn_readout(oc, ogc, norm_w), multiscale_pool(pinc, pool_w, pool_scale)], axis=-1) @ w_out
    return y, yc


def conformer_conv(a, dw_w, dw_b, ln_w, ln_b):
    val, gate = jnp.split(a, 2, axis=-1)
    h = val * jax.nn.sigmoid(gate)
    h = lax.conv_general_dilated(h, dw_w[:, None, :], window_strides=(1,),
                                 padding=((CONV_K // 2, CONV_K // 2),),
                                 dimension_numbers=('NWC', 'WIO', 'NWC'),
                                 feature_group_count=CONV_WIDTH) + dw_b
    return jax.nn.silu(layernorm(h, ln_w, ln_b))


def mla_project(cq, ckv, kr, q_a_norm, w_uq, kv_a_norm, w_ukv, q_norm, k_norm, cos, sin):
    B, L, _ = ckv.shape
    kv = (rmsnorm(ckv, kv_a_norm) @ w_ukv).reshape(B, L, MLA_HEADS, MLA_NOPE + MLA_V)
    k_nope, v = kv[..., :MLA_NOPE], kv[..., MLA_NOPE:]
    k = jnp.concatenate([k_nope, jnp.broadcast_to(kr[:, :, None, :], (B, L, MLA_HEADS, MLA_ROPE))], axis=-1)
    k = rmsnorm(k, k_norm)
    q = None
    if cq is not None:
        q = (rmsnorm(cq, q_a_norm) @ w_uq).reshape(B, L, MLA_HEADS, MLA_QK)
        q = rmsnorm(q, q_norm)
    if cos is not None:
        k = jnp.concatenate([k[..., :MLA_NOPE], rope2d(k[..., MLA_NOPE:], cos[:, None], sin[:, None])], axis=-1)
        q = jnp.concatenate([q[..., :MLA_NOPE], rope2d(q[..., MLA_NOPE:], cos[:, None], sin[:, None])], axis=-1)
    return q, k, v


def attend(q, k, v):
    s = jnp.einsum('bqhd,bkhd->bhqk', q, k).astype(jnp.float32) * (MLA_QK ** -0.5)
    p = jax.nn.softmax(s, axis=-1).astype(v.dtype)
    return jnp.einsum('bhqk,bkhd->bqhd', p, v)


def blockwise_attention(q, k, v):
    B, L, H, Dq = q.shape
    nb = L // Q_BLOCK
    qb = q.reshape(B, nb, Q_BLOCK, H, Dq).transpose(1, 0, 2, 3, 4)
    o = lax.map(lambda qq: attend(qq, k, v), qb)
    return o.transpose(1, 0, 2, 3, 4).reshape(B, L, H * MLA_V)


def odd_mixer(u, uc, w_in, dw_w, dw_b, ln_w, ln_b, q_a_norm, w_uq, kv_a_norm, w_ukv,
              q_norm, k_norm, w_out, cos, sin, need_ctx):
    mla_w = (q_a_norm, w_uq, kv_a_norm, w_ukv, q_norm, k_norm)
    p = u @ w_in
    q, k, v = mla_project(p[..., OD_C1:OD_C2], p[..., OD_C2:OD_C3], p[..., OD_C3:], *mla_w, cos, sin)
    if need_ctx:
        pc = uc @ w_in
        qc, kc, vc = mla_project(pc[..., OD_C1:OD_C2], pc[..., OD_C2:OD_C3], pc[..., OD_C3:], *mla_w, None, None)
    else:
        pc = uc @ w_in[:, OD_C2:]
        _, kc, vc = mla_project(None, pc[..., :MLA_KV_RANK], pc[..., MLA_KV_RANK:], *mla_w, None, None)
    k_all = jnp.concatenate([kc, k], axis=1)
    v_all = jnp.concatenate([vc, v], axis=1)
    attn = blockwise_attention(q, k_all, v_all)
    y = jnp.concatenate([conformer_conv(p[..., :OD_C1], dw_w, dw_b, ln_w, ln_b), attn], axis=-1) @ w_out
    if not need_ctx:
        return y, None
    Bc, Lc = uc.shape[:2]
    attn_c = attend(qc, kc, vc).reshape(Bc, Lc, MLA_HEADS * MLA_V)
    yc = jnp.concatenate([conformer_conv(pc[..., :OD_C1], dw_w, dw_b, ln_w, ln_b), attn_c], axis=-1) @ w_out
    return y, yc


def moe_ffn(x2, w_r, b_r, w1, b1, w2, b2):
    T, D = x2.shape
    logits = (x2 @ w_r + b_r).astype(jnp.float32)
    top_v, top_i = lax.top_k(logits, TOP_K)
    gates = jax.nn.softmax(top_v, axis=-1)
    M = T * TOP_K
    flat_e = top_i.reshape(-1)
    flat_t = jnp.repeat(jnp.arange(T, dtype=jnp.int32), TOP_K)
    flat_g = gates.reshape(-1)
    order = jnp.argsort(flat_e)
    se, st, sg = flat_e[order], flat_t[order], flat_g[order]
    counts = jnp.bincount(flat_e, length=N_EXPERTS)
    padded = (counts + MOE_BLOCK - 1) // MOE_BLOCK * MOE_BLOCK
    pad_end = jnp.cumsum(padded)
    pad_start = pad_end - padded
    start = jnp.cumsum(counts) - counts
    dest = pad_start[se] + jnp.arange(M, dtype=jnp.int32) - start[se]
    n_blocks = -(-M // MOE_BLOCK) + N_EXPERTS
    P = n_blocks * MOE_BLOCK
    buf_t = jnp.zeros((P,), jnp.int32).at[dest].set(st)
    buf_g = jnp.zeros((P,), jnp.float32).at[dest].set(sg)
    blk_e = jnp.minimum(jnp.searchsorted(pad_end, jnp.arange(n_blocks) * MOE_BLOCK, side='right'), N_EXPERTS - 1)

    def step(acc, inp):
        idx, g, e = inp
        hb = x2[idx] @ w1[e] + b1[e]
        gl, lin = hb[:, :D_FF], hb[:, D_FF:]
        gl = jnp.minimum(gl, SWIGLU_LIMIT)
        lin = jnp.clip(lin, -SWIGLU_LIMIT, SWIGLU_LIMIT)
        a = gl * jax.nn.sigmoid(SWIGLU_ALPHA * gl) * (lin + 1.0)
        y = a @ w2[e] + b2[e]
        return acc.at[idx].add(y.astype(jnp.float32) * g[:, None]), None

    out, _ = lax.scan(step, jnp.zeros((T, D), jnp.float32),
                      (buf_t.reshape(n_blocks, MOE_BLOCK), buf_g.reshape(n_blocks, MOE_BLOCK), blk_e))
    return out.astype(x2.dtype)


def setup_inputs(seed: int = 0) -> dict:
    key = jax.random.key(seed)
    ks = iter(jax.random.split(key, 40))
    D = D_MODEL

    def nrm(shape, s):
        return jax.random.normal(next(ks), shape, jnp.float32) * s

    def gain(shape):
        return 1.0 + nrm(shape, 0.1)

    return {
        "x": nrm((BATCH, SEQ, D), 1.0),
        "c": nrm((BATCH, D), 1.0),
        "ctx": nrm((BATCH, CTX_LEN, D), 1.0),
        "c_ctx": nrm((D,), 1.0),
        "ada_w": nrm((DEPTH, D, 6 * D), 0.5 * D ** -0.5),
        "ada_b": nrm((DEPTH, 6 * D), 0.02),
        "ev_w_in": nrm((N_EVEN, D, EV_IN), D ** -0.5),
        "hgrn_lb_logits": nrm((DEPTH + 1, 2, HG_KW), 0.1),
        "hgrn_norm_w": gain((N_EVEN, HG_VDIM)),
        "pool_w": nrm((N_EVEN, len(POOL_WINDOWS), POOL_GROUP, POOL_GROUP), POOL_GROUP ** -0.5),
        "pool_scale": gain((N_EVEN, POOL_WIDTH)),
        "ev_w_out": nrm((N_EVEN, EV_MIX, D), EV_MIX ** -0.5),
        "od_w_in": nrm((N_ODD, D, OD_IN), D ** -0.5),
        "conv_dw_w": nrm((N_ODD, CONV_K, CONV_WIDTH), CONV_K ** -0.5),
        "conv_dw_b": nrm((N_ODD, CONV_WIDTH), 0.02),
        "conv_ln_w": gain((N_ODD, CONV_WIDTH)),
        "conv_ln_b": nrm((N_ODD, CONV_WIDTH), 0.02),
        "mla_q_a_norm": gain((N_ODD, MLA_Q_RANK)),
        "mla_w_uq": nrm((N_ODD, MLA_Q_RANK, MLA_HEADS * MLA_QK), MLA_Q_RANK ** -0.5),
        "mla_kv_a_norm": gain((N_ODD, MLA_KV_RANK)),
        "mla_w_ukv": nrm((N_ODD, MLA_KV_RANK, MLA_HEADS * (MLA_NOPE + MLA_V)), MLA_KV_RANK ** -0.5),
        "mla_q_norm": gain((N_ODD, MLA_QK)),
        "mla_k_norm": gain((N_ODD, MLA_QK)),
        "od_w_out": nrm((N_ODD, OD_MIX, D), OD_MIX ** -0.5),
        "moe_router_w": nrm((DEPTH, D, N_EXPERTS), D ** -0.5),
        "moe_router_b": nrm((DEPTH, N_EXPERTS), 0.01),
        "moe_w1": nrm((DEPTH, N_EXPERTS, D, 2 * D_FF), D ** -0.5),
        "moe_b1": nrm((DEPTH, N_EXPERTS, 2 * D_FF), 0.01),
        "moe_w2": nrm((DEPTH, N_EXPERTS, D_FF, D), D_FF ** -0.5),
        "moe_b2": nrm((DEPTH, N_EXPERTS, D), 0.01),
    }


def reference(x, c, ctx, c_ctx, ada_w, ada_b, ev_w_in, hgrn_lb_logits, hgrn_norm_w, pool_w,
              pool_scale, ev_w_out, od_w_in, conv_dw_w, conv_dw_b, conv_ln_w, conv_ln_b,
              mla_q_a_norm, mla_w_uq, mla_kv_a_norm, mla_w_ukv, mla_q_norm, mla_k_norm, od_w_out,
              moe_router_w, moe_router_b, moe_w1, moe_b1, moe_w2, moe_b2):
    B, L, D = x.shape
    rows = L // GRID_W
    cos, sin = axial_rope_tables(rows)
    lb_all = jnp.cumsum(jax.nn.softmax(hgrn_lb_logits.astype(jnp.float32), axis=0), axis=0)
    s_lat = jax.nn.silu(c)
    s_ctx = jax.nn.silu(c_ctx)
    h, hc = x, ctx
    for layer in range(DEPTH):
        need_ctx = layer < DEPTH - 1
        j = layer // 2
        mod = (s_lat @ ada_w[layer] + ada_b[layer])[:, None, :]
        modc = s_ctx @ ada_w[layer] + ada_b[layer]
        sh1, sc1, g1, sh2, sc2, g2 = jnp.split(mod, 6, axis=-1)
        csh1, csc1, cg1, csh2, csc2, cg2 = jnp.split(modc, 6, axis=-1)
        u = modulate(h, sh1, sc1)
        uc = modulate(hc, csh1, csc1)
        if layer % 2 == 0:
            y, yc = even_mixer(u, uc, ev_w_in[j], lb_all[layer, 0], lb_all[layer, 1], hgrn_norm_w[j],
                               pool_w[j], pool_scale[j], ev_w_out[j], need_ctx)
        else:
            y, yc = odd_mixer(u, uc, od_w_in[j], conv_dw_w[j], conv_dw_b[j], conv_ln_w[j], conv_ln_b[j],
                              mla_q_a_norm[j], mla_w_uq[j], mla_kv_a_norm[j], mla_w_ukv[j],
                              mla_q_norm[j], mla_k_norm[j], od_w_out[j], cos, sin, need_ctx)
        h = h + g1 * y
        moe_w = (moe_router_w[layer], moe_router_b[layer], moe_w1[layer], moe_b1[layer],
                 moe_w2[layer], moe_b2[layer])
        m = modulate(h, sh2, sc2).reshape(-1, D)
        if need_ctx:
            hc = hc + cg1 * yc
            mc = modulate(hc, csh2, csc2).reshape(-1, D)
            f = moe_ffn(jnp.concatenate([m, mc], axis=0), *moe_w)
            n_lat = m.shape[0]
            h = h + g2 * f[:n_lat].reshape(h.shape)
            hc = hc + cg2 * f[n_lat:].reshape(hc.shape)
        else:
            h = h + g2 * moe_ffn(m, *moe_w).reshape(h.shape)
    return h
```

```python
import functools

import jax
import jax.numpy as jnp
from jax import lax
from jax.experimental import pallas as pl
from jax.experimental.pallas import tpu as pltpu

F32 = jnp.float32
BF16 = jnp.bfloat16

D_MODEL = 1024
GRID_W = 64
EPS = 1e-6
HG_HEADS = 4
HG_DIM = 128
HG_W = HG_HEADS * HG_DIM
HG_SUB = 32
POOL_WINDOWS = (2, 4, 8, 16)
POOL_GROUP = 128
POOL_PAD = 16
CONV_WIDTH = 512
CONV_K = 31
CONV_PAD = 16
MLA_HEADS = 8
MLA_Q_RANK = 384
MLA_KV_RANK = 256
MLA_NOPE = 64
MLA_ROPE = 32
MLA_V = 64
MLA_QK = MLA_NOPE + MLA_ROPE
HEAD_PAD = 128
ROPE_AXIS_HALF = MLA_ROPE // 4
ROPE_BASE = 10000.0
N_EXPERTS = 32
TOP_K = 4
D_FF = 1024
SWIGLU_LIMIT = 7.0
SWIGLU_ALPHA = 1.702

ROW_TILE = 256
ROUTE_TILE = 512
MOE_TILE = 512
Q_TILE = 512
LANES = 128
VMEM_LIMIT = 56 * 1024 * 1024

OD_CQ = 1024
OD_CKV = 1536
OD_KR = 1792
OD_IN_PAD = 1920


def _cp(sem, vmem=VMEM_LIMIT):
    return pltpu.CompilerParams(dimension_semantics=sem, vmem_limit_bytes=vmem)


def _dot(a, b):
    return jnp.dot(a, b, preferred_element_type=F32)


def _dot_nt(a, b):
    return lax.dot_general(a, b, (((1,), (1,)), ((), ())), preferred_element_type=F32)


def _dot_tn(a, b):
    return lax.dot_general(a, b, (((0,), (0,)), ((), ())), preferred_element_type=F32)


def _split3(x):
    hi = x.astype(BF16)
    r1 = x - hi.astype(F32)
    mid = r1.astype(BF16)
    lo = (r1 - mid.astype(F32)).astype(BF16)
    return hi, mid, lo


def _sigmoid(x):
    return 1.0 / (1.0 + jnp.exp(-x))


def _silu(x):
    return x * _sigmoid(x)


def _rms_scale(x, width):
    return lax.rsqrt(jnp.sum(x * x, axis=-1, keepdims=True) * (1.0 / width) + EPS)


def _ada_kernel(c_ref, w_ref, b_ref, o_ref):
    s = _silu(c_ref[...])
    sh, sm, sl = _split3(s)
    wh, wm, wl = _split3(w_ref[0])
    acc = _dot(sh, wh) + _dot(sh, wm) + _dot(sm, wh) + _dot(sh, wl) + _dot(sl, wh) + _dot(sm, wm)
    o_ref[0] = acc + b_ref[0]


def _ada_table(cond, ada_w, ada_b):
    depth, d, n = ada_w.shape
    rows = cond.shape[0]
    nb = n // d
    return pl.pallas_call(
        _ada_kernel,
        grid=(depth, nb),
        in_specs=[
            pl.BlockSpec((rows, d), lambda l, j: (0, 0)),
            pl.BlockSpec((1, d, d), lambda l, j: (l, 0, j)),
            pl.BlockSpec((1, 1, d), lambda l, j: (l, 0, j)),
        ],
        out_specs=pl.BlockSpec((1, rows, d), lambda l, j: (l, 0, j)),
        out_shape=jax.ShapeDtypeStruct((depth, rows, n), F32),
        compiler_params=_cp(("arbitrary", "arbitrary")),
    )(cond, ada_w, ada_b.reshape(depth, 1, n))


def _modproj_kernel(h_ref, mod_ref, w_ref, o_ref):
    x = h_ref[0]
    mod = mod_ref[0, 0]
    u = x * _rms_scale(x, x.shape[-1]) * (1.0 + mod[1:2]) + mod[0:1]
    o_ref[0] = _dot(u.astype(BF16), w_ref[...])


def _modproj(h, modtab, w, sel):
    b, s, d = h.shape
    n = w.shape[1]
    return pl.pallas_call(
        _modproj_kernel,
        grid=(b, s // ROW_TILE),
        in_specs=[
            pl.BlockSpec((1, ROW_TILE, d), lambda i, j: (i, j, 0)),
            pl.BlockSpec((1, 1, 6, d), lambda i, j: (i, sel(j), 0, 0)),
            pl.BlockSpec((d, n), lambda i, j: (0, 0)),
        ],
        out_specs=pl.BlockSpec((1, ROW_TILE, n), lambda i, j: (i, j, 0)),
        out_shape=jax.ShapeDtypeStruct((b, s, n), F32),
        compiler_params=_cp(("parallel", "arbitrary")),
    )(h, modtab, w)


def _hgrn_block(q, z, v, lb, st, reverse):
    rows = q.shape[0]
    nsub = rows // HG_SUB
    f = lb + (1.0 - lb) * _sigmoid(z)
    kin = 1.0 - f
    logf = jnp.log(f)
    ri = lax.broadcasted_iota(jnp.int32, (rows, rows), 0)
    ci = lax.broadcasted_iota(jnp.int32, (rows, rows), 1)
    same = (ri // HG_SUB) == (ci // HG_SUB)
    causal = same & ((ci >= ri) if reverse else (ci <= ri))
    tri = jnp.where(causal, 1.0, 0.0).astype(BF16)
    lh, lm, ll = _split3(logf)
    bcum = _dot(tri, lh) + _dot(tri, lm) + _dot(tri, ll)
    b3 = bcum.reshape(nsub, HG_SUB, HG_DIM)
    tot_row = 0 if reverse else HG_SUB - 1
    mid_row = HG_SUB // 2 if reverse else HG_SUB // 2 - 1
    tot = b3[:, tot_row:tot_row + 1, :]
    mid = b3[:, mid_row:mid_row + 1, :]
    q3 = q.reshape(nsub, HG_SUB, HG_DIM)
    k3 = kin.reshape(nsub, HG_SUB, HG_DIM)
    q_dec = (q3 * jnp.exp(b3)).astype(BF16)
    k_end = (k3 * jnp.exp(tot - b3)).astype(BF16)
    q_mid = (q3 * jnp.exp(b3 - mid)).reshape(rows, HG_DIM).astype(BF16)
    k_mid = (k3 * jnp.exp(mid - b3)).reshape(rows, HG_DIM).astype(BF16)
    vb = v.astype(BF16)
    att = jnp.where(causal, _dot_nt(q_mid, k_mid), 0.0).astype(BF16)
    o_intra = _dot(att, vb)
    v3 = vb.reshape(nsub, HG_SUB, HG_DIM)
    dec = jnp.exp(tot)
    upd = [_dot_tn(v3[j], k_end[j]) for j in range(nsub)]
    order = range(nsub - 1, -1, -1) if reverse else range(nsub)
    o_inter = [None] * nsub
    for j in order:
        o_inter[j] = _dot_nt(q_dec[j], st.astype(BF16))
        st = st * dec[j] + upd[j]
    o = o_intra + jnp.concatenate(o_inter, axis=0)
    return o, st


def _hgrn_kernel(q_ref, zf_ref, zb_ref, v_ref, og_ref, lb_ref, nw_ref, o_ref, acc_ref, *, nc, nl):
    blk = ROW_TILE
    nblk = nc + nl
    acc_ref[...] = jnp.zeros_like(acc_ref)
    lb_f = lb_ref[0:1, :]
    lb_b = lb_ref[1:2, :]

    def body(i, carry):
        st_f, st_b = carry
        rf = pl.multiple_of(i * blk, blk)
        ib = jnp.where(i < nc, nc - 1 - i, 2 * nc + nl - 1 - i)
        rb = pl.multiple_of(ib * blk, blk)
        qf = _silu(q_ref[0, pl.ds(rf, blk), :])
        o_f, st_f = _hgrn_block(qf, zf_ref[0, pl.ds(rf, blk), :], v_ref[0, pl.ds(rf, blk), :],
                                lb_f, st_f, False)
        acc_ref[pl.ds(rf, blk), :] += o_f
        qb = _silu(q_ref[0, pl.ds(rb, blk), :])
        o_b, st_b = _hgrn_block(qb, zb_ref[0, pl.ds(rb, blk), :], v_ref[0, pl.ds(rb, blk), :],
                                lb_b, st_b, True)
        acc_ref[pl.ds(rb, blk), :] += o_b
        return st_f, st_b

    zero = jnp.zeros((HG_DIM, HG_DIM), F32)
    lax.fori_loop(0, nblk, body, (zero, zero))
    o = acc_ref[...]
    og = og_ref[0]
    o_ref[0] = o * _rms_scale(o, HG_DIM) * nw_ref[...] * _silu(og)


def _hgrn(p, lb, norm_w, nc, nl):
    b, s, _ = p.shape
    sec = lambda k: pl.BlockSpec((1, s, HG_DIM), lambda i, h, k=k: (i, 0, HG_HEADS * k + h))
    return pl.pallas_call(
        functools.partial(_hgrn_kernel, nc=nc, nl=nl),
        grid=(b, HG_HEADS),
        in_specs=[sec(0), sec(1), sec(2), sec(3), sec(4),
                  pl.BlockSpec((2, HG_DIM), lambda i, h: (0, h)),
                  pl.BlockSpec((1, HG_DIM), lambda i, h: (0, 0))],
        out_specs=pl.BlockSpec((1, s, HG_DIM), lambda i, h: (i, 0, h)),
        out_shape=jax.ShapeDtypeStruct((b, s, HG_W), F32),
        scratch_shapes=[pltpu.VMEM((s, HG_DIM), F32)],
        compiler_params=_cp(("parallel", "arbitrary")),
    )(p, p, p, p, p, lb, norm_w.reshape(1, HG_DIM))


def _pool_kernel(x_ref, w_ref, sc_ref, o_ref, *, lc, ll):
    z = jnp.zeros((POOL_PAD, POOL_GROUP), F32)
    n = lc + ll + 3 * POOL_PAD
    pos_c = lax.broadcasted_iota(jnp.int32, (lc, 1), 0)
    pos_l = lax.broadcasted_iota(jnp.int32, (ll, 1), 0)
    for g, w in enumerate(POOL_WINDOWS):
        x = x_ref[0, :, g * POOL_GROUP:(g + 1) * POOL_GROUP]
        xp = jnp.concatenate([z, x[:lc], z, x[lc:], z], axis=0)
        acc = xp + pltpu.roll(xp, 1, 0)
        half = 1
        while 2 * half < w:
            acc = pltpu.roll(acc, half, 0) + pltpu.roll(acc, n - half, 0)
            half *= 2
        win = jnp.concatenate([acc[POOL_PAD:POOL_PAD + lc],
                               acc[2 * POOL_PAD + lc:2 * POOL_PAD + lc + ll]], axis=0)

        def count(pos, length):
            lo = jnp.maximum(pos - w // 2, 0)
            hi = jnp.minimum(pos + w - w // 2, length)
            return (hi - lo).astype(F32)

        cnt = jnp.concatenate([count(pos_c, lc), count(pos_l, ll)], axis=0)
        d = win / cnt - x
        y = _dot(d.astype(BF16), w_ref[g])
        o_ref[0, :, g * POOL_GROUP:(g + 1) * POOL_GROUP] = y * sc_ref[:, g * POOL_GROUP:(g + 1) * POOL_GROUP]


def _pool(p, pool_w, pool_scale, lc, ll):
    b, s, n = p.shape
    width = len(POOL_WINDOWS) * POOL_GROUP
    return pl.pallas_call(
        functools.partial(_pool_kernel, lc=lc, ll=ll),
        grid=(b,),
        in_specs=[pl.BlockSpec((1, s, width), lambda i: (i, 0, n // width - 1)),
                  pl.BlockSpec(pool_w.shape, lambda i: (0, 0, 0)),
                  pl.BlockSpec((1, width), lambda i: (0, 0))],
        out_specs=pl.BlockSpec((1, s, width), lambda i: (i, 0, 0)),
        out_shape=jax.ShapeDtypeStruct((b, s, width), F32),
        compiler_params=_cp(("parallel",)),
    )(p, pool_w.astype(BF16), pool_scale.reshape(1, width))


def _mixout_kernel(a_ref, b_ref, h_ref, mod_ref, wa_ref, wb_ref, wr_ref, br_ref,
                   h1_ref, m_ref, lg_ref):
    mod = mod_ref[0, 0]
    y = _dot(a_ref[0].astype(BF16), wa_ref[...]) + _dot(b_ref[0].astype(BF16), wb_ref[...])
    h1 = h_ref[0] + mod[2:3] * y
    h1_ref[0] = h1
    m = h1 * _rms_scale(h1, h1.shape[-1]) * (1.0 + mod[4:5]) + mod[3:4]
    m_ref[0] = m
    mh, mm, ml = _split3(m)
    wh, wm, wl = wr_ref[0], wr_ref[1], wr_ref[2]
    lg = (_dot_nt(mh, wh) + _dot_nt(mh, wm) + _dot_nt(mm, wh)
          + _dot_nt(mh, wl) + _dot_nt(ml, wh) + _dot_nt(mm, wm))
    lg_ref[0] = lg + br_ref[...]


def _mixout(a, bmix, h, modtab, wa, wb, wr3, br, sel, h_off, rows):
    b = h.shape[0]
    d = h.shape[-1]
    ka, kb = a.shape[-1], bmix.shape[-1]
    out3 = lambda n: jax.ShapeDtypeStruct((b, rows, n), F32)
    return pl.pallas_call(
        _mixout_kernel,
        grid=(b, rows // ROW_TILE),
        in_specs=[
            pl.BlockSpec((1, ROW_TILE, ka), lambda i, j: (i, j, 0)),
            pl.BlockSpec((1, ROW_TILE, kb), lambda i, j: (i, j, 0)),
            pl.BlockSpec((1, ROW_TILE, d), lambda i, j: (i, j + h_off, 0)),
            pl.BlockSpec((1, 1, 6, d), lambda i, j: (i, sel(j), 0, 0)),
            pl.BlockSpec((ka, d), lambda i, j: (0, 0)),
            pl.BlockSpec((kb, d), lambda i, j: (0, 0)),
            pl.BlockSpec((3, N_EXPERTS, d), lambda i, j: (0, 0, 0)),
            pl.BlockSpec((1, N_EXPERTS), lambda i, j: (0, 0)),
        ],
        out_specs=[
            pl.BlockSpec((1, ROW_TILE, d), lambda i, j: (i, j, 0)),
            pl.BlockSpec((1, ROW_TILE, d), lambda i, j: (i, j, 0)),
            pl.BlockSpec((1, ROW_TILE, N_EXPERTS), lambda i, j: (i, j, 0)),
        ],
        out_shape=[out3(d), out3(d), out3(N_EXPERTS)],
        compiler_params=_cp(("parallel", "arbitrary")),
    )(a, bmix, h, modtab, wa, wb, wr3, br)


def _route_kernel(lg_ref, idx_ref, gate_ref, rank_ref, cnt_ref):
    t = lg_ref.shape[0]

    @pl.when(pl.program_id(0) == 0)
    def _():
        cnt_ref[...] = jnp.zeros_like(cnt_ref)

    lg = lg_ref[...]
    lane = lax.broadcasted_iota(jnp.int32, (t, N_EXPERTS), 1).astype(F32)
    hot, vals, idxs = [], [], []
    for _ in range(TOP_K):
        mx = jnp.max(lg, axis=-1, keepdims=True)
        ix = jnp.min(jnp.where(lg == mx, lane, float(N_EXPERTS)), axis=-1, keepdims=True)
        oh = lane == ix
        hot.append(oh)
        vals.append(mx)
        idxs.append(ix.astype(jnp.int32))
        lg = jnp.where(oh, -jnp.inf, lg)
    ex = [jnp.exp(v - vals[0]) for v in vals]
    den = ex[0] + ex[1] + ex[2] + ex[3]
    sel = jnp.where(hot[0] | hot[1] | hot[2] | hot[3], 1.0, 0.0)
    ri = lax.broadcasted_iota(jnp.int32, (t, t), 0)
    ci = lax.broadcasted_iota(jnp.int32, (t, t), 1)
    before = jnp.where(ci < ri, 1.0, 0.0).astype(BF16)
    base = _dot(before, sel.astype(BF16)) + cnt_ref[...]
    out_lane = lax.broadcasted_iota(jnp.int32, (t, LANES), 1)
    idx_o = jnp.zeros((t, LANES), jnp.int32)
    gate_o = jnp.zeros((t, LANES), F32)
    rank_o = jnp.zeros((t, LANES), jnp.int32)
    for k in range(TOP_K):
        rk = jnp.sum(jnp.where(hot[k], base, 0.0), axis=-1, keepdims=True).astype(jnp.int32)
        idx_o = jnp.where(out_lane == k, idxs[k], idx_o)
        gate_o = jnp.where(out_lane == k, ex[k] / den, gate_o)
        rank_o = jnp.where(out_lane == k, rk, rank_o)
    idx_ref[...] = idx_o
    gate_ref[...] = gate_o
    rank_ref[...] = rank_o
    cnt_ref[...] += jnp.sum(sel, axis=0, keepdims=True)


def _route(logits):
    t = logits.shape[0]
    tile = pl.BlockSpec((ROUTE_TILE, LANES), lambda i: (i, 0))
    return pl.pallas_call(
        _route_kernel,
        grid=(t // ROUTE_TILE,),
        in_specs=[pl.BlockSpec((ROUTE_TILE, N_EXPERTS), lambda i: (i, 0))],
        out_specs=[tile, tile, tile, pl.BlockSpec((1, N_EXPERTS), lambda i: (0, 0))],
        out_shape=[jax.ShapeDtypeStruct((t, LANES), jnp.int32),
                   jax.ShapeDtypeStruct((t, LANES), F32),
                   jax.ShapeDtypeStruct((t, LANES), jnp.int32),
                   jax.ShapeDtypeStruct((1, N_EXPERTS), F32)],
        compiler_params=_cp(("arbitrary",)),
    )(logits)


def _row_copy(src_ref, src_row, dst_ref, dst_row, sem):
    return pltpu.make_async_copy(src_ref.at[pl.ds(src_row, 1)], dst_ref.at[pl.ds(dst_row, 1)], sem)


def _dispatch_kernel(dest_ref, m_ref, xs_ref, sem, *, tile):
    base = pl.program_id(0) * tile

    def issue(r, c):
        for k in range(TOP_K):
            _row_copy(m_ref, base + r, xs_ref, dest_ref[r * TOP_K + k], sem).start()
        return c

    lax.fori_loop(0, tile, issue, 0)

    def drain(r, c):
        for k in range(TOP_K):
            _row_copy(m_ref, base + r, xs_ref, dest_ref[r * TOP_K + k], sem).wait()
        return c

    lax.fori_loop(0, tile, drain, 0)


def _dispatch(m, dest_flat, n_rows):
    t, d = m.shape
    tile = ROUTE_TILE
    return pl.pallas_call(
        functools.partial(_dispatch_kernel, tile=tile),
        grid=(t // tile,),
        in_specs=[pl.BlockSpec((tile * TOP_K,), lambda i: (i,), memory_space=pltpu.SMEM),
                  pl.BlockSpec(memory_space=pl.ANY)],
        out_specs=pl.BlockSpec(memory_space=pl.ANY),
        out_shape=jax.ShapeDtypeStruct((n_rows, d), F32),
        scratch_shapes=[pltpu.SemaphoreType.DMA(())],
        compiler_params=_cp(("arbitrary",)),
    )(dest_flat, m)


def _experts_kernel(be_ref, nv_ref, x_ref, w1_ref, b1_ref, w2_ref, b2_ref, y_ref):
    i = pl.program_id(0)
    nv = nv_ref[i]

    @pl.when(nv > 0)
    def _():
        rows = lax.broadcasted_iota(jnp.int32, (x_ref.shape[0], 1), 0)
        x = jnp.where(rows < nv, x_ref[...], 0.0).astype(BF16)
        hb = _dot(x, w1_ref[0]) + b1_ref[0]
        gl = jnp.minimum(hb[:, :D_FF], SWIGLU_LIMIT)
        lin = jnp.clip(hb[:, D_FF:], -SWIGLU_LIMIT, SWIGLU_LIMIT)
        a = gl * _sigmoid(SWIGLU_ALPHA * gl) * (lin + 1.0)
        y_ref[...] = _dot(a.astype(BF16), w2_ref[0]) + b2_ref[0]

    @pl.when(nv <= 0)
    def _():
        y_ref[...] = jnp.zeros_like(y_ref)


def _experts(xs, blk_e, blk_nv, w1, b1, w2, b2):
    p, d = xs.shape
    e, _, f2 = w1.shape
    grid_spec = pltpu.PrefetchScalarGridSpec(
        num_scalar_prefetch=2,
        grid=(p // MOE_TILE,),
        in_specs=[
            pl.BlockSpec((MOE_TILE, d), lambda i, be, nv: (i, 0)),
            pl.BlockSpec((1, d, f2), lambda i, be, nv: (be[i], 0, 0)),
            pl.BlockSpec((1, 1, f2), lambda i, be, nv: (be[i], 0, 0)),
            pl.BlockSpec((1, f2 // 2, d), lambda i, be, nv: (be[i], 0, 0)),
            pl.BlockSpec((1, 1, d), lambda i, be, nv: (be[i], 0, 0)),
        ],
        out_specs=pl.BlockSpec((MOE_TILE, d), lambda i, be, nv: (i, 0)),
    )
    return pl.pallas_call(
        _experts_kernel,
        grid_spec=grid_spec,
        out_shape=jax.ShapeDtypeStruct((p, d), F32),
        compiler_params=_cp(("arbitrary",)),
    )(blk_e, blk_nv, xs, w1, b1.reshape(e, 1, f2), w2, b2.reshape(e, 1, d))


def _combine_kernel(dest_ref, ys_ref, gate_ref, h_ref, mod_ref, o_ref, buf_ref, sem):
    tile = h_ref.shape[1]

    def issue(r, c):
        for k in range(TOP_K):
            pltpu.make_async_copy(ys_ref.at[pl.ds(dest_ref[r * TOP_K + k], 1)],
                                  buf_ref.at[k, pl.ds(r, 1)], sem).start()
        return c

    lax.fori_loop(0, tile, issue, 0)

    def drain(r, c):
        for k in range(TOP_K):
            pltpu.make_async_copy(ys_ref.at[pl.ds(dest_ref[r * TOP_K + k], 1)],
                                  buf_ref.at[k, pl.ds(r, 1)], sem).wait()
        return c

    lax.fori_loop(0, tile, drain, 0)
    gate = gate_ref[...]
    f = gate[:, 0:1] * buf_ref[0]
    for k in range(1, TOP_K):
        f = f + gate[:, k:k + 1] * buf_ref[k]
    o_ref[0] = h_ref[0] + mod_ref[0, 0][5:6] * f


def _combine(ys, dest_flat, gate, h1, modtab, sel, out_rows, out_map):
    b, r, d = h1.shape
    nblk = r // ROW_TILE
    return pl.pallas_call(
        _combine_kernel,
        grid=(b, nblk),
        in_specs=[
            pl.BlockSpec((ROW_TILE * TOP_K,), lambda i, j: (i * nblk + j,), memory_space=pltpu.SMEM),
            pl.BlockSpec(memory_space=pl.ANY),
            pl.BlockSpec((ROW_TILE, LANES), lambda i, j: (i * nblk + j, 0)),
            pl.BlockSpec((1, ROW_TILE, d), lambda i, j: (i, j, 0)),
            pl.BlockSpec((1, 1, 6, d), lambda i, j: (i, sel(j), 0, 0)),
        ],
        out_specs=pl.BlockSpec((1, ROW_TILE, d), lambda i, j: (i, out_map(j), 0)),
        out_shape=jax.ShapeDtypeStruct((b, out_rows, d), F32),
        scratch_shapes=[pltpu.VMEM((TOP_K, ROW_TILE, d), F32), pltpu.SemaphoreType.DMA(())],
        compiler_params=_cp(("arbitrary", "arbitrary")),
    )(dest_flat, ys, gate, h1, modtab)


def _moe(m, logits, h1, modtab, sel, w1, b1, w2, b2, out_rows, out_map):
    b, r, d = m.shape
    t = b * r
    idx, gate, rank, counts = _route(logits.reshape(t, N_EXPERTS))
    top_i = idx[:, :TOP_K]
    counts = counts[0].astype(jnp.int32)
    padded = (counts + MOE_TILE - 1) // MOE_TILE * MOE_TILE
    pad_end = jnp.cumsum(padded)
    pad_start = pad_end - padded
    dest = (pad_start[top_i] + rank[:, :TOP_K]).reshape(-1)
    n_blocks = (t * TOP_K) // MOE_TILE + N_EXPERTS
    blk_start = jnp.arange(n_blocks, dtype=jnp.int32) * MOE_TILE
    blk_e = jnp.minimum(jnp.searchsorted(pad_end, blk_start, side='right'), N_EXPERTS - 1).astype(jnp.int32)
    blk_nv = jnp.clip(counts[blk_e] - (blk_start - pad_start[blk_e]), 0, MOE_TILE).astype(jnp.int32)
    xs = _dispatch(m.reshape(t, d), dest, n_blocks * MOE_TILE)
    ys = _experts(xs, blk_e, blk_nv, w1, b1, w2, b2)
    return _combine(ys, dest, gate, h1, modtab, sel, out_rows, out_map)


def _conv_kernel(a_ref, dw_ref, db_ref, lw_ref, lb_ref, o_ref, hs_ref, *, ll):
    hs_ref[0:CONV_PAD, :] = jnp.zeros((CONV_PAD, CONV_WIDTH), F32)
    hs_ref[CONV_PAD + ll:2 * CONV_PAD + ll, :] = jnp.zeros((CONV_PAD, CONV_WIDTH), F32)
    hs_ref[CONV_PAD:CONV_PAD + ll, :] = a_ref[0, :, :CONV_WIDTH] * _sigmoid(a_ref[0, :, CONV_WIDTH:])
    first = CONV_PAD - CONV_K // 2

    def tile(i, c):
        r0 = pl.multiple_of(i * ROW_TILE, ROW_TILE)
        win = hs_ref[pl.ds(r0, ROW_TILE + 2 * CONV_PAD), :]
        acc = jnp.zeros((ROW_TILE, CONV_WIDTH), F32)
        for k in range(CONV_K):
            acc = acc + dw_ref[k:k + 1, :] * win[first + k:first + k + ROW_TILE]
        acc = acc + db_ref[...]
        mu = jnp.mean(acc, axis=-1, keepdims=True)
        dlt = acc - mu
        var = jnp.mean(dlt * dlt, axis=-1, keepdims=True)
        y = dlt * lax.rsqrt(var + EPS) * lw_ref[...] + lb_ref[...]
        o_ref[0, pl.ds(r0, ROW_TILE), :] = _silu(y)
        return c

    lax.fori_loop(0, ll // ROW_TILE, tile, 0)


def _conv(p, dw_w, dw_b, ln_w, ln_b, ll):
    b = p.shape[0]
    row = lambda v: v.reshape(1, CONV_WIDTH)
    vec = pl.BlockSpec((1, CONV_WIDTH), lambda i: (0, 0))
    return pl.pallas_call(
        functools.partial(_conv_kernel, ll=ll),
        grid=(b,),
        in_specs=[pl.BlockSpec((1, ll, 2 * CONV_WIDTH), lambda i: (i, 0, 0)),
                  pl.BlockSpec((CONV_K, CONV_WIDTH), lambda i: (0, 0)), vec, vec, vec],
        out_specs=pl.BlockSpec((1, ll, CONV_WIDTH), lambda i: (i, 0, 0)),
        out_shape=jax.ShapeDtypeStruct((b, ll, CONV_WIDTH), F32),
        scratch_shapes=[pltpu.VMEM((ll + 2 * CONV_PAD, CONV_WIDTH), F32)],
        compiler_params=_cp(("parallel",)),
    )(p, dw_w, row(dw_b), row(ln_w), row(ln_b))


def _mla_proj_kernel(cq_ref, ckv_ref, kr_ref, cs_ref, qa_ref, kva_ref, wq_ref, wqs_ref, wk_ref,
                     wv_ref, ek_ref, eks_ref, gq_ref, gqs_ref, gk_ref, gks_ref,
                     q_ref, k_ref, v_ref):
    cos = cs_ref[0]
    sin = cs_ref[1]
    cq = cq_ref[0]
    cqn = (cq * _rms_scale(cq, MLA_Q_RANK) * qa_ref[...]).astype(BF16)
    qx = _dot(cqn, wq_ref[...])
    qs = _dot(cqn, wqs_ref[...])
    ckv = ckv_ref[0]
    ckvn = (ckv * _rms_scale(ckv, MLA_KV_RANK) * kva_ref[...]).astype(BF16)
    kr = kr_ref[0]
    rh = kr.astype(BF16)
    rl = (kr - rh.astype(F32)).astype(BF16)
    kx = _dot(ckvn, wk_ref[...]) + _dot(rh, ek_ref[...]) + _dot(rl, ek_ref[...])
    ks = _dot(rh, eks_ref[...]) + _dot(rl, eks_ref[...])
    v_ref[0] = _dot(ckvn, wv_ref[...]).astype(BF16)
    scale = MLA_QK ** -0.5
    for h in range(MLA_HEADS):
        sl = slice(h * HEAD_PAD, (h + 1) * HEAD_PAD)
        qh = qx[:, sl]
        q_ref[0, :, sl] = (_rms_scale(qh, MLA_QK) * scale
                           * (qh * gq_ref[...] * cos + qs[:, sl] * gqs_ref[...] * sin)).astype(BF16)
        kh = kx[:, sl]
        k_ref[0, :, sl] = (_rms_scale(kh, MLA_QK)
                           * (kh * gk_ref[...] * cos + ks[:, sl] * gks_ref[...] * sin)).astype(BF16)


def _mla_proj(p, cs, qa, kva, wq, wqs, wk, wv, ek, eks, gq, gqs, gk, gks):
    b, s, _ = p.shape
    hw = MLA_HEADS * HEAD_PAD
    full = lambda a: pl.BlockSpec(a.shape, lambda i, j: (0,) * a.ndim)
    out = pl.BlockSpec((1, ROW_TILE, hw), lambda i, j: (i, j, 0))
    consts = (qa, kva, wq, wqs, wk, wv, ek, eks, gq, gqs, gk, gks)
    return pl.pallas_call(
        _mla_proj_kernel,
        grid=(b, s // ROW_TILE),
        in_specs=[pl.BlockSpec((1, ROW_TILE, 512), lambda i, j: (i, j, OD_CQ // 512)),
                  pl.BlockSpec((1, ROW_TILE, MLA_KV_RANK), lambda i, j: (i, j, OD_CKV // MLA_KV_RANK)),
                  pl.BlockSpec((1, ROW_TILE, LANES), lambda i, j: (i, j, OD_KR // LANES)),
                  pl.BlockSpec((2, ROW_TILE, HEAD_PAD), lambda i, j: (0, j, 0))]
                 + [full(a) for a in consts],
        out_specs=[out, out, out],
        out_shape=[jax.ShapeDtypeStruct((b, s, hw), BF16)] * 3,
        compiler_params=_cp(("parallel", "arbitrary")),
    )(p, p, p, cs, *consts)


def _attn_kernel(q_ref, k_ref, v_ref, o_ref):
    s = _dot_nt(q_ref[0], k_ref[0])
    p = jnp.exp(s - jnp.max(s, axis=-1, keepdims=True))
    den = jnp.sum(p, axis=-1, keepdims=True)
    o_ref[0] = (_dot(p.astype(BF16), v_ref[0]) / den).astype(BF16)


def _attention(q, k, v, ll):
    b, s, hw = k.shape
    blk = lambda rows: pl.BlockSpec((1, rows, HEAD_PAD), lambda i, h, j: (i, 0, h))
    qo = pl.BlockSpec((1, Q_TILE, HEAD_PAD), lambda i, h, j: (i, j, h))
    return pl.pallas_call(
        _attn_kernel,
        grid=(b, MLA_HEADS, ll // Q_TILE),
        in_specs=[qo, blk(s), blk(s)],
        out_specs=qo,
        out_shape=jax.ShapeDtypeStruct((b, ll, hw), BF16),
        compiler_params=_cp(("parallel", "arbitrary", "arbitrary")),
    )(q, k, v)


def _place(parts, lead):
    out = jnp.zeros(lead + (MLA_HEADS, HEAD_PAD), F32)
    for arr, off in parts:
        out = out.at[..., off:off + arr.shape[-1]].set(arr)
    return out.reshape(lead + (MLA_HEADS * HEAD_PAD,))


def _swap_pairs(a):
    g = a.reshape(a.shape[:-1] + (2, 2, ROPE_AXIS_HALF))
    return jnp.flip(g, axis=-2).reshape(a.shape)


def _rope_tables(rows_total, ll):
    grid_rows = ll // GRID_W
    t_row = jnp.repeat(jnp.arange(grid_rows, dtype=F32), GRID_W)
    t_col = jnp.tile(jnp.arange(GRID_W, dtype=F32), grid_rows)
    inv = 1.0 / (ROPE_BASE ** (jnp.arange(ROPE_AXIS_HALF, dtype=F32) / ROPE_AXIS_HALF))
    ang_r = t_row[:, None] * inv
    ang_c = t_col[:, None] * inv
    cos32 = jnp.concatenate([jnp.cos(ang_r), jnp.cos(ang_r), jnp.cos(ang_c), jnp.cos(ang_c)], axis=-1)
    sin32 = jnp.concatenate([-jnp.sin(ang_r), jnp.sin(ang_r), -jnp.sin(ang_c), jnp.sin(ang_c)], axis=-1)
    extra = rows_total - ll
    cos32 = jnp.concatenate([cos32, jnp.ones((extra, MLA_ROPE), F32)], axis=0)
    sin32 = jnp.concatenate([sin32, jnp.zeros((extra, MLA_ROPE), F32)], axis=0)
    cos_t = jnp.concatenate([jnp.ones((rows_total, MLA_NOPE), F32), cos32,
                             jnp.zeros((rows_total, HEAD_PAD - MLA_QK), F32)], axis=-1)
    sin_t = jnp.concatenate([jnp.zeros((rows_total, MLA_NOPE), F32), sin32,
                             jnp.zeros((rows_total, HEAD_PAD - MLA_QK), F32)], axis=-1)
    return jnp.stack([cos_t, sin_t])


def kernel(x, c, ctx, c_ctx, ada_w, ada_b, ev_w_in, hgrn_lb_logits, hgrn_norm_w, pool_w, pool_scale,
           ev_w_out, od_w_in, conv_dw_w, conv_dw_b, conv_ln_w, conv_ln_b, mla_q_a_norm, mla_w_uq,
           mla_kv_a_norm, mla_w_ukv, mla_q_norm, mla_k_norm, od_w_out, moe_router_w, moe_router_b,
           moe_w1, moe_b1, moe_w2, moe_b2):
    b, ll, d = x.shape
    lc = ctx.shape[1]
    s = ll + lc
    nc, nl = lc // ROW_TILE, ll // ROW_TILE
    assert ada_w.shape[0] == 2 and d == D_MODEL
    assert ll % Q_TILE == 0 and lc % ROW_TILE == 0 and (b * s) % ROUTE_TILE == 0 and (b * ll) % ROUTE_TILE == 0

    cond_rows = (b + 1 + 7) // 8 * 8
    cond = jnp.zeros((cond_rows, d), F32).at[:b].set(c).at[b].set(c_ctx)
    mod = _ada_table(cond, ada_w, ada_b)
    modtab = []
    for layer in range(2):
        lat = mod[layer, :b].reshape(b, 1, 6, d)
        con = jnp.broadcast_to(mod[layer, b].reshape(1, 1, 6, d), (b, 1, 6, d))
        modtab.append(jnp.concatenate([con, lat], axis=1))
    lb_all = jnp.cumsum(jax.nn.softmax(hgrn_lb_logits.astype(F32), axis=0), axis=0)

    def router_pieces(w):
        hi, mid, lo = _split3(w.T)
        return jnp.stack([hi, mid, lo])

    sel0 = lambda j: jnp.where(j < nc, 0, 1)
    h = jnp.concatenate([ctx, x], axis=1)
    p0 = _modproj(h, modtab[0], ev_w_in[0].astype(BF16), sel0)
    hg = _hgrn(p0, lb_all[0], hgrn_norm_w[0], nc, nl)
    yp = _pool(p0, pool_w[0], pool_scale[0], lc, ll)
    wo = ev_w_out[0].astype(BF16)
    h1, m, lg = _mixout(hg, yp, h, modtab[0], wo[:HG_W], wo[HG_W:], router_pieces(moe_router_w[0]),
                        moe_router_b[0].reshape(1, N_EXPERTS), sel0, 0, s)
    to_l1 = lambda j: jnp.where(j < nc, j + nl, j - nc)
    h = _moe(m, lg, h1, modtab[0], sel0, moe_w1[0].astype(BF16), moe_b1[0], moe_w2[0].astype(BF16),
             moe_b2[0], s, to_l1)

    sel1 = lambda j: jnp.where(j < nl, 1, 0)
    w_in = od_w_in[0]
    kr_cols = w_in[:, 1664:1696]
    w1p = jnp.zeros((d, OD_IN_PAD), F32)
    w1p = w1p.at[:, :1024].set(w_in[:, :1024]).at[:, OD_CQ:OD_CQ + MLA_Q_RANK].set(w_in[:, 1024:1408])
    w1p = w1p.at[:, OD_CKV:OD_CKV + MLA_KV_RANK].set(w_in[:, 1408:1664])
    w1p = w1p.at[:, OD_KR:OD_KR + MLA_ROPE].set(kr_cols)
    w1p = w1p.at[:, OD_KR + MLA_ROPE:OD_KR + 2 * MLA_ROPE].set(_swap_pairs(kr_cols))
    p1 = _modproj(h, modtab[1], w1p.astype(BF16), sel1)
    hc = _conv(p1, conv_dw_w[0], conv_dw_b[0], conv_ln_w[0], conv_ln_b[0], ll)

    wuq = mla_w_uq[0].reshape(MLA_Q_RANK, MLA_HEADS, MLA_QK)
    wuq_rope = wuq[..., MLA_NOPE:]
    pad_rows = lambda w: jnp.zeros((512, w.shape[1]), F32).at[:MLA_Q_RANK].set(w)
    wq = pad_rows(_place([(wuq, 0)], (MLA_Q_RANK,))).astype(BF16)
    wqs = pad_rows(_place([(_swap_pairs(wuq_rope), MLA_NOPE)], (MLA_Q_RANK,))).astype(BF16)
    wukv = mla_w_ukv[0].reshape(MLA_KV_RANK, MLA_HEADS, MLA_NOPE + MLA_V)
    wk = _place([(wukv[..., :MLA_NOPE], 0)], (MLA_KV_RANK,)).astype(BF16)
    wv = _place([(wukv[..., MLA_NOPE:], 0)], (MLA_KV_RANK,)).astype(BF16)
    eye = jnp.broadcast_to(jnp.eye(MLA_ROPE, dtype=F32)[:, None, :], (MLA_ROPE, MLA_HEADS, MLA_ROPE))
    zero_rows = jnp.zeros((LANES - 2 * MLA_ROPE, MLA_HEADS * HEAD_PAD), F32)
    place_kr = _place([(eye, MLA_NOPE)], (MLA_ROPE,))
    zero_kr = jnp.zeros_like(place_kr)
    ek = jnp.concatenate([place_kr, zero_kr, zero_rows], axis=0).astype(BF16)
    eks = jnp.concatenate([zero_kr, place_kr, zero_rows], axis=0).astype(BF16)
    lane_gain = lambda g: jnp.concatenate([g, jnp.zeros((HEAD_PAD - MLA_QK,), F32)]).reshape(1, HEAD_PAD)
    swapped_gain = lambda g: jnp.concatenate(
        [jnp.zeros((MLA_NOPE,), F32), _swap_pairs(g[MLA_NOPE:]), jnp.zeros((HEAD_PAD - MLA_QK,), F32)]
    ).reshape(1, HEAD_PAD)
    qa = jnp.concatenate([mla_q_a_norm[0], jnp.zeros((512 - MLA_Q_RANK,), F32)]).reshape(1, 512)
    q, k, v = _mla_proj(p1, _rope_tables(s, ll), qa, mla_kv_a_norm[0].reshape(1, MLA_KV_RANK),
                        wq, wqs, wk, wv, ek, eks,
                        lane_gain(mla_q_norm[0]), swapped_gain(mla_q_norm[0]),
                        lane_gain(mla_k_norm[0]), swapped_gain(mla_k_norm[0]))
    attn = _attention(q, k, v, ll)

    wo1 = od_w_out[0]
    wo_attn = wo1[CONV_WIDTH:].reshape(MLA_HEADS, MLA_V, d)
    wo_attn = jnp.zeros((MLA_HEADS, HEAD_PAD, d), F32).at[:, :MLA_V].set(wo_attn).reshape(-1, d)
    h1, m, lg = _mixout(hc, attn, h, modtab[1], wo1[:CONV_WIDTH].astype(BF16), wo_attn.astype(BF16),
                        router_pieces(moe_router_w[1]), moe_router_b[1].reshape(1, N_EXPERTS),
                        lambda j: 1, 0, ll)
    return _moe(m, lg, h1, modtab[1], lambda j: 1, moe_w1[1].astype(BF16), moe_b1[1],
                moe_w2[1].astype(BF16), moe_b2[1], ll, lambda j: j)
```

```python
import functools

import jax
import jax.numpy as jnp
from jax import lax
from jax.experimental import pallas as pl
from jax.experimental.pallas import tpu as pltpu

F32 = jnp.float32
BF16 = jnp.bfloat16

D_MODEL = 1024
GRID_W = 64
EPS = 1e-6
HG_HEADS = 4
HG_DIM = 128
HG_W = HG_HEADS * HG_DIM
HG_SUB = 32
POOL_WINDOWS = (2, 4, 8, 16)
POOL_GROUP = 128
POOL_PAD = 16
CONV_WIDTH = 512
CONV_K = 31
CONV_PAD = 16
MLA_HEADS = 8
MLA_Q_RANK = 384
MLA_KV_RANK = 256
MLA_NOPE = 64
MLA_ROPE = 32
MLA_V = 64
MLA_QK = MLA_NOPE + MLA_ROPE
HEAD_PAD = 128
ROPE_AXIS_HALF = MLA_ROPE // 4
ROPE_BASE = 10000.0
N_EXPERTS = 32
TOP_K = 4
D_FF = 1024
SWIGLU_LIMIT = 7.0
SWIGLU_ALPHA = 1.702

ROW_TILE = 256
ROUTE_TILE = 512
MOE_TILE = 512
Q_TILE = 512
LANES = 128
VMEM_LIMIT = 56 * 1024 * 1024

OD_CQ = 1024
OD_CKV = 1536
OD_KR = 1792
OD_IN_PAD = 1920


def _cp(sem, vmem=VMEM_LIMIT):
    return pltpu.CompilerParams(dimension_semantics=sem, vmem_limit_bytes=vmem)


def _dot(a, b):
    return jnp.dot(a, b, preferred_element_type=F32)


def _dot_nt(a, b):
    return lax.dot_general(a, b, (((1,), (1,)), ((), ())), preferred_element_type=F32)


def _dot_tn(a, b):
    return lax.dot_general(a, b, (((0,), (0,)), ((), ())), preferred_element_type=F32)


def _split3(x):
    hi = x.astype(BF16)
    r1 = x - hi.astype(F32)
    mid = r1.astype(BF16)
    lo = (r1 - mid.astype(F32)).astype(BF16)
    return hi, mid, lo


def _sigmoid(x):
    return 1.0 / (1.0 + jnp.exp(-x))


def _silu(x):
    return x * _sigmoid(x)


def _rms_scale(x, width):
    return lax.rsqrt(jnp.sum(x * x, axis=-1, keepdims=True) * (1.0 / width) + EPS)


def _ada_kernel(c_ref, w_ref, b_ref, o_ref):
    s = _silu(c_ref[...])
    sh, sm, sl = _split3(s)
    wh, wm, wl = _split3(w_ref[0])
    acc = _dot(sh, wh) + _dot(sh, wm) + _dot(sm, wh) + _dot(sh, wl) + _dot(sl, wh) + _dot(sm, wm)
    o_ref[0] = acc + b_ref[0]


def _ada_table(cond, ada_w, ada_b):
    depth, d, n = ada_w.shape
    rows = cond.shape[0]
    nb = n // d
    return pl.pallas_call(
        _ada_kernel,
        grid=(depth, nb),
        in_specs=[
            pl.BlockSpec((rows, d), lambda l, j: (0, 0)),
            pl.BlockSpec((1, d, d), lambda l, j: (l, 0, j)),
            pl.BlockSpec((1, 1, d), lambda l, j: (l, 0, j)),
        ],
        out_specs=pl.BlockSpec((1, rows, d), lambda l, j: (l, 0, j)),
        out_shape=jax.ShapeDtypeStruct((depth, rows, n), F32),
        compiler_params=_cp(("arbitrary", "arbitrary")),
    )(cond, ada_w, ada_b.reshape(depth, 1, n))


def _modproj_kernel(h_ref, mod_ref, w_ref, o_ref):
    x = h_ref[0]
    mod = mod_ref[0, 0]
    u = x * _rms_scale(x, x.shape[-1]) * (1.0 + mod[1:2]) + mod[0:1]
    o_ref[0] = _dot(u.astype(BF16), w_ref[...])


def _modproj(h, modtab, w, sel):
    b, s, d = h.shape
    n = w.shape[1]
    return pl.pallas_call(
        _modproj_kernel,
        grid=(b, s // ROW_TILE),
        in_specs=[
            pl.BlockSpec((1, ROW_TILE, d), lambda i, j: (i, j, 0)),
            pl.BlockSpec((1, 1, 6, d), lambda i, j: (i, sel(j), 0, 0)),
            pl.BlockSpec((d, n), lambda i, j: (0, 0)),
        ],
        out_specs=pl.BlockSpec((1, ROW_TILE, n), lambda i, j: (i, j, 0)),
        out_shape=jax.ShapeDtypeStruct((b, s, n), F32),
        compiler_params=_cp(("arbitrary", "arbitrary")),
    )(h, modtab, w)


def _hgrn_block(q, z, v, lb, st, reverse):
    rows = q.shape[0]
    nsub = rows // HG_SUB
    f = lb + (1.0 - lb) * _sigmoid(z)
    kin = 1.0 - f
    logf = jnp.log(f)
    ri = lax.broadcasted_iota(jnp.int32, (rows, rows), 0)
    ci = lax.broadcasted_iota(jnp.int32, (rows, rows), 1)
    same = (ri // HG_SUB) == (ci // HG_SUB)
    causal = same & ((ci >= ri) if reverse else (ci <= ri))
    tri = jnp.where(causal, 1.0, 0.0).astype(BF16)
    lh, lm, ll = _split3(logf)
    bcum = _dot(tri, lh) + _dot(tri, lm) + _dot(tri, ll)
    b3 = bcum.reshape(nsub, HG_SUB, HG_DIM)
    tot_row = 0 if reverse else HG_SUB - 1
    mid_row = HG_SUB // 2 if reverse else HG_SUB // 2 - 1
    tot = b3[:, tot_row:tot_row + 1, :]
    mid = b3[:, mid_row:mid_row + 1, :]
    q3 = q.reshape(nsub, HG_SUB, HG_DIM)
    k3 = kin.reshape(nsub, HG_SUB, HG_DIM)
    q_dec = (q3 * jnp.exp(b3)).astype(BF16)
    k_end = (k3 * jnp.exp(tot - b3)).astype(BF16)
    q_mid = (q3 * jnp.exp(b3 - mid)).reshape(rows, HG_DIM).astype(BF16)
    k_mid = (k3 * jnp.exp(mid - b3)).reshape(rows, HG_DIM).astype(BF16)
    vb = v.astype(BF16)
    att = jnp.where(causal, _dot_nt(q_mid, k_mid), 0.0).astype(BF16)
    o_intra = _dot(att, vb)
    v3 = vb.reshape(nsub, HG_SUB, HG_DIM)
    dec = jnp.exp(tot)
    upd = [_dot_tn(v3[j], k_end[j]) for j in range(nsub)]
    order = range(nsub - 1, -1, -1) if reverse else range(nsub)
    o_inter = [None] * nsub
    for j in order:
        o_inter[j] = _dot_nt(q_dec[j], st.astype(BF16))
        st = st * dec[j] + upd[j]
    o = o_intra + jnp.concatenate(o_inter, axis=0)
    return o, st


def _hgrn_kernel(q_ref, zf_ref, zb_ref, v_ref, og_ref, lb_ref, nw_ref, o_ref, acc_ref, *, nc, nl):
    blk = ROW_TILE
    nblk = nc + nl
    acc_ref[...] = jnp.zeros_like(acc_ref)
    lb_f = lb_ref[0:1, :]
    lb_b = lb_ref[1:2, :]

    def body(i, carry):
        st_f, st_b = carry
        rf = pl.multiple_of(i * blk, blk)
        ib = jnp.where(i < nc, nc - 1 - i, 2 * nc + nl - 1 - i)
        rb = pl.multiple_of(ib * blk, blk)
        qf = _silu(q_ref[0, pl.ds(rf, blk), :])
        o_f, st_f = _hgrn_block(qf, zf_ref[0, pl.ds(rf, blk), :], v_ref[0, pl.ds(rf, blk), :],
                                lb_f, st_f, False)
        acc_ref[pl.ds(rf, blk), :] += o_f
        qb = _silu(q_ref[0, pl.ds(rb, blk), :])
        o_b, st_b = _hgrn_block(qb, zb_ref[0, pl.ds(rb, blk), :], v_ref[0, pl.ds(rb, blk), :],
                                lb_b, st_b, True)
        acc_ref[pl.ds(rb, blk), :] += o_b
        return st_f, st_b

    zero = jnp.zeros((HG_DIM, HG_DIM), F32)
    lax.fori_loop(0, nblk, body, (zero, zero))
    o = acc_ref[...]
    og = og_ref[0]
    o_ref[0] = o * _rms_scale(o, HG_DIM) * nw_ref[...] * _silu(og)


def _hgrn(p, lb, norm_w, nc, nl):
    b, s, _ = p.shape
    sec = lambda k: pl.BlockSpec((1, s, HG_DIM), lambda i, h, k=k: (i, 0, HG_HEADS * k + h))
    return pl.pallas_call(
        functools.partial(_hgrn_kernel, nc=nc, nl=nl),
        grid=(b, HG_HEADS),
        in_specs=[sec(0), sec(1), sec(2), sec(3), sec(4),
                  pl.BlockSpec((2, HG_DIM), lambda i, h: (0, h)),
                  pl.BlockSpec((1, HG_DIM), lambda i, h: (0, 0))],
        out_specs=pl.BlockSpec((1, s, HG_DIM), lambda i, h: (i, 0, h)),
        out_shape=jax.ShapeDtypeStruct((b, s, HG_W), F32),
        scratch_shapes=[pltpu.VMEM((s, HG_DIM), F32)],
        compiler_params=_cp(("arbitrary", "arbitrary")),
    )(p, p, p, p, p, lb, norm_w.reshape(1, HG_DIM))


def _pool_kernel(x_ref, w_ref, sc_ref, o_ref, *, lc, ll):
    z = jnp.zeros((POOL_PAD, POOL_GROUP), F32)
    n = lc + ll + 3 * POOL_PAD
    pos_c = lax.broadcasted_iota(jnp.int32, (lc, 1), 0)
    pos_l = lax.broadcasted_iota(jnp.int32, (ll, 1), 0)
    for g, w in enumerate(POOL_WINDOWS):
        x = x_ref[0, :, g * POOL_GROUP:(g + 1) * POOL_GROUP]
        xp = jnp.concatenate([z, x[:lc], z, x[lc:], z], axis=0)
        acc = xp + pltpu.roll(xp, 1, 0)
        half = 1
        while 2 * half < w:
            acc = pltpu.roll(acc, half, 0) + pltpu.roll(acc, n - half, 0)
            half *= 2
        win = jnp.concatenate([acc[POOL_PAD:POOL_PAD + lc],
                               acc[2 * POOL_PAD + lc:2 * POOL_PAD + lc + ll]], axis=0)

        def count(pos, length):
            lo = jnp.maximum(pos - w // 2, 0)
            hi = jnp.minimum(pos + w - w // 2, length)
            return (hi - lo).astype(F32)

        cnt = jnp.concatenate([count(pos_c, lc), count(pos_l, ll)], axis=0)
        d = win / cnt - x
        y = _dot(d.astype(BF16), w_ref[g])
        o_ref[0, :, g * POOL_GROUP:(g + 1) * POOL_GROUP] = y * sc_ref[:, g * POOL_GROUP:(g + 1) * POOL_GROUP]


def _pool(p, pool_w, pool_scale, lc, ll):
    b, s, n = p.shape
    width = len(POOL_WINDOWS) * POOL_GROUP
    return pl.pallas_call(
        functools.partial(_pool_kernel, lc=lc, ll=ll),
        grid=(b,),
        in_specs=[pl.BlockSpec((1, s, width), lambda i: (i, 0, n // width - 1)),
                  pl.BlockSpec(pool_w.shape, lambda i: (0, 0, 0)),
                  pl.BlockSpec((1, width), lambda i: (0, 0))],
        out_specs=pl.BlockSpec((1, s, width), lambda i: (i, 0, 0)),
        out_shape=jax.ShapeDtypeStruct((b, s, width), F32),
        compiler_params=_cp(("arbitrary",)),
    )(p, pool_w.astype(BF16), pool_scale.reshape(1, width))


def _mixout_kernel(a_ref, b_ref, h_ref, mod_ref, wa_ref, wb_ref, wr_ref, br_ref,
                   h1_ref, m_ref, lg_ref):
    mod = mod_ref[0, 0]
    y = _dot(a_ref[0].astype(BF16), wa_ref[...]) + _dot(b_ref[0].astype(BF16), wb_ref[...])
    h1 = h_ref[0] + mod[2:3] * y
    h1_ref[0] = h1
    m = h1 * _rms_scale(h1, h1.shape[-1]) * (1.0 + mod[4:5]) + mod[3:4]
    m_ref[0] = m
    mh, mm, ml = _split3(m)
    wh, wm, wl = wr_ref[0], wr_ref[1], wr_ref[2]
    lg = (_dot_nt(mh, wh) + _dot_nt(mh, wm) + _dot_nt(mm, wh)
          + _dot_nt(mh, wl) + _dot_nt(ml, wh) + _dot_nt(mm, wm))
    lg_ref[0] = lg + br_ref[...]


def _mixout(a, bmix, h, modtab, wa, wb, wr3, br, sel, h_off, rows):
    b = h.shape[0]
    d = h.shape[-1]
    ka, kb = a.shape[-1], bmix.shape[-1]
    out3 = lambda n: jax.ShapeDtypeStruct((b, rows, n), F32)
    return pl.pallas_call(
        _mixout_kernel,
        grid=(b, rows // ROW_TILE),
        in_specs=[
            pl.BlockSpec((1, ROW_TILE, ka), lambda i, j: (i, j, 0)),
            pl.BlockSpec((1, ROW_TILE, kb), lambda i, j: (i, j, 0)),
            pl.BlockSpec((1, ROW_TILE, d), lambda i, j: (i, j + h_off, 0)),
            pl.BlockSpec((1, 1, 6, d), lambda i, j: (i, sel(j), 0, 0)),
            pl.BlockSpec((ka, d), lambda i, j: (0, 0)),
            pl.BlockSpec((kb, d), lambda i, j: (0, 0)),
            pl.BlockSpec((3, N_EXPERTS, d), lambda i, j: (0, 0, 0)),
            pl.BlockSpec((1, N_EXPERTS), lambda i, j: (0, 0)),
        ],
        out_specs=[
            pl.BlockSpec((1, ROW_TILE, d), lambda i, j: (i, j, 0)),
            pl.BlockSpec((1, ROW_TILE, d), lambda i, j: (i, j, 0)),
            pl.BlockSpec((1, ROW_TILE, N_EXPERTS), lambda i, j: (i, j, 0)),
        ],
        out_shape=[out3(d), out3(d), out3(N_EXPERTS)],
        compiler_params=_cp(("arbitrary", "arbitrary")),
    )(a, bmix, h, modtab, wa, wb, wr3, br)


def _route_kernel(lg_ref, idx_ref, gate_ref, rank_ref, cnt_ref):
    t = lg_ref.shape[0]

    @pl.when(pl.program_id(0) == 0)
    def _():
        cnt_ref[...] = jnp.zeros_like(cnt_ref)

    lg = lg_ref[...]
    lane = lax.broadcasted_iota(jnp.int32, (t, N_EXPERTS), 1).astype(F32)
    hot, vals, idxs = [], [], []
    for _ in range(TOP_K):
        mx = jnp.max(lg, axis=-1, keepdims=True)
        ix = jnp.min(jnp.where(lg == mx, lane, float(N_EXPERTS)), axis=-1, keepdims=True)
        oh = lane == ix
        hot.append(oh)
        vals.append(mx)
        idxs.append(ix.astype(jnp.int32))
        lg = jnp.where(oh, -jnp.inf, lg)
    ex = [jnp.exp(v - vals[0]) for v in vals]
    den = ex[0] + ex[1] + ex[2] + ex[3]
    sel = jnp.where(hot[0] | hot[1] | hot[2] | hot[3], 1.0, 0.0)
    ri = lax.broadcasted_iota(jnp.int32, (t, t), 0)
    ci = lax.broadcasted_iota(jnp.int32, (t, t), 1)
    before = jnp.where(ci < ri, 1.0, 0.0).astype(BF16)
    base = _dot(before, sel.astype(BF16)) + cnt_ref[...]
    out_lane = lax.broadcasted_iota(jnp.int32, (t, LANES), 1)
    idx_o = jnp.zeros((t, LANES), jnp.int32)
    gate_o = jnp.zeros((t, LANES), F32)
    rank_o = jnp.zeros((t, LANES), jnp.int32)
    for k in range(TOP_K):
        rk = jnp.sum(jnp.where(hot[k], base, 0.0), axis=-1, keepdims=True).astype(jnp.int32)
        idx_o = jnp.where(out_lane == k, idxs[k], idx_o)
        gate_o = jnp.where(out_lane == k, ex[k] / den, gate_o)
        rank_o = jnp.where(out_lane == k, rk, rank_o)
    idx_ref[...] = idx_o
    gate_ref[...] = gate_o
    rank_ref[...] = rank_o
    cnt_ref[...] += jnp.sum(sel, axis=0, keepdims=True)


def _route(logits):
    t = logits.shape[0]
    tile = pl.BlockSpec((ROUTE_TILE, LANES), lambda i: (i, 0))
    return pl.pallas_call(
        _route_kernel,
        grid=(t // ROUTE_TILE,),
        in_specs=[pl.BlockSpec((ROUTE_TILE, N_EXPERTS), lambda i: (i, 0))],
        out_specs=[tile, tile, tile, pl.BlockSpec((1, N_EXPERTS), lambda i: (0, 0))],
        out_shape=[jax.ShapeDtypeStruct((t, LANES), jnp.int32),
                   jax.ShapeDtypeStruct((t, LANES), F32),
                   jax.ShapeDtypeStruct((t, LANES), jnp.int32),
                   jax.ShapeDtypeStruct((1, N_EXPERTS), F32)],
        compiler_params=_cp(("arbitrary",)),
    )(logits)


def _dispatch_kernel(dest_ref, m_ref, xs_ref, sem):
    tile = m_ref.shape[0]

    def copy(r, k):
        return pltpu.make_async_copy(m_ref.at[pl.ds(r, 1)], xs_ref.at[pl.ds(dest_ref[r * TOP_K + k], 1)], sem)

    def issue(r, c):
        for k in range(TOP_K):
            copy(r, k).start()
        return c

    lax.fori_loop(0, tile, issue, 0)

    def drain(r, c):
        for k in range(TOP_K):
            copy(r, k).wait()
        return c

    lax.fori_loop(0, tile, drain, 0)


def _dispatch(m, dest_flat, n_rows):
    t, d = m.shape
    tile = ROUTE_TILE
    return pl.pallas_call(
        _dispatch_kernel,
        grid=(t // tile,),
        in_specs=[pl.BlockSpec((tile * TOP_K,), lambda i: (i,), memory_space=pltpu.SMEM),
                  pl.BlockSpec((tile, d), lambda i: (i, 0))],
        out_specs=pl.BlockSpec(memory_space=pl.ANY),
        out_shape=jax.ShapeDtypeStruct((n_rows, d), F32),
        scratch_shapes=[pltpu.SemaphoreType.DMA(())],
        compiler_params=_cp(("arbitrary",)),
    )(dest_flat, m)


def _experts_kernel(be_ref, nv_ref, x_ref, w1_ref, b1_ref, w2_ref, b2_ref, y_ref):
    i = pl.program_id(0)
    nv = nv_ref[i]

    @pl.when(nv > 0)
    def _():
        rows = lax.broadcasted_iota(jnp.int32, (x_ref.shape[0], 1), 0)
        x = jnp.where(rows < nv, x_ref[...], 0.0).astype(BF16)
        hb = _dot(x, w1_ref[0]) + b1_ref[0]
        gl = jnp.minimum(hb[:, :D_FF], SWIGLU_LIMIT)
        lin = jnp.clip(hb[:, D_FF:], -SWIGLU_LIMIT, SWIGLU_LIMIT)
        a = gl * _sigmoid(SWIGLU_ALPHA * gl) * (lin + 1.0)
        y_ref[...] = _dot(a.astype(BF16), w2_ref[0]) + b2_ref[0]

    @pl.when(nv <= 0)
    def _():
        y_ref[...] = jnp.zeros_like(y_ref)


def _experts(xs, blk_e, blk_nv, w1, b1, w2, b2):
    p, d = xs.shape
    e, _, f2 = w1.shape
    grid_spec = pltpu.PrefetchScalarGridSpec(
        num_scalar_prefetch=2,
        grid=(p // MOE_TILE,),
        in_specs=[
            pl.BlockSpec((MOE_TILE, d), lambda i, be, nv: (i, 0)),
            pl.BlockSpec((1, d, f2), lambda i, be, nv: (be[i], 0, 0)),
            pl.BlockSpec((1, 1, f2), lambda i, be, nv: (be[i], 0, 0)),
            pl.BlockSpec((1, f2 // 2, d), lambda i, be, nv: (be[i], 0, 0)),
            pl.BlockSpec((1, 1, d), lambda i, be, nv: (be[i], 0, 0)),
        ],
        out_specs=pl.BlockSpec((MOE_TILE, d), lambda i, be, nv: (i, 0)),
    )
    return pl.pallas_call(
        _experts_kernel,
        grid_spec=grid_spec,
        out_shape=jax.ShapeDtypeStruct((p, d), F32),
        compiler_params=_cp(("arbitrary",)),
    )(blk_e, blk_nv, xs, w1, b1.reshape(e, 1, f2), w2, b2.reshape(e, 1, d))


def _combine_kernel(dest_ref, ys_ref, gate_ref, h_ref, mod_ref, o_ref, buf_ref, sem):
    tile = h_ref.shape[1]

    def issue(r, c):
        for k in range(TOP_K):
            pltpu.make_async_copy(ys_ref.at[pl.ds(dest_ref[r * TOP_K + k], 1)],
                                  buf_ref.at[k, pl.ds(r, 1)], sem).start()
        return c

    lax.fori_loop(0, tile, issue, 0)

    def drain(r, c):
        for k in range(TOP_K):
            pltpu.make_async_copy(ys_ref.at[pl.ds(dest_ref[r * TOP_K + k], 1)],
                                  buf_ref.at[k, pl.ds(r, 1)], sem).wait()
        return c

    lax.fori_loop(0, tile, drain, 0)
    gate = gate_ref[...]
    f = gate[:, 0:1] * buf_ref[0]
    for k in range(1, TOP_K):
        f = f + gate[:, k:k + 1] * buf_ref[k]
    o_ref[0] = h_ref[0] + mod_ref[0, 0][5:6] * f


def _combine(ys, dest_flat, gate, h1, modtab, sel, out_rows, out_map):
    b, r, d = h1.shape
    nblk = r // ROW_TILE
    return pl.pallas_call(
        _combine_kernel,
        grid=(b, nblk),
        in_specs=[
            pl.BlockSpec((ROW_TILE * TOP_K,), lambda i, j: (i * nblk + j,), memory_space=pltpu.SMEM),
            pl.BlockSpec(memory_space=pl.ANY),
            pl.BlockSpec((ROW_TILE, LANES), lambda i, j: (i * nblk + j, 0)),
            pl.BlockSpec((1, ROW_TILE, d), lambda i, j: (i, j, 0)),
            pl.BlockSpec((1, 1, 6, d), lambda i, j: (i, sel(j), 0, 0)),
        ],
        out_specs=pl.BlockSpec((1, ROW_TILE, d), lambda i, j: (i, out_map(j), 0)),
        out_shape=jax.ShapeDtypeStruct((b, out_rows, d), F32),
        scratch_shapes=[pltpu.VMEM((TOP_K, ROW_TILE, d), F32), pltpu.SemaphoreType.DMA(())],
        compiler_params=_cp(("arbitrary", "arbitrary")),
    )(dest_flat, ys, gate, h1, modtab)


def _moe(m, logits, h1, modtab, sel, w1, b1, w2, b2, out_rows, out_map):
    b, r, d = m.shape
    t = b * r
    idx, gate, rank, counts = _route(logits.reshape(t, N_EXPERTS))
    top_i = idx[:, :TOP_K]
    counts = counts[0].astype(jnp.int32)
    padded = (counts + MOE_TILE - 1) // MOE_TILE * MOE_TILE
    pad_end = jnp.cumsum(padded)
    pad_start = pad_end - padded
    dest = (pad_start[top_i] + rank[:, :TOP_K]).reshape(-1)
    n_blocks = (t * TOP_K) // MOE_TILE + N_EXPERTS
    blk_start = jnp.arange(n_blocks, dtype=jnp.int32) * MOE_TILE
    blk_e = jnp.minimum(jnp.sum(blk_start[:, None] >= pad_end[None, :], axis=1), N_EXPERTS - 1).astype(jnp.int32)
    blk_nv = jnp.clip(counts[blk_e] - (blk_start - pad_start[blk_e]), 0, MOE_TILE).astype(jnp.int32)
    xs = _dispatch(m.reshape(t, d), dest, n_blocks * MOE_TILE)
    ys = _experts(xs, blk_e, blk_nv, w1, b1, w2, b2)
    return _combine(ys, dest, gate, h1, modtab, sel, out_rows, out_map)


def _conv_kernel(a_ref, dw_ref, db_ref, lw_ref, lb_ref, o_ref, hs_ref, *, ll):
    hs_ref[0:CONV_PAD, :] = jnp.zeros((CONV_PAD, CONV_WIDTH), F32)
    hs_ref[CONV_PAD + ll:2 * CONV_PAD + ll, :] = jnp.zeros((CONV_PAD, CONV_WIDTH), F32)
    hs_ref[CONV_PAD:CONV_PAD + ll, :] = a_ref[0, :, :CONV_WIDTH] * _sigmoid(a_ref[0, :, CONV_WIDTH:])
    first = CONV_PAD - CONV_K // 2

    def tile(i, c):
        r0 = pl.multiple_of(i * ROW_TILE, ROW_TILE)
        win = hs_ref[pl.ds(r0, ROW_TILE + 2 * CONV_PAD), :]
        acc = jnp.zeros((ROW_TILE, CONV_WIDTH), F32)
        for k in range(CONV_K):
            acc = acc + dw_ref[k:k + 1, :] * win[first + k:first + k + ROW_TILE]
        acc = acc + db_ref[...]
        mu = jnp.mean(acc, axis=-1, keepdims=True)
        dlt = acc - mu
        var = jnp.mean(dlt * dlt, axis=-1, keepdims=True)
        y = dlt * lax.rsqrt(var + EPS) * lw_ref[...] + lb_ref[...]
        o_ref[0, pl.ds(r0, ROW_TILE), :] = _silu(y)
        return c

    lax.fori_loop(0, ll // ROW_TILE, tile, 0)


def _conv(p, dw_w, dw_b, ln_w, ln_b, ll):
    b = p.shape[0]
    row = lambda v: v.reshape(1, CONV_WIDTH)
    vec = pl.BlockSpec((1, CONV_WIDTH), lambda i: (0, 0))
    return pl.pallas_call(
        functools.partial(_conv_kernel, ll=ll),
        grid=(b,),
        in_specs=[pl.BlockSpec((1, ll, 2 * CONV_WIDTH), lambda i: (i, 0, 0)),
                  pl.BlockSpec((CONV_K, CONV_WIDTH), lambda i: (0, 0)), vec, vec, vec],
        out_specs=pl.BlockSpec((1, ll, CONV_WIDTH), lambda i: (i, 0, 0)),
        out_shape=jax.ShapeDtypeStruct((b, ll, CONV_WIDTH), F32),
        scratch_shapes=[pltpu.VMEM((ll + 2 * CONV_PAD, CONV_WIDTH), F32)],
        compiler_params=_cp(("arbitrary",)),
    )(p, dw_w, row(dw_b), row(ln_w), row(ln_b))


def _mla_proj_kernel(cq_ref, ckv_ref, kr_ref, cs_ref, qa_ref, kva_ref, wq_ref, wqs_ref, wk_ref,
                     wv_ref, ek_ref, eks_ref, gq_ref, gqs_ref, gk_ref, gks_ref,
                     q_ref, k_ref, v_ref):
    cos = cs_ref[0]
    sin = cs_ref[1]
    cq = cq_ref[0]
    cqn = (cq * _rms_scale(cq, MLA_Q_RANK) * qa_ref[...]).astype(BF16)
    qx = _dot(cqn, wq_ref[...])
    qs = _dot(cqn, wqs_ref[...])
    ckv = ckv_ref[0]
    ckvn = (ckv * _rms_scale(ckv, MLA_KV_RANK) * kva_ref[...]).astype(BF16)
    kr = kr_ref[0]
    rh = kr.astype(BF16)
    rl = (kr - rh.astype(F32)).astype(BF16)
    kx = _dot(ckvn, wk_ref[...]) + _dot(rh, ek_ref[...]) + _dot(rl, ek_ref[...])
    ks = _dot(rh, eks_ref[...]) + _dot(rl, eks_ref[...])
    v_ref[0] = _dot(ckvn, wv_ref[...]).astype(BF16)
    scale = MLA_QK ** -0.5
    for h in range(MLA_HEADS):
        sl = slice(h * HEAD_PAD, (h + 1) * HEAD_PAD)
        qh = qx[:, sl]
        q_ref[0, :, sl] = (_rms_scale(qh, MLA_QK) * scale
                           * (qh * gq_ref[...] * cos + qs[:, sl] * gqs_ref[...] * sin)).astype(BF16)
        kh = kx[:, sl]
        k_ref[0, :, sl] = (_rms_scale(kh, MLA_QK)
                           * (kh * gk_ref[...] * cos + ks[:, sl] * gks_ref[...] * sin)).astype(BF16)


def _mla_proj(p, cs, qa, kva, wq, wqs, wk, wv, ek, eks, gq, gqs, gk, gks):
    b, s, _ = p.shape
    hw = MLA_HEADS * HEAD_PAD
    full = lambda a: pl.BlockSpec(a.shape, lambda i, j: (0,) * a.ndim)
    out = pl.BlockSpec((1, ROW_TILE, hw), lambda i, j: (i, j, 0))
    consts = (qa, kva, wq, wqs, wk, wv, ek, eks, gq, gqs, gk, gks)
    return pl.pallas_call(
        _mla_proj_kernel,
        grid=(b, s // ROW_TILE),
        in_specs=[pl.BlockSpec((1, ROW_TILE, 512), lambda i, j: (i, j, OD_CQ // 512)),
                  pl.BlockSpec((1, ROW_TILE, MLA_KV_RANK), lambda i, j: (i, j, OD_CKV // MLA_KV_RANK)),
                  pl.BlockSpec((1, ROW_TILE, LANES), lambda i, j: (i, j, OD_KR // LANES)),
                  pl.BlockSpec((2, ROW_TILE, HEAD_PAD), lambda i, j: (0, j, 0))]
                 + [full(a) for a in consts],
        out_specs=[out, out, out],
        out_shape=[jax.ShapeDtypeStruct((b, s, hw), BF16)] * 3,
        compiler_params=_cp(("arbitrary", "arbitrary")),
    )(p, p, p, cs, *consts)


def _attn_kernel(q_ref, k_ref, v_ref, o_ref):
    def scores(h):
        sl = slice(h * HEAD_PAD, (h + 1) * HEAD_PAD)
        return _dot_nt(q_ref[0, :, sl], k_ref[0, :, sl])

    s = scores(0)
    for h in range(MLA_HEADS):
        s_next = scores(h + 1) if h + 1 < MLA_HEADS else None
        sl = slice(h * HEAD_PAD, (h + 1) * HEAD_PAD)
        p = jnp.exp(s - jnp.max(s, axis=-1, keepdims=True))
        den = jnp.sum(p, axis=-1, keepdims=True)
        o_ref[0, :, sl] = (_dot(p.astype(BF16), v_ref[0, :, sl]) / den).astype(BF16)
        s = s_next


def _attention(q, k, v, ll):
    b, s, hw = k.shape
    kv = pl.BlockSpec((1, s, hw), lambda i, j: (i, 0, 0))
    qo = pl.BlockSpec((1, Q_TILE, hw), lambda i, j: (i, j, 0))
    return pl.pallas_call(
        _attn_kernel,
        grid=(b, ll // Q_TILE),
        in_specs=[qo, kv, kv],
        out_specs=qo,
        out_shape=jax.ShapeDtypeStruct((b, ll, hw), BF16),
        compiler_params=_cp(("arbitrary", "arbitrary")),
    )(q, k, v)


def _place(parts, lead):
    out = jnp.zeros(lead + (MLA_HEADS, HEAD_PAD), F32)
    for arr, off in parts:
        out = out.at[..., off:off + arr.shape[-1]].set(arr)
    return out.reshape(lead + (MLA_HEADS * HEAD_PAD,))


def _swap_pairs(a):
    g = a.reshape(a.shape[:-1] + (2, 2, ROPE_AXIS_HALF))
    return jnp.flip(g, axis=-2).reshape(a.shape)


def _rope_tables(rows_total, ll):
    grid_rows = ll // GRID_W
    t_row = jnp.repeat(jnp.arange(grid_rows, dtype=F32), GRID_W)
    t_col = jnp.tile(jnp.arange(GRID_W, dtype=F32), grid_rows)
    inv = 1.0 / (ROPE_BASE ** (jnp.arange(ROPE_AXIS_HALF, dtype=F32) / ROPE_AXIS_HALF))
    ang_r = t_row[:, None] * inv
    ang_c = t_col[:, None] * inv
    cos32 = jnp.concatenate([jnp.cos(ang_r), jnp.cos(ang_r), jnp.cos(ang_c), jnp.cos(ang_c)], axis=-1)
    sin32 = jnp.concatenate([-jnp.sin(ang_r), jnp.sin(ang_r), -jnp.sin(ang_c), jnp.sin(ang_c)], axis=-1)
    extra = rows_total - ll
    cos32 = jnp.concatenate([cos32, jnp.ones((extra, MLA_ROPE), F32)], axis=0)
    sin32 = jnp.concatenate([sin32, jnp.zeros((extra, MLA_ROPE), F32)], axis=0)
    cos_t = jnp.concatenate([jnp.ones((rows_total, MLA_NOPE), F32), cos32,
                             jnp.zeros((rows_total, HEAD_PAD - MLA_QK), F32)], axis=-1)
    sin_t = jnp.concatenate([jnp.zeros((rows_total, MLA_NOPE), F32), sin32,
                             jnp.zeros((rows_total, HEAD_PAD - MLA_QK), F32)], axis=-1)
    return jnp.stack([cos_t, sin_t])


def kernel(x, c, ctx, c_ctx, ada_w, ada_b, ev_w_in, hgrn_lb_logits, hgrn_norm_w, pool_w, pool_scale,
           ev_w_out, od_w_in, conv_dw_w, conv_dw_b, conv_ln_w, conv_ln_b, mla_q_a_norm, mla_w_uq,
           mla_kv_a_norm, mla_w_ukv, mla_q_norm, mla_k_norm, od_w_out, moe_router_w, moe_router_b,
           moe_w1, moe_b1, moe_w2, moe_b2):
    b, ll, d = x.shape
    lc = ctx.shape[1]
    s = ll + lc
    nc, nl = lc // ROW_TILE, ll // ROW_TILE
    assert ada_w.shape[0] == 2 and d == D_MODEL
    assert ll % Q_TILE == 0 and lc % ROW_TILE == 0 and (b * s) % ROUTE_TILE == 0 and (b * ll) % ROUTE_TILE == 0

    cond_rows = (b + 1 + 7) // 8 * 8
    cond = jnp.zeros((cond_rows, d), F32).at[:b].set(c).at[b].set(c_ctx)
    mod = _ada_table(cond, ada_w, ada_b)
    modtab = []
    for layer in range(2):
        lat = mod[layer, :b].reshape(b, 1, 6, d)
        con = jnp.broadcast_to(mod[layer, b].reshape(1, 1, 6, d), (b, 1, 6, d))
        modtab.append(jnp.concatenate([con, lat], axis=1))
    lb_all = jnp.cumsum(jax.nn.softmax(hgrn_lb_logits.astype(F32), axis=0), axis=0)

    def router_pieces(w):
        hi, mid, lo = _split3(w.T)
        return jnp.stack([hi, mid, lo])

    sel0 = lambda j: jnp.where(j < nc, 0, 1)
    h = jnp.concatenate([ctx, x], axis=1)
    p0 = _modproj(h, modtab[0], ev_w_in[0].astype(BF16), sel0)
    hg = _hgrn(p0, lb_all[0], hgrn_norm_w[0], nc, nl)
    yp = _pool(p0, pool_w[0], pool_scale[0], lc, ll)
    wo = ev_w_out[0].astype(BF16)
    h1, m, lg = _mixout(hg, yp, h, modtab[0], wo[:HG_W], wo[HG_W:], router_pieces(moe_router_w[0]),
                        moe_router_b[0].reshape(1, N_EXPERTS), sel0, 0, s)
    to_l1 = lambda j: jnp.where(j < nc, j + nl, j - nc)
    h = _moe(m, lg, h1, modtab[0], sel0, moe_w1[0].astype(BF16), moe_b1[0], moe_w2[0].astype(BF16),
             moe_b2[0], s, to_l1)

    sel1 = lambda j: jnp.where(j < nl, 1, 0)
    w_in = od_w_in[0]
    kr_cols = w_in[:, 1664:1696]
    w1p = jnp.zeros((d, OD_IN_PAD), F32)
    w1p = w1p.at[:, :1024].set(w_in[:, :1024]).at[:, OD_CQ:OD_CQ + MLA_Q_RANK].set(w_in[:, 1024:1408])
    w1p = w1p.at[:, OD_CKV:OD_CKV + MLA_KV_RANK].set(w_in[:, 1408:1664])
    w1p = w1p.at[:, OD_KR:OD_KR + MLA_ROPE].set(kr_cols)
    w1p = w1p.at[:, OD_KR + MLA_ROPE:OD_KR + 2 * MLA_ROPE].set(_swap_pairs(kr_cols))
    p1 = _modproj(h, modtab[1], w1p.astype(BF16), sel1)
    hc = _conv(p1, conv_dw_w[0], conv_dw_b[0], conv_ln_w[0], conv_ln_b[0], ll)

    wuq = mla_w_uq[0].reshape(MLA_Q_RANK, MLA_HEADS, MLA_QK)
    wuq_rope = wuq[..., MLA_NOPE:]
    pad_rows = lambda w: jnp.zeros((512, w.shape[1]), F32).at[:MLA_Q_RANK].set(w)
    wq = pad_rows(_place([(wuq, 0)], (MLA_Q_RANK,))).astype(BF16)
    wqs = pad_rows(_place([(_swap_pairs(wuq_rope), MLA_NOPE)], (MLA_Q_RANK,))).astype(BF16)
    wukv = mla_w_ukv[0].reshape(MLA_KV_RANK, MLA_HEADS, MLA_NOPE + MLA_V)
    wk = _place([(wukv[..., :MLA_NOPE], 0)], (MLA_KV_RANK,)).astype(BF16)
    wv = _place([(wukv[..., MLA_NOPE:], 0)], (MLA_KV_RANK,)).astype(BF16)
    eye = jnp.broadcast_to(jnp.eye(MLA_ROPE, dtype=F32)[:, None, :], (MLA_ROPE, MLA_HEADS, MLA_ROPE))
    zero_rows = jnp.zeros((LANES - 2 * MLA_ROPE, MLA_HEADS * HEAD_PAD), F32)
    place_kr = _place([(eye, MLA_NOPE)], (MLA_ROPE,))
    zero_kr = jnp.zeros_like(place_kr)
    ek = jnp.concatenate([place_kr, zero_kr, zero_rows], axis=0).astype(BF16)
    eks = jnp.concatenate([zero_kr, place_kr, zero_rows], axis=0).astype(BF16)
    lane_gain = lambda g: jnp.concatenate([g, jnp.zeros((HEAD_PAD - MLA_QK,), F32)]).reshape(1, HEAD_PAD)
    swapped_gain = lambda g: jnp.concatenate(
        [jnp.zeros((MLA_NOPE,), F32), _swap_pairs(g[MLA_NOPE:]), jnp.zeros((HEAD_PAD - MLA_QK,), F32)]
    ).reshape(1, HEAD_PAD)
    qa = jnp.concatenate([mla_q_a_norm[0], jnp.zeros((512 - MLA_Q_RANK,), F32)]).reshape(1, 512)
    q, k, v = _mla_proj(p1, _rope_tables(s, ll), qa, mla_kv_a_norm[0].reshape(1, MLA_KV_RANK),
                        wq, wqs, wk, wv, ek, eks,
                        lane_gain(mla_q_norm[0]), swapped_gain(mla_q_norm[0]),
                        lane_gain(mla_k_norm[0]), swapped_gain(mla_k_norm[0]))
    attn = _attention(q, k, v, ll)

    wo1 = od_w_out[0]
    wo_attn = wo1[CONV_WIDTH:].reshape(MLA_HEADS, MLA_V, d)
    wo_attn = jnp.zeros((MLA_HEADS, HEAD_PAD, d), F32).at[:, :MLA_V].set(wo_attn).reshape(-1, d)
    h1, m, lg = _mixout(hc, attn, h, modtab[1], wo1[:CONV_WIDTH].astype(BF16), wo_attn.astype(BF16),
                        router_pieces(moe_router_w[1]), moe_router_b[1].reshape(1, N_EXPERTS),
                        lambda j: 1, 0, ll)
    return _moe(m, lg, h1, modtab[1], lambda j: 1, moe_w1[1].astype(BF16), moe_b1[1],
                moe_w2[1].astype(BF16), moe_b2[1], ll, lambda j: j)
```

```python
import functools

import jax
import jax.numpy as jnp
from jax import lax
from jax.experimental import pallas as pl
from jax.experimental.pallas import tpu as pltpu

F32 = jnp.float32
BF16 = jnp.bfloat16

D_MODEL = 1024
GRID_W = 64
EPS = 1e-6
HG_HEADS = 4
HG_DIM = 128
HG_W = HG_HEADS * HG_DIM
HG_SUB = 32
POOL_WINDOWS = (2, 4, 8, 16)
POOL_GROUP = 128
POOL_PAD = 16
CONV_WIDTH = 512
CONV_K = 31
CONV_PAD = 16
MLA_HEADS = 8
MLA_Q_RANK = 384
MLA_KV_RANK = 256
MLA_NOPE = 64
MLA_ROPE = 32
MLA_V = 64
MLA_QK = MLA_NOPE + MLA_ROPE
HEAD_PAD = 128
ROPE_AXIS_HALF = MLA_ROPE // 4
ROPE_BASE = 10000.0
N_EXPERTS = 32
TOP_K = 4
D_FF = 1024
SWIGLU_LIMIT = 7.0
SWIGLU_ALPHA = 1.702

ROW_TILE = 256
ROUTE_TILE = 512
MOE_TILE = 512
EXPERT_CHUNKS = 4
SPARE_BLOCKS = 4
Q_TILE = 512
LANES = 128
VMEM_LIMIT = 56 * 1024 * 1024

OD_CQ = 1024
OD_CKV = 1536
OD_KR = 1792
OD_IN_PAD = 1920


def _cp(sem, vmem=VMEM_LIMIT):
    return pltpu.CompilerParams(dimension_semantics=sem, vmem_limit_bytes=vmem)


def _dot(a, b):
    return jnp.dot(a, b, preferred_element_type=F32)


def _dot_nt(a, b):
    return lax.dot_general(a, b, (((1,), (1,)), ((), ())), preferred_element_type=F32)


def _dot_tn(a, b):
    return lax.dot_general(a, b, (((0,), (0,)), ((), ())), preferred_element_type=F32)


def _split3(x):
    hi = x.astype(BF16)
    r1 = x - hi.astype(F32)
    mid = r1.astype(BF16)
    lo = (r1 - mid.astype(F32)).astype(BF16)
    return hi, mid, lo


def _sigmoid(x):
    return 1.0 / (1.0 + jnp.exp(-x))


def _silu(x):
    return x * _sigmoid(x)


def _rms_scale(x, width):
    return lax.rsqrt(jnp.sum(x * x, axis=-1, keepdims=True) * (1.0 / width) + EPS)


def _ada_kernel(c_ref, w_ref, b_ref, o_ref):
    s = _silu(c_ref[...])
    sh, sm, sl = _split3(s)
    wh, wm, wl = _split3(w_ref[0])
    acc = _dot(sh, wh) + _dot(sh, wm) + _dot(sm, wh) + _dot(sh, wl) + _dot(sl, wh) + _dot(sm, wm)
    o_ref[0] = acc + b_ref[0]


def _ada_table(cond, ada_w, ada_b):
    depth, d, n = ada_w.shape
    rows = cond.shape[0]
    nb = n // d
    return pl.pallas_call(
        _ada_kernel,
        grid=(depth, nb),
        in_specs=[
            pl.BlockSpec((rows, d), lambda l, j: (0, 0)),
            pl.BlockSpec((1, d, d), lambda l, j: (l, 0, j)),
            pl.BlockSpec((1, 1, d), lambda l, j: (l, 0, j)),
        ],
        out_specs=pl.BlockSpec((1, rows, d), lambda l, j: (l, 0, j)),
        out_shape=jax.ShapeDtypeStruct((depth, rows, n), F32),
        compiler_params=_cp(("arbitrary", "arbitrary")),
    )(cond, ada_w, ada_b.reshape(depth, 1, n))


def _modproj_kernel(h_ref, mod_ref, w_ref, o_ref):
    x = h_ref[0]
    mod = mod_ref[0, 0]
    u = x * _rms_scale(x, x.shape[-1]) * (1.0 + mod[1:2]) + mod[0:1]
    o_ref[0] = _dot(u.astype(BF16), w_ref[...])


def _modproj(h, modtab, w, sel):
    b, s, d = h.shape
    n = w.shape[1]
    return pl.pallas_call(
        _modproj_kernel,
        grid=(b, s // ROW_TILE),
        in_specs=[
            pl.BlockSpec((1, ROW_TILE, d), lambda i, j: (i, j, 0)),
            pl.BlockSpec((1, 1, 6, d), lambda i, j: (i, sel(j), 0, 0)),
            pl.BlockSpec((d, n), lambda i, j: (0, 0)),
        ],
        out_specs=pl.BlockSpec((1, ROW_TILE, n), lambda i, j: (i, j, 0)),
        out_shape=jax.ShapeDtypeStruct((b, s, n), F32),
        compiler_params=_cp(("arbitrary", "arbitrary")),
    )(h, modtab, w)


def _hgrn_block(q, z, v, lb, st, reverse):
    rows = q.shape[0]
    nsub = rows // HG_SUB
    f = lb + (1.0 - lb) * _sigmoid(z)
    kin = 1.0 - f
    logf = jnp.log(f)
    ri = lax.broadcasted_iota(jnp.int32, (rows, rows), 0)
    ci = lax.broadcasted_iota(jnp.int32, (rows, rows), 1)
    same = (ri // HG_SUB) == (ci // HG_SUB)
    causal = same & ((ci >= ri) if reverse else (ci <= ri))
    tri = jnp.where(causal, 1.0, 0.0).astype(BF16)
    lh, lm, ll = _split3(logf)
    bcum = _dot(tri, lh) + _dot(tri, lm) + _dot(tri, ll)
    b3 = bcum.reshape(nsub, HG_SUB, HG_DIM)
    tot_row = 0 if reverse else HG_SUB - 1
    mid_row = HG_SUB // 2 if reverse else HG_SUB // 2 - 1
    tot = b3[:, tot_row:tot_row + 1, :]
    mid = b3[:, mid_row:mid_row + 1, :]
    q3 = q.reshape(nsub, HG_SUB, HG_DIM)
    k3 = kin.reshape(nsub, HG_SUB, HG_DIM)
    q_dec = (q3 * jnp.exp(b3)).astype(BF16)
    k_end = (k3 * jnp.exp(tot - b3)).astype(BF16)
    q_mid = (q3 * jnp.exp(b3 - mid)).reshape(rows, HG_DIM).astype(BF16)
    k_mid = (k3 * jnp.exp(mid - b3)).reshape(rows, HG_DIM).astype(BF16)
    vb = v.astype(BF16)
    att = jnp.where(causal, _dot_nt(q_mid, k_mid), 0.0).astype(BF16)
    o_intra = _dot(att, vb)
    v3 = vb.reshape(nsub, HG_SUB, HG_DIM)
    dec = jnp.exp(tot)
    upd = [_dot_tn(v3[j], k_end[j]) for j in range(nsub)]
    order = range(nsub - 1, -1, -1) if reverse else range(nsub)
    o_inter = [None] * nsub
    for j in order:
        o_inter[j] = _dot_nt(q_dec[j], st.astype(BF16))
        st = st * dec[j] + upd[j]
    o = o_intra + jnp.concatenate(o_inter, axis=0)
    return o, st


def _hgrn_kernel(q_ref, zf_ref, zb_ref, v_ref, og_ref, lb_ref, nw_ref, o_ref, acc_ref, *, nc, nl):
    blk = ROW_TILE
    nblk = nc + nl
    acc_ref[...] = jnp.zeros_like(acc_ref)
    lb_f = lb_ref[0:1, :]
    lb_b = lb_ref[1:2, :]

    def body(i, carry):
        st_f, st_b = carry
        rf = pl.multiple_of(i * blk, blk)
        ib = jnp.where(i < nc, nc - 1 - i, 2 * nc + nl - 1 - i)
        rb = pl.multiple_of(ib * blk, blk)
        qf = _silu(q_ref[0, pl.ds(rf, blk), :])
        o_f, st_f = _hgrn_block(qf, zf_ref[0, pl.ds(rf, blk), :], v_ref[0, pl.ds(rf, blk), :],
                                lb_f, st_f, False)
        acc_ref[pl.ds(rf, blk), :] += o_f
        qb = _silu(q_ref[0, pl.ds(rb, blk), :])
        o_b, st_b = _hgrn_block(qb, zb_ref[0, pl.ds(rb, blk), :], v_ref[0, pl.ds(rb, blk), :],
                                lb_b, st_b, True)
        acc_ref[pl.ds(rb, blk), :] += o_b
        return st_f, st_b

    zero = jnp.zeros((HG_DIM, HG_DIM), F32)
    lax.fori_loop(0, nblk, body, (zero, zero))
    o = acc_ref[...]
    og = og_ref[0]
    o_ref[0] = o * _rms_scale(o, HG_DIM) * nw_ref[...] * _silu(og)


def _hgrn(p, lb, norm_w, nc, nl):
    b, s, _ = p.shape
    sec = lambda k: pl.BlockSpec((1, s, HG_DIM), lambda i, h, k=k: (i, 0, HG_HEADS * k + h))
    return pl.pallas_call(
        functools.partial(_hgrn_kernel, nc=nc, nl=nl),
        grid=(b, HG_HEADS),
        in_specs=[sec(0), sec(1), sec(2), sec(3), sec(4),
                  pl.BlockSpec((2, HG_DIM), lambda i, h: (0, h)),
                  pl.BlockSpec((1, HG_DIM), lambda i, h: (0, 0))],
        out_specs=pl.BlockSpec((1, s, HG_DIM), lambda i, h: (i, 0, h)),
        out_shape=jax.ShapeDtypeStruct((b, s, HG_W), F32),
        scratch_shapes=[pltpu.VMEM((s, HG_DIM), F32)],
        compiler_params=_cp(("arbitrary", "arbitrary")),
    )(p, p, p, p, p, lb, norm_w.reshape(1, HG_DIM))


def _pool_kernel(x_ref, w_ref, sc_ref, o_ref, *, lc, ll):
    z = jnp.zeros((POOL_PAD, POOL_GROUP), F32)
    n = lc + ll + 3 * POOL_PAD
    pos_c = lax.broadcasted_iota(jnp.int32, (lc, 1), 0)
    pos_l = lax.broadcasted_iota(jnp.int32, (ll, 1), 0)
    for g, w in enumerate(POOL_WINDOWS):
        x = x_ref[0, :, g * POOL_GROUP:(g + 1) * POOL_GROUP]
        xp = jnp.concatenate([z, x[:lc], z, x[lc:], z], axis=0)
        acc = xp + pltpu.roll(xp, 1, 0)
        half = 1
        while 2 * half < w:
            acc = pltpu.roll(acc, half, 0) + pltpu.roll(acc, n - half, 0)
            half *= 2
        win = jnp.concatenate([acc[POOL_PAD:POOL_PAD + lc],
                               acc[2 * POOL_PAD + lc:2 * POOL_PAD + lc + ll]], axis=0)

        def count(pos, length):
            lo = jnp.maximum(pos - w // 2, 0)
            hi = jnp.minimum(pos + w - w // 2, length)
            return (hi - lo).astype(F32)

        cnt = jnp.concatenate([count(pos_c, lc), count(pos_l, ll)], axis=0)
        d = win / cnt - x
        y = _dot(d.astype(BF16), w_ref[g])
        o_ref[0, :, g * POOL_GROUP:(g + 1) * POOL_GROUP] = y * sc_ref[:, g * POOL_GROUP:(g + 1) * POOL_GROUP]


def _pool(p, pool_w, pool_scale, lc, ll):
    b, s, n = p.shape
    width = len(POOL_WINDOWS) * POOL_GROUP
    return pl.pallas_call(
        functools.partial(_pool_kernel, lc=lc, ll=ll),
        grid=(b,),
        in_specs=[pl.BlockSpec((1, s, width), lambda i: (i, 0, n // width - 1)),
                  pl.BlockSpec(pool_w.shape, lambda i: (0, 0, 0)),
                  pl.BlockSpec((1, width), lambda i: (0, 0))],
        out_specs=pl.BlockSpec((1, s, width), lambda i: (i, 0, 0)),
        out_shape=jax.ShapeDtypeStruct((b, s, width), F32),
        compiler_params=_cp(("arbitrary",)),
    )(p, pool_w.astype(BF16), pool_scale.reshape(1, width))


def _mixout_kernel(a_ref, b_ref, h_ref, mod_ref, wa_ref, wb_ref, wr_ref, br_ref,
                   h1_ref, m_ref, lg_ref):
    mod = mod_ref[0, 0]
    y = _dot(a_ref[0].astype(BF16), wa_ref[...]) + _dot(b_ref[0].astype(BF16), wb_ref[...])
    h1 = h_ref[0] + mod[2:3] * y
    h1_ref[0] = h1
    m = h1 * _rms_scale(h1, h1.shape[-1]) * (1.0 + mod[4:5]) + mod[3:4]
    m_ref[0] = m
    mh, mm, ml = _split3(m)
    wh, wm, wl = wr_ref[0], wr_ref[1], wr_ref[2]
    lg = (_dot_nt(mh, wh) + _dot_nt(mh, wm) + _dot_nt(mm, wh)
          + _dot_nt(mh, wl) + _dot_nt(ml, wh) + _dot_nt(mm, wm))
    lg_ref[0] = lg + br_ref[...]


def _mixout(a, bmix, h, modtab, wa, wb, wr3, br, sel, h_off, rows):
    b = h.shape[0]
    d = h.shape[-1]
    ka, kb = a.shape[-1], bmix.shape[-1]
    out3 = lambda n: jax.ShapeDtypeStruct((b, rows, n), F32)
    return pl.pallas_call(
        _mixout_kernel,
        grid=(b, rows // ROW_TILE),
        in_specs=[
            pl.BlockSpec((1, ROW_TILE, ka), lambda i, j: (i, j, 0)),
            pl.BlockSpec((1, ROW_TILE, kb), lambda i, j: (i, j, 0)),
            pl.BlockSpec((1, ROW_TILE, d), lambda i, j: (i, j + h_off, 0)),
            pl.BlockSpec((1, 1, 6, d), lambda i, j: (i, sel(j), 0, 0)),
            pl.BlockSpec((ka, d), lambda i, j: (0, 0)),
            pl.BlockSpec((kb, d), lambda i, j: (0, 0)),
            pl.BlockSpec((3, N_EXPERTS, d), lambda i, j: (0, 0, 0)),
            pl.BlockSpec((1, N_EXPERTS), lambda i, j: (0, 0)),
        ],
        out_specs=[
            pl.BlockSpec((1, ROW_TILE, d), lambda i, j: (i, j, 0)),
            pl.BlockSpec((1, ROW_TILE, d), lambda i, j: (i, j, 0)),
            pl.BlockSpec((1, ROW_TILE, N_EXPERTS), lambda i, j: (i, j, 0)),
        ],
        out_shape=[out3(d), out3(d), out3(N_EXPERTS)],
        compiler_params=_cp(("arbitrary", "arbitrary")),
    )(a, bmix, h, modtab, wa, wb, wr3, br)


def _route_kernel(lg_ref, idx_ref, gate_ref, cnt_ref):
    t = lg_ref.shape[0]

    @pl.when(pl.program_id(0) == 0)
    def _():
        cnt_ref[...] = jnp.zeros_like(cnt_ref)

    lg = lg_ref[...]
    lane = lax.broadcasted_iota(jnp.int32, (t, N_EXPERTS), 1).astype(F32)
    hot, vals, idxs = [], [], []
    for _ in range(TOP_K):
        mx = jnp.max(lg, axis=-1, keepdims=True)
        ix = jnp.min(jnp.where(lg == mx, lane, float(N_EXPERTS)), axis=-1, keepdims=True)
        oh = lane == ix
        hot.append(oh)
        vals.append(mx)
        idxs.append(ix.astype(jnp.int32))
        lg = jnp.where(oh, -jnp.inf, lg)
    ex = [jnp.exp(v - vals[0]) for v in vals]
    den = ex[0] + ex[1] + ex[2] + ex[3]
    sel = jnp.where(hot[0] | hot[1] | hot[2] | hot[3], 1.0, 0.0)
    out_lane = lax.broadcasted_iota(jnp.int32, (t, LANES), 1)
    idx_o = jnp.zeros((t, LANES), jnp.int32)
    gate_o = jnp.zeros((t, LANES), F32)
    for k in range(TOP_K):
        idx_o = jnp.where(out_lane == k, idxs[k], idx_o)
        gate_o = jnp.where(out_lane == k, ex[k] / den, gate_o)
    idx_ref[...] = idx_o
    gate_ref[...] = gate_o
    cnt_ref[...] += jnp.sum(sel, axis=0, keepdims=True)


def _route(logits):
    t = logits.shape[0]
    tile = pl.BlockSpec((ROUTE_TILE, LANES), lambda i: (i, 0))
    return pl.pallas_call(
        _route_kernel,
        grid=(t // ROUTE_TILE,),
        in_specs=[pl.BlockSpec((ROUTE_TILE, N_EXPERTS), lambda i: (i, 0))],
        out_specs=[tile, tile, pl.BlockSpec((1, N_EXPERTS), lambda i: (0, 0))],
        out_shape=[jax.ShapeDtypeStruct((t, LANES), jnp.int32),
                   jax.ShapeDtypeStruct((t, LANES), F32),
                   jax.ShapeDtypeStruct((1, N_EXPERTS), F32)],
        compiler_params=_cp(("arbitrary",)),
    )(logits)


def _experts_kernel(be_ref, src0_ref, srcn_ref, dstp_ref, dstc_ref, m_ref, w1_ref, b1_ref, w2_ref,
                    b2_ref, yu_ref, xa, xb, ya, yb, w1b, w2b, gsem, ssem, *, spare):
    g = pl.program_id(0)
    last = pl.num_programs(0) - 1
    tile = xa.shape[0]
    xbuf = (xa, xb)
    ybuf = (ya, yb)

    def gather(idx_ref, r, s):
        return pltpu.make_async_copy(m_ref.at[pl.ds(idx_ref[r], 1)], xbuf[s].at[pl.ds(r, 1)], gsem.at[s])

    def scatter(row, r, s):
        return pltpu.make_async_copy(ybuf[s].at[pl.ds(r, 1)], yu_ref.at[pl.ds(row, 1)], ssem.at[s])

    def wait_gather(s):
        pltpu.make_async_copy(m_ref.at[pl.ds(0, tile)], xbuf[s], gsem.at[s]).wait()

    def wait_scatter(s):
        pltpu.make_async_copy(ybuf[s], yu_ref.at[pl.ds(0, tile)], ssem.at[s]).wait()

    @pl.when(g == 0)
    def _():
        ya[...] = jnp.zeros_like(ya)
        yb[...] = jnp.zeros_like(yb)

        def first(r, c):
            gather(src0_ref, r, 0).start()
            scatter(spare + 3 * tile + r, r, 0).start()
            return c

        lax.fori_loop(0, tile, first, 0)

    changed = (g == 0) | (be_ref[g] != be_ref[jnp.maximum(g - 1, 0)])

    @pl.when(changed)
    def _():
        w1b[...] = w1_ref[0, 0].astype(BF16)
        w2b[...] = w2_ref[0, 0].astype(BF16)

    def block(slot, other):
        wait_gather(slot)
        wait_scatter(slot)
        rows = tile // EXPERT_CHUNKS
        for c in range(EXPERT_CHUNKS):
            for r in range(c * rows, (c + 1) * rows):
                gather(srcn_ref, r, other).start()
                scatter(jnp.where(g > 0, dstp_ref[r], spare + 2 * tile + r), r, other).start()
            x = xbuf[slot][pl.ds(c * rows, rows), :].astype(BF16)
            hb = _dot(x, w1b[...]) + b1_ref[0, 0]
            gl = jnp.minimum(hb[:, :D_FF], SWIGLU_LIMIT)
            lin = jnp.clip(hb[:, D_FF:], -SWIGLU_LIMIT, SWIGLU_LIMIT)
            a = gl * _sigmoid(SWIGLU_ALPHA * gl) * (lin + 1.0)
            ybuf[slot][pl.ds(c * rows, rows), :] = _dot(a.astype(BF16), w2b[...]) + b2_ref[0, 0]

        @pl.when(g == last)
        def _():
            def final(r, c):
                scatter(dstc_ref[r], r, slot).start()
                return c

            lax.fori_loop(0, tile, final, 0)
            wait_scatter(other)
            wait_scatter(slot)
            wait_gather(other)

    @pl.when(g % 2 == 1)
    def _():
        block(1, 0)

    @pl.when(g % 2 == 0)
    def _():
        block(0, 1)


def _experts(m, blk_e, src_tok, dst_row, layer, w1, b1, w2, b2):
    t, d = m.shape
    _, e, _, f2 = w1.shape
    nb = src_tok.shape[0] // MOE_TILE
    spare = TOP_K * t
    idx = lambda fn: pl.BlockSpec((MOE_TILE,), fn, memory_space=pltpu.SMEM)
    grid_spec = pltpu.PrefetchScalarGridSpec(
        num_scalar_prefetch=1,
        grid=(nb,),
        in_specs=[
            idx(lambda i, be: (0,)),
            idx(lambda i, be: (jnp.minimum(i + 1, nb - 1),)),
            idx(lambda i, be: (jnp.maximum(i - 1, 0),)),
            idx(lambda i, be: (i,)),
            pl.BlockSpec(memory_space=pl.ANY),
            pl.BlockSpec((1, 1, d, f2), lambda i, be: (layer, be[i], 0, 0)),
            pl.BlockSpec((1, 1, 1, f2), lambda i, be: (layer, be[i], 0, 0)),
            pl.BlockSpec((1, 1, f2 // 2, d), lambda i, be: (layer, be[i], 0, 0)),
            pl.BlockSpec((1, 1, 1, d), lambda i, be: (layer, be[i], 0, 0)),
        ],
        out_specs=pl.BlockSpec(memory_space=pl.ANY),
        scratch_shapes=[pltpu.VMEM((MOE_TILE, d), F32)] * 4 + [
                        pltpu.VMEM((d, f2), BF16), pltpu.VMEM((f2 // 2, d), BF16),
                        pltpu.SemaphoreType.DMA((2,)), pltpu.SemaphoreType.DMA((2,))],
    )
    return pl.pallas_call(
        functools.partial(_experts_kernel, spare=spare),
        grid_spec=grid_spec,
        out_shape=jax.ShapeDtypeStruct((spare + SPARE_BLOCKS * MOE_TILE, d), F32),
        compiler_params=_cp(("arbitrary",)),
    )(blk_e, src_tok, src_tok, dst_row, dst_row, m, w1, b1.reshape(-1, e, 1, f2), w2,
      b2.reshape(-1, e, 1, d))


def _combine_kernel(y0_ref, y1_ref, y2_ref, y3_ref, gate_ref, h_ref, mod_ref, o_ref):
    gate = gate_ref[...]
    f = gate[:, 0:1] * y0_ref[...]
    for k, y_ref in ((1, y1_ref), (2, y2_ref), (3, y3_ref)):
        f = f + gate[:, k:k + 1] * y_ref[...]
    o_ref[0] = h_ref[0] + mod_ref[0, 0][5:6] * f


def _combine(yu, gate, h1, modtab, sel, out_rows, out_map):
    b, r, d = h1.shape
    nblk = r // ROW_TILE
    tblk = b * nblk
    ysp = lambda k: pl.BlockSpec((ROW_TILE, d), lambda i, j, k=k: (k * tblk + i * nblk + j, 0))
    return pl.pallas_call(
        _combine_kernel,
        grid=(b, nblk),
        in_specs=[
            ysp(0), ysp(1), ysp(2), ysp(3),
            pl.BlockSpec((ROW_TILE, LANES), lambda i, j: (i * nblk + j, 0)),
            pl.BlockSpec((1, ROW_TILE, d), lambda i, j: (i, j, 0)),
            pl.BlockSpec((1, 1, 6, d), lambda i, j: (i, sel(j), 0, 0)),
        ],
        out_specs=pl.BlockSpec((1, ROW_TILE, d), lambda i, j: (i, out_map(j), 0)),
        out_shape=jax.ShapeDtypeStruct((b, out_rows, d), F32),
        compiler_params=_cp(("arbitrary", "arbitrary")),
    )(yu, yu, yu, yu, gate, h1, modtab)


def _moe(m, logits, h1, modtab, sel, layer, w1, b1, w2, b2, out_rows, out_map):
    b, r, d = m.shape
    t = b * r
    n_assign = t * TOP_K
    idx, gate, counts = _route(logits.reshape(t, N_EXPERTS))
    counts = counts[0].astype(jnp.int32)
    n_blocks = n_assign // MOE_TILE + N_EXPERTS
    n_slots = n_blocks * MOE_TILE
    padded = (counts + MOE_TILE - 1) // MOE_TILE * MOE_TILE
    pad_end = jnp.cumsum(padded)
    fill_end = jnp.cumsum(padded - counts)
    fill = jnp.arange(n_slots - n_assign, dtype=jnp.int32)
    fill_e = jnp.sum(fill[:, None] >= fill_end[None, :], axis=1).astype(jnp.int32)
    keys = jnp.concatenate([idx[:, :TOP_K].reshape(-1), fill_e])
    order = jnp.argsort(keys, stable=True).astype(jnp.int32)
    real = order < n_assign
    src_tok = jnp.where(real, order // TOP_K, 0)
    spare_row = n_assign + jnp.arange(n_slots, dtype=jnp.int32) % (2 * MOE_TILE)
    dst_row = jnp.where(real, (order % TOP_K) * t + order // TOP_K, spare_row)
    blk_start = jnp.arange(n_blocks, dtype=jnp.int32) * MOE_TILE
    blk_e = jnp.minimum(jnp.sum(blk_start[:, None] >= pad_end[None, :], axis=1), N_EXPERTS - 1).astype(jnp.int32)
    yu = _experts(m.reshape(t, d), blk_e, src_tok, dst_row, layer, w1, b1, w2, b2)
    return _combine(yu, gate, h1, modtab, sel, out_rows, out_map)


def _conv_kernel(a_ref, dw_ref, db_ref, lw_ref, lb_ref, o_ref, hs_ref, *, ll):
    hs_ref[0:CONV_PAD, :] = jnp.zeros((CONV_PAD, CONV_WIDTH), F32)
    hs_ref[CONV_PAD + ll:2 * CONV_PAD + ll, :] = jnp.zeros((CONV_PAD, CONV_WIDTH), F32)
    hs_ref[CONV_PAD:CONV_PAD + ll, :] = a_ref[0, :, :CONV_WIDTH] * _sigmoid(a_ref[0, :, CONV_WIDTH:])
    first = CONV_PAD - CONV_K // 2

    def tile(i, c):
        r0 = pl.multiple_of(i * ROW_TILE, ROW_TILE)
        win = hs_ref[pl.ds(r0, ROW_TILE + 2 * CONV_PAD), :]
        acc = jnp.zeros((ROW_TILE, CONV_WIDTH), F32)
        for k in range(CONV_K):
            acc = acc + dw_ref[k:k + 1, :] * win[first + k:first + k + ROW_TILE]
        acc = acc + db_ref[...]
        mu = jnp.mean(acc, axis=-1, keepdims=True)
        dlt = acc - mu
        var = jnp.mean(dlt * dlt, axis=-1, keepdims=True)
        y = dlt * lax.rsqrt(var + EPS) * lw_ref[...] + lb_ref[...]
        o_ref[0, pl.ds(r0, ROW_TILE), :] = _silu(y)
        return c

    lax.fori_loop(0, ll // ROW_TILE, tile, 0)


def _conv(p, dw_w, dw_b, ln_w, ln_b, ll):
    b = p.shape[0]
    row = lambda v: v.reshape(1, CONV_WIDTH)
    vec = pl.BlockSpec((1, CONV_WIDTH), lambda i: (0, 0))
    return pl.pallas_call(
        functools.partial(_conv_kernel, ll=ll),
        grid=(b,),
        in_specs=[pl.BlockSpec((1, ll, 2 * CONV_WIDTH), lambda i: (i, 0, 0)),
                  pl.BlockSpec((CONV_K, CONV_WIDTH), lambda i: (0, 0)), vec, vec, vec],
        out_specs=pl.BlockSpec((1, ll, CONV_WIDTH), lambda i: (i, 0, 0)),
        out_shape=jax.ShapeDtypeStruct((b, ll, CONV_WIDTH), F32),
        scratch_shapes=[pltpu.VMEM((ll + 2 * CONV_PAD, CONV_WIDTH), F32)],
        compiler_params=_cp(("arbitrary",)),
    )(p, dw_w, row(dw_b), row(ln_w), row(ln_b))


def _mla_proj_kernel(cq_ref, ckv_ref, kr_ref, cs_ref, qa_ref, kva_ref, wq_ref, wqs_ref, wk_ref,
                     wv_ref, ek_ref, eks_ref, gq_ref, gqs_ref, gk_ref, gks_ref,
                     q_ref, k_ref, v_ref):
    cos = cs_ref[0]
    sin = cs_ref[1]
    cq = cq_ref[0]
    cqn = (cq * _rms_scale(cq, MLA_Q_RANK) * qa_ref[...]).astype(BF16)
    qx = _dot(cqn, wq_ref[...])
    qs = _dot(cqn, wqs_ref[...])
    ckv = ckv_ref[0]
    ckvn = (ckv * _rms_scale(ckv, MLA_KV_RANK) * kva_ref[...]).astype(BF16)
    kr = kr_ref[0]
    rh = kr.astype(BF16)
    rl = (kr - rh.astype(F32)).astype(BF16)
    kx = _dot(ckvn, wk_ref[...]) + _dot(rh, ek_ref[...]) + _dot(rl, ek_ref[...])
    ks = _dot(rh, eks_ref[...]) + _dot(rl, eks_ref[...])
    v_ref[0] = _dot(ckvn, wv_ref[...]).astype(BF16)
    scale = MLA_QK ** -0.5
    for h in range(MLA_HEADS):
        sl = slice(h * HEAD_PAD, (h + 1) * HEAD_PAD)
        qh = qx[:, sl]
        q_ref[0, :, sl] = (_rms_scale(qh, MLA_QK) * scale
                           * (qh * gq_ref[...] * cos + qs[:, sl] * gqs_ref[...] * sin)).astype(BF16)
        kh = kx[:, sl]
        k_ref[0, :, sl] = (_rms_scale(kh, MLA_QK)
                           * (kh * gk_ref[...] * cos + ks[:, sl] * gks_ref[...] * sin)).astype(BF16)


def _mla_proj(p, cs, qa, kva, wq, wqs, wk, wv, ek, eks, gq, gqs, gk, gks):
    b, s, _ = p.shape
    hw = MLA_HEADS * HEAD_PAD
    full = lambda a: pl.BlockSpec(a.shape, lambda i, j: (0,) * a.ndim)
    out = pl.BlockSpec((1, ROW_TILE, hw), lambda i, j: (i, j, 0))
    consts = (qa, kva, wq, wqs, wk, wv, ek, eks, gq, gqs, gk, gks)
    return pl.pallas_call(
        _mla_proj_kernel,
        grid=(b, s // ROW_TILE),
        in_specs=[pl.BlockSpec((1, ROW_TILE, 512), lambda i, j: (i, j, OD_CQ // 512)),
                  pl.BlockSpec((1, ROW_TILE, MLA_KV_RANK), lambda i, j: (i, j, OD_CKV // MLA_KV_RANK)),
                  pl.BlockSpec((1, ROW_TILE, LANES), lambda i, j: (i, j, OD_KR // LANES)),
                  pl.BlockSpec((2, ROW_TILE, HEAD_PAD), lambda i, j: (0, j, 0))]
                 + [full(a) for a in consts],
        out_specs=[out, out, out],
        out_shape=[jax.ShapeDtypeStruct((b, s, hw), BF16)] * 3,
        compiler_params=_cp(("arbitrary", "arbitrary")),
    )(p, p, p, cs, *consts)


def _attn_kernel(q_ref, k_ref, v_ref, o_ref):
    def scores(h):
        sl = slice(h * HEAD_PAD, (h + 1) * HEAD_PAD)
        return _dot_nt(q_ref[0, :, sl], k_ref[0, :, sl])

    s = scores(0)
    for h in range(MLA_HEADS):
        s_next = scores(h + 1) if h + 1 < MLA_HEADS else None
        sl = slice(h * HEAD_PAD, (h + 1) * HEAD_PAD)
        p = jnp.exp(s - jnp.max(s, axis=-1, keepdims=True))
        den = jnp.sum(p, axis=-1, keepdims=True)
        o_ref[0, :, sl] = (_dot(p.astype(BF16), v_ref[0, :, sl]) / den).astype(BF16)
        s = s_next


def _attention(q, k, v, ll):
    b, s, hw = k.shape
    kv = pl.BlockSpec((1, s, hw), lambda i, j: (i, 0, 0))
    qo = pl.BlockSpec((1, Q_TILE, hw), lambda i, j: (i, j, 0))
    return pl.pallas_call(
        _attn_kernel,
        grid=(b, ll // Q_TILE),
        in_specs=[qo, kv, kv],
        out_specs=qo,
        out_shape=jax.ShapeDtypeStruct((b, ll, hw), BF16),
        compiler_params=_cp(("arbitrary", "arbitrary")),
    )(q, k, v)


def _place(parts, lead):
    out = jnp.zeros(lead + (MLA_HEADS, HEAD_PAD), F32)
    for arr, off in parts:
        out = out.at[..., off:off + arr.shape[-1]].set(arr)
    return out.reshape(lead + (MLA_HEADS * HEAD_PAD,))


def _swap_pairs(a):
    g = a.reshape(a.shape[:-1] + (2, 2, ROPE_AXIS_HALF))
    return jnp.flip(g, axis=-2).reshape(a.shape)


def _rope_tables(rows_total, ll):
    grid_rows = ll // GRID_W
    t_row = jnp.repeat(jnp.arange(grid_rows, dtype=F32), GRID_W)
    t_col = jnp.tile(jnp.arange(GRID_W, dtype=F32), grid_rows)
    inv = 1.0 / (ROPE_BASE ** (jnp.arange(ROPE_AXIS_HALF, dtype=F32) / ROPE_AXIS_HALF))
    ang_r = t_row[:, None] * inv
    ang_c = t_col[:, None] * inv
    cos32 = jnp.concatenate([jnp.cos(ang_r), jnp.cos(ang_r), jnp.cos(ang_c), jnp.cos(ang_c)], axis=-1)
    sin32 = jnp.concatenate([-jnp.sin(ang_r), jnp.sin(ang_r), -jnp.sin(ang_c), jnp.sin(ang_c)], axis=-1)
    extra = rows_total - ll
    cos32 = jnp.concatenate([cos32, jnp.ones((extra, MLA_ROPE), F32)], axis=0)
    sin32 = jnp.concatenate([sin32, jnp.zeros((extra, MLA_ROPE), F32)], axis=0)
    cos_t = jnp.concatenate([jnp.ones((rows_total, MLA_NOPE), F32), cos32,
                             jnp.zeros((rows_total, HEAD_PAD - MLA_QK), F32)], axis=-1)
    sin_t = jnp.concatenate([jnp.zeros((rows_total, MLA_NOPE), F32), sin32,
                             jnp.zeros((rows_total, HEAD_PAD - MLA_QK), F32)], axis=-1)
    return jnp.stack([cos_t, sin_t])


def kernel(x, c, ctx, c_ctx, ada_w, ada_b, ev_w_in, hgrn_lb_logits, hgrn_norm_w, pool_w, pool_scale,
           ev_w_out, od_w_in, conv_dw_w, conv_dw_b, conv_ln_w, conv_ln_b, mla_q_a_norm, mla_w_uq,
           mla_kv_a_norm, mla_w_ukv, mla_q_norm, mla_k_norm, od_w_out, moe_router_w, moe_router_b,
           moe_w1, moe_b1, moe_w2, moe_b2):
    b, ll, d = x.shape
    lc = ctx.shape[1]
    s = ll + lc
    nc, nl = lc // ROW_TILE, ll // ROW_TILE
    assert ada_w.shape[0] == 2 and d == D_MODEL
    assert ll % Q_TILE == 0 and lc % ROW_TILE == 0 and (b * s) % ROUTE_TILE == 0 and (b * ll) % ROUTE_TILE == 0

    cond_rows = (b + 1 + 7) // 8 * 8
    cond = jnp.zeros((cond_rows, d), F32).at[:b].set(c).at[b].set(c_ctx)
    mod = _ada_table(cond, ada_w, ada_b)
    modtab = []
    for layer in range(2):
        lat = mod[layer, :b].reshape(b, 1, 6, d)
        con = jnp.broadcast_to(mod[layer, b].reshape(1, 1, 6, d), (b, 1, 6, d))
        modtab.append(jnp.concatenate([con, lat], axis=1))
    lb_all = jnp.cumsum(jax.nn.softmax(hgrn_lb_logits.astype(F32), axis=0), axis=0)

    def router_pieces(w):
        hi, mid, lo = _split3(w.T)
        return jnp.stack([hi, mid, lo])

    sel0 = lambda j: jnp.where(j < nc, 0, 1)
    h = jnp.concatenate([ctx, x], axis=1)
    p0 = _modproj(h, modtab[0], ev_w_in[0].astype(BF16), sel0)
    hg = _hgrn(p0, lb_all[0], hgrn_norm_w[0], nc, nl)
    yp = _pool(p0, pool_w[0], pool_scale[0], lc, ll)
    wo = ev_w_out[0].astype(BF16)
    h1, m, lg = _mixout(hg, yp, h, modtab[0], wo[:HG_W], wo[HG_W:], router_pieces(moe_router_w[0]),
                        moe_router_b[0].reshape(1, N_EXPERTS), sel0, 0, s)
    to_l1 = lambda j: jnp.where(j < nc, j + nl, j - nc)
    h = _moe(m, lg, h1, modtab[0], sel0, 0, moe_w1, moe_b1, moe_w2, moe_b2, s, to_l1)

    sel1 = lambda j: jnp.where(j < nl, 1, 0)
    w_in = od_w_in[0]
    kr_cols = w_in[:, 1664:1696]
    w1p = jnp.zeros((d, OD_IN_PAD), F32)
    w1p = w1p.at[:, :1024].set(w_in[:, :1024]).at[:, OD_CQ:OD_CQ + MLA_Q_RANK].set(w_in[:, 1024:1408])
    w1p = w1p.at[:, OD_CKV:OD_CKV + MLA_KV_RANK].set(w_in[:, 1408:1664])
    w1p = w1p.at[:, OD_KR:OD_KR + MLA_ROPE].set(kr_cols)
    w1p = w1p.at[:, OD_KR + MLA_ROPE:OD_KR + 2 * MLA_ROPE].set(_swap_pairs(kr_cols))
    p1 = _modproj(h, modtab[1], w1p.astype(BF16), sel1)
    hc = _conv(p1, conv_dw_w[0], conv_dw_b[0], conv_ln_w[0], conv_ln_b[0], ll)

    wuq = mla_w_uq[0].reshape(MLA_Q_RANK, MLA_HEADS, MLA_QK)
    wuq_rope = wuq[..., MLA_NOPE:]
    pad_rows = lambda w: jnp.zeros((512, w.shape[1]), F32).at[:MLA_Q_RANK].set(w)
    wq = pad_rows(_place([(wuq, 0)], (MLA_Q_RANK,))).astype(BF16)
    wqs = pad_rows(_place([(_swap_pairs(wuq_rope), MLA_NOPE)], (MLA_Q_RANK,))).astype(BF16)
    wukv = mla_w_ukv[0].reshape(MLA_KV_RANK, MLA_HEADS, MLA_NOPE + MLA_V)
    wk = _place([(wukv[..., :MLA_NOPE], 0)], (MLA_KV_RANK,)).astype(BF16)
    wv = _place([(wukv[..., MLA_NOPE:], 0)], (MLA_KV_RANK,)).astype(BF16)
    eye = jnp.broadcast_to(jnp.eye(MLA_ROPE, dtype=F32)[:, None, :], (MLA_ROPE, MLA_HEADS, MLA_ROPE))
    zero_rows = jnp.zeros((LANES - 2 * MLA_ROPE, MLA_HEADS * HEAD_PAD), F32)
    place_kr = _place([(eye, MLA_NOPE)], (MLA_ROPE,))
    zero_kr = jnp.zeros_like(place_kr)
    ek = jnp.concatenate([place_kr, zero_kr, zero_rows], axis=0).astype(BF16)
    eks = jnp.concatenate([zero_kr, place_kr, zero_rows], axis=0).astype(BF16)
    lane_gain = lambda g: jnp.concatenate([g, jnp.zeros((HEAD_PAD - MLA_QK,), F32)]).reshape(1, HEAD_PAD)
    swapped_gain = lambda g: jnp.concatenate(
        [jnp.zeros((MLA_NOPE,), F32), _swap_pairs(g[MLA_NOPE:]), jnp.zeros((HEAD_PAD - MLA_QK,), F32)]
    ).reshape(1, HEAD_PAD)
    qa = jnp.concatenate([mla_q_a_norm[0], jnp.zeros((512 - MLA_Q_RANK,), F32)]).reshape(1, 512)
    q, k, v = _mla_proj(p1, _rope_tables(s, ll), qa, mla_kv_a_norm[0].reshape(1, MLA_KV_RANK),
                        wq, wqs, wk, wv, ek, eks,
                        lane_gain(mla_q_norm[0]), swapped_gain(mla_q_norm[0]),
                        lane_gain(mla_k_norm[0]), swapped_gain(mla_k_norm[0]))
    attn = _attention(q, k, v, ll)

    wo1 = od_w_out[0]
    wo_attn = wo1[CONV_WIDTH:].reshape(MLA_HEADS, MLA_V, d)
    wo_attn = jnp.zeros((MLA_HEADS, HEAD_PAD, d), F32).at[:, :MLA_V].set(wo_attn).reshape(-1, d)
    h1, m, lg = _mixout(hc, attn, h, modtab[1], wo1[:CONV_WIDTH].astype(BF16), wo_attn.astype(BF16),
                        router_pieces(moe_router_w[1]), moe_router_b[1].reshape(1, N_EXPERTS),
                        lambda j: 1, 0, ll)
    return _moe(m, lg, h1, modtab[1], lambda j: 1, 1, moe_w1, moe_b1, moe_w2, moe_b2, ll, lambda j: j)
```

```python
import functools

import jax
import jax.numpy as jnp
from jax import lax
from jax.experimental import pallas as pl
from jax.experimental.pallas import tpu as pltpu

F32 = jnp.float32
BF16 = jnp.bfloat16

D_MODEL = 1024
GRID_W = 64
EPS = 1e-6
HG_HEADS = 4
HG_DIM = 128
HG_W = HG_HEADS * HG_DIM
HG_SUB = 32
POOL_WINDOWS = (2, 4, 8, 16)
POOL_GROUP = 128
POOL_PAD = 16
CONV_WIDTH = 512
CONV_K = 31
CONV_PAD = 16
MLA_HEADS = 8
MLA_Q_RANK = 384
MLA_KV_RANK = 256
MLA_NOPE = 64
MLA_ROPE = 32
MLA_V = 64
MLA_QK = MLA_NOPE + MLA_ROPE
HEAD_PAD = 128
ROPE_AXIS_HALF = MLA_ROPE // 4
ROPE_BASE = 10000.0
N_EXPERTS = 32
TOP_K = 4
D_FF = 1024
SWIGLU_LIMIT = 7.0
SWIGLU_ALPHA = 1.702

ROW_TILE = 256
ROUTE_TILE = 512
MOE_TILE = 512
EXPERT_CHUNKS = 4
SPARE_BLOCKS = 4
Q_TILE = 512
LANES = 128
ROW_SPLIT = D_MODEL // LANES
VMEM_LIMIT = 56 * 1024 * 1024

OD_CQ = 1024
OD_CKV = 1536
OD_KR = 1792
OD_IN_PAD = 1920


def _cp(sem, vmem=VMEM_LIMIT):
    return pltpu.CompilerParams(dimension_semantics=sem, vmem_limit_bytes=vmem)


def _dot(a, b):
    return jnp.dot(a, b, preferred_element_type=F32)


def _dot_nt(a, b):
    return lax.dot_general(a, b, (((1,), (1,)), ((), ())), preferred_element_type=F32)


def _dot_tn(a, b):
    return lax.dot_general(a, b, (((0,), (0,)), ((), ())), preferred_element_type=F32)


def _split3(x):
    hi = x.astype(BF16)
    r1 = x - hi.astype(F32)
    mid = r1.astype(BF16)
    lo = (r1 - mid.astype(F32)).astype(BF16)
    return hi, mid, lo


def _sigmoid(x):
    return 1.0 / (1.0 + jnp.exp(-x))


def _silu(x):
    return x * _sigmoid(x)


def _load_row_tiles(ref, row0, rows):
    return jnp.concatenate(
        [ref[pl.ds(row0 * ROW_SPLIT + s, rows, stride=ROW_SPLIT), :] for s in range(ROW_SPLIT)], axis=1)


def _store_row_tiles(ref, row0, x):
    for s in range(ROW_SPLIT):
        ref[pl.ds(row0 * ROW_SPLIT + s, x.shape[0], stride=ROW_SPLIT), :] = x[:, s * LANES:(s + 1) * LANES]


def _rms_scale(x, width):
    return lax.rsqrt(jnp.sum(x * x, axis=-1, keepdims=True) * (1.0 / width) + EPS)


def _ada_kernel(c_ref, w_ref, b_ref, o_ref):
    s = _silu(c_ref[...])
    sh, sm, sl = _split3(s)
    wh, wm, wl = _split3(w_ref[0])
    acc = _dot(sh, wh) + _dot(sh, wm) + _dot(sm, wh) + _dot(sh, wl) + _dot(sl, wh) + _dot(sm, wm)
    o_ref[0] = acc + b_ref[0]


def _ada_table(cond, ada_w, ada_b):
    depth, d, n = ada_w.shape
    rows = cond.shape[0]
    nb = n // d
    return pl.pallas_call(
        _ada_kernel,
        grid=(depth, nb),
        in_specs=[
            pl.BlockSpec((rows, d), lambda l, j: (0, 0)),
            pl.BlockSpec((1, d, d), lambda l, j: (l, 0, j)),
            pl.BlockSpec((1, 1, d), lambda l, j: (l, 0, j)),
        ],
        out_specs=pl.BlockSpec((1, rows, d), lambda l, j: (l, 0, j)),
        out_shape=jax.ShapeDtypeStruct((depth, rows, n), F32),
        compiler_params=_cp(("arbitrary", "arbitrary")),
    )(cond, ada_w, ada_b.reshape(depth, 1, n))


def _modproj_kernel(h_ref, mod_ref, w_ref, o_ref):
    x = h_ref[0]
    mod = mod_ref[0, 0]
    u = x * _rms_scale(x, x.shape[-1]) * (1.0 + mod[1:2]) + mod[0:1]
    o_ref[0] = _dot(u.astype(BF16), w_ref[...])


def _modproj(h, modtab, w, sel):
    b, s, d = h.shape
    n = w.shape[1]
    return pl.pallas_call(
        _modproj_kernel,
        grid=(b, s // ROW_TILE),
        in_specs=[
            pl.BlockSpec((1, ROW_TILE, d), lambda i, j: (i, j, 0)),
            pl.BlockSpec((1, 1, 6, d), lambda i, j: (i, sel(j), 0, 0)),
            pl.BlockSpec((d, n), lambda i, j: (0, 0)),
        ],
        out_specs=pl.BlockSpec((1, ROW_TILE, n), lambda i, j: (i, j, 0)),
        out_shape=jax.ShapeDtypeStruct((b, s, n), F32),
        compiler_params=_cp(("arbitrary", "arbitrary")),
    )(h, modtab, w)


def _hgrn_block(q, z, v, lb, st, reverse):
    rows = q.shape[0]
    nsub = rows // HG_SUB
    f = lb + (1.0 - lb) * _sigmoid(z)
    kin = 1.0 - f
    logf = jnp.log(f)
    ri = lax.broadcasted_iota(jnp.int32, (rows, rows), 0)
    ci = lax.broadcasted_iota(jnp.int32, (rows, rows), 1)
    same = (ri // HG_SUB) == (ci // HG_SUB)
    causal = same & ((ci >= ri) if reverse else (ci <= ri))
    tri = jnp.where(causal, 1.0, 0.0).astype(BF16)
    lh, lm, ll = _split3(logf)
    bcum = _dot(tri, lh) + _dot(tri, lm) + _dot(tri, ll)
    b3 = bcum.reshape(nsub, HG_SUB, HG_DIM)
    tot_row = 0 if reverse else HG_SUB - 1
    mid_row = HG_SUB // 2 if reverse else HG_SUB // 2 - 1
    tot = b3[:, tot_row:tot_row + 1, :]
    mid = b3[:, mid_row:mid_row + 1, :]
    q3 = q.reshape(nsub, HG_SUB, HG_DIM)
    k3 = kin.reshape(nsub, HG_SUB, HG_DIM)
    q_dec = (q3 * jnp.exp(b3)).astype(BF16)
    k_end = (k3 * jnp.exp(tot - b3)).astype(BF16)
    q_mid = (q3 * jnp.exp(b3 - mid)).reshape(rows, HG_DIM).astype(BF16)
    k_mid = (k3 * jnp.exp(mid - b3)).reshape(rows, HG_DIM).astype(BF16)
    vb = v.astype(BF16)
    att = jnp.where(causal, _dot_nt(q_mid, k_mid), 0.0).astype(BF16)
    o_intra = _dot(att, vb)
    v3 = vb.reshape(nsub, HG_SUB, HG_DIM)
    dec = jnp.exp(tot)
    upd = [_dot_tn(v3[j], k_end[j]) for j in range(nsub)]
    order = range(nsub - 1, -1, -1) if reverse else range(nsub)
    o_inter = [None] * nsub
    for j in order:
        o_inter[j] = _dot_nt(q_dec[j], st.astype(BF16))
        st = st * dec[j] + upd[j]
    o = o_intra + jnp.concatenate(o_inter, axis=0)
    return o, st


def _hgrn_kernel(q_ref, zf_ref, zb_ref, v_ref, og_ref, lb_ref, nw_ref, o_ref, acc_ref, *, nc, nl):
    blk = ROW_TILE
    nblk = nc + nl
    acc_ref[...] = jnp.zeros_like(acc_ref)
    lb_f = lb_ref[0:1, :]
    lb_b = lb_ref[1:2, :]

    def body(i, carry):
        st_f, st_b = carry
        rf = pl.multiple_of(i * blk, blk)
        ib = jnp.where(i < nc, nc - 1 - i, 2 * nc + nl - 1 - i)
        rb = pl.multiple_of(ib * blk, blk)
        qf = _silu(q_ref[0, pl.ds(rf, blk), :])
        o_f, st_f = _hgrn_block(qf, zf_ref[0, pl.ds(rf, blk), :], v_ref[0, pl.ds(rf, blk), :],
                                lb_f, st_f, False)
        acc_ref[pl.ds(rf, blk), :] += o_f
        qb = _silu(q_ref[0, pl.ds(rb, blk), :])
        o_b, st_b = _hgrn_block(qb, zb_ref[0, pl.ds(rb, blk), :], v_ref[0, pl.ds(rb, blk), :],
                                lb_b, st_b, True)
        acc_ref[pl.ds(rb, blk), :] += o_b
        return st_f, st_b

    zero = jnp.zeros((HG_DIM, HG_DIM), F32)
    lax.fori_loop(0, nblk, body, (zero, zero))
    o = acc_ref[...]
    og = og_ref[0]
    o_ref[0] = o * _rms_scale(o, HG_DIM) * nw_ref[...] * _silu(og)


def _hgrn(p, lb, norm_w, nc, nl):
    b, s, _ = p.shape
    sec = lambda k: pl.BlockSpec((1, s, HG_DIM), lambda i, h, k=k: (i, 0, HG_HEADS * k + h))
    return pl.pallas_call(
        functools.partial(_hgrn_kernel, nc=nc, nl=nl),
        grid=(b, HG_HEADS),
        in_specs=[sec(0), sec(1), sec(2), sec(3), sec(4),
                  pl.BlockSpec((2, HG_DIM), lambda i, h: (0, h)),
                  pl.BlockSpec((1, HG_DIM), lambda i, h: (0, 0))],
        out_specs=pl.BlockSpec((1, s, HG_DIM), lambda i, h: (i, 0, h)),
        out_shape=jax.ShapeDtypeStruct((b, s, HG_W), F32),
        scratch_shapes=[pltpu.VMEM((s, HG_DIM), F32)],
        compiler_params=_cp(("arbitrary", "arbitrary")),
    )(p, p, p, p, p, lb, norm_w.reshape(1, HG_DIM))


def _pool_kernel(x_ref, w_ref, sc_ref, o_ref, *, lc, ll):
    z = jnp.zeros((POOL_PAD, POOL_GROUP), F32)
    n = lc + ll + 3 * POOL_PAD
    pos_c = lax.broadcasted_iota(jnp.int32, (lc, 1), 0)
    pos_l = lax.broadcasted_iota(jnp.int32, (ll, 1), 0)
    for g, w in enumerate(POOL_WINDOWS):
        x = x_ref[0, :, g * POOL_GROUP:(g + 1) * POOL_GROUP]
        xp = jnp.concatenate([z, x[:lc], z, x[lc:], z], axis=0)
        acc = xp + pltpu.roll(xp, 1, 0)
        half = 1
        while 2 * half < w:
            acc = pltpu.roll(acc, half, 0) + pltpu.roll(acc, n - half, 0)
            half *= 2
        win = jnp.concatenate([acc[POOL_PAD:POOL_PAD + lc],
                               acc[2 * POOL_PAD + lc:2 * POOL_PAD + lc + ll]], axis=0)

        def count(pos, length):
            lo = jnp.maximum(pos - w // 2, 0)
            hi = jnp.minimum(pos + w - w // 2, length)
            return (hi - lo).astype(F32)

        cnt = jnp.concatenate([count(pos_c, lc), count(pos_l, ll)], axis=0)
        d = win / cnt - x
        y = _dot(d.astype(BF16), w_ref[g])
        o_ref[0, :, g * POOL_GROUP:(g + 1) * POOL_GROUP] = y * sc_ref[:, g * POOL_GROUP:(g + 1) * POOL_GROUP]


def _pool(p, pool_w, pool_scale, lc, ll):
    b, s, n = p.shape
    width = len(POOL_WINDOWS) * POOL_GROUP
    return pl.pallas_call(
        functools.partial(_pool_kernel, lc=lc, ll=ll),
        grid=(b,),
        in_specs=[pl.BlockSpec((1, s, width), lambda i: (i, 0, n // width - 1)),
                  pl.BlockSpec(pool_w.shape, lambda i: (0, 0, 0)),
                  pl.BlockSpec((1, width), lambda i: (0, 0))],
        out_specs=pl.BlockSpec((1, s, width), lambda i: (i, 0, 0)),
        out_shape=jax.ShapeDtypeStruct((b, s, width), F32),
        compiler_params=_cp(("arbitrary",)),
    )(p, pool_w.astype(BF16), pool_scale.reshape(1, width))


def _mixout_kernel(a_ref, b_ref, h_ref, mod_ref, wa_ref, wb_ref, wr_ref, br_ref,
                   h1_ref, m_ref, lg_ref):
    mod = mod_ref[0, 0]
    y = _dot(a_ref[0].astype(BF16), wa_ref[...]) + _dot(b_ref[0].astype(BF16), wb_ref[...])
    h1 = h_ref[0] + mod[2:3] * y
    h1_ref[0] = h1
    m = h1 * _rms_scale(h1, h1.shape[-1]) * (1.0 + mod[4:5]) + mod[3:4]
    _store_row_tiles(m_ref, 0, m)
    mh, mm, ml = _split3(m)
    wh, wm, wl = wr_ref[0], wr_ref[1], wr_ref[2]
    lg = (_dot_nt(mh, wh) + _dot_nt(mh, wm) + _dot_nt(mm, wh)
          + _dot_nt(mh, wl) + _dot_nt(ml, wh) + _dot_nt(mm, wm))
    lg_ref[0] = lg + br_ref[...]


def _mixout(a, bmix, h, modtab, wa, wb, wr3, br, sel, h_off, rows):
    b = h.shape[0]
    d = h.shape[-1]
    ka, kb = a.shape[-1], bmix.shape[-1]
    out3 = lambda n: jax.ShapeDtypeStruct((b, rows, n), F32)
    return pl.pallas_call(
        _mixout_kernel,
        grid=(b, rows // ROW_TILE),
        in_specs=[
            pl.BlockSpec((1, ROW_TILE, ka), lambda i, j: (i, j, 0)),
            pl.BlockSpec((1, ROW_TILE, kb), lambda i, j: (i, j, 0)),
            pl.BlockSpec((1, ROW_TILE, d), lambda i, j: (i, j + h_off, 0)),
            pl.BlockSpec((1, 1, 6, d), lambda i, j: (i, sel(j), 0, 0)),
            pl.BlockSpec((ka, d), lambda i, j: (0, 0)),
            pl.BlockSpec((kb, d), lambda i, j: (0, 0)),
            pl.BlockSpec((3, N_EXPERTS, d), lambda i, j: (0, 0, 0)),
            pl.BlockSpec((1, N_EXPERTS), lambda i, j: (0, 0)),
        ],
        out_specs=[
            pl.BlockSpec((1, ROW_TILE, d), lambda i, j: (i, j, 0)),
            pl.BlockSpec((ROW_TILE * ROW_SPLIT, LANES), lambda i, j: (i * (rows // ROW_TILE) + j, 0)),
            pl.BlockSpec((1, ROW_TILE, N_EXPERTS), lambda i, j: (i, j, 0)),
        ],
        out_shape=[out3(d), jax.ShapeDtypeStruct((b * rows * ROW_SPLIT, LANES), F32), out3(N_EXPERTS)],
        compiler_params=_cp(("arbitrary", "arbitrary")),
    )(a, bmix, h, modtab, wa, wb, wr3, br)


def _route_kernel(lg_ref, idx_ref, gate_ref, cnt_ref):
    t = lg_ref.shape[0]

    @pl.when(pl.program_id(0) == 0)
    def _():
        cnt_ref[...] = jnp.zeros_like(cnt_ref)

    lg = lg_ref[...]
    lane = lax.broadcasted_iota(jnp.int32, (t, N_EXPERTS), 1).astype(F32)
    hot, vals, idxs = [], [], []
    for _ in range(TOP_K):
        mx = jnp.max(lg, axis=-1, keepdims=True)
        ix = jnp.min(jnp.where(lg == mx, lane, float(N_EXPERTS)), axis=-1, keepdims=True)
        oh = lane == ix
        hot.append(oh)
        vals.append(mx)
        idxs.append(ix.astype(jnp.int32))
        lg = jnp.where(oh, -jnp.inf, lg)
    ex = [jnp.exp(v - vals[0]) for v in vals]
    den = ex[0] + ex[1] + ex[2] + ex[3]
    sel = jnp.where(hot[0] | hot[1] | hot[2] | hot[3], 1.0, 0.0)
    out_lane = lax.broadcasted_iota(jnp.int32, (t, LANES), 1)
    idx_o = jnp.zeros((t, LANES), jnp.int32)
    gate_o = jnp.zeros((t, LANES), F32)
    for k in range(TOP_K):
        idx_o = jnp.where(out_lane == k, idxs[k], idx_o)
        gate_o = jnp.where(out_lane == k, ex[k] / den, gate_o)
    idx_ref[...] = idx_o
    gate_ref[...] = gate_o
    cnt_ref[...] += jnp.sum(sel, axis=0, keepdims=True)


def _route(logits):
    t = logits.shape[0]
    tile = pl.BlockSpec((ROUTE_TILE, LANES), lambda i: (i, 0))
    return pl.pallas_call(
        _route_kernel,
        grid=(t // ROUTE_TILE,),
        in_specs=[pl.BlockSpec((ROUTE_TILE, N_EXPERTS), lambda i: (i, 0))],
        out_specs=[tile, tile, pl.BlockSpec((1, N_EXPERTS), lambda i: (0, 0))],
        out_shape=[jax.ShapeDtypeStruct((t, LANES), jnp.int32),
                   jax.ShapeDtypeStruct((t, LANES), F32),
                   jax.ShapeDtypeStruct((1, N_EXPERTS), F32)],
        compiler_params=_cp(("arbitrary",)),
    )(logits)


def _experts_kernel(be_ref, src0_ref, srcn_ref, dstp_ref, dstc_ref, m_ref, w1_ref, b1_ref, w2_ref,
                    b2_ref, yu_ref, xa, xb, ya, yb, w1b, w2b, gsem, ssem, *, spare):
    g = pl.program_id(0)
    last = pl.num_programs(0) - 1
    tile = xa.shape[0] // ROW_SPLIT
    xbuf = (xa, xb)
    ybuf = (ya, yb)

    def tile_rows(start):
        return pl.ds(pl.multiple_of(start, ROW_SPLIT), ROW_SPLIT)

    def gather(idx_ref, r, s):
        return pltpu.make_async_copy(m_ref.at[tile_rows(idx_ref[r])], xbuf[s].at[tile_rows(r * ROW_SPLIT)],
                                     gsem.at[s])

    def scatter(row, r, s):
        return pltpu.make_async_copy(ybuf[s].at[tile_rows(r * ROW_SPLIT)], yu_ref.at[tile_rows(row)],
                                     ssem.at[s])

    def wait_gather(s):
        pltpu.make_async_copy(m_ref.at[pl.ds(0, tile * ROW_SPLIT)], xbuf[s], gsem.at[s]).wait()

    def wait_scatter(s):
        pltpu.make_async_copy(ybuf[s], yu_ref.at[pl.ds(0, tile * ROW_SPLIT)], ssem.at[s]).wait()

    @pl.when(g == 0)
    def _():
        ya[...] = jnp.zeros_like(ya)
        yb[...] = jnp.zeros_like(yb)

        def first(r, c):
            gather(src0_ref, r, 0).start()
            scatter((spare + 3 * tile + r) * ROW_SPLIT, r, 0).start()
            return c

        lax.fori_loop(0, tile, first, 0)

    changed = (g == 0) | (be_ref[g] != be_ref[jnp.maximum(g - 1, 0)])

    @pl.when(changed)
    def _():
        w1b[...] = w1_ref[0, 0].astype(BF16)
        w2b[...] = w2_ref[0, 0].astype(BF16)

    def block(slot, other):
        wait_gather(slot)
        wait_scatter(slot)
        rows = tile // EXPERT_CHUNKS
        for c in range(EXPERT_CHUNKS):
            for r in range(c * rows, (c + 1) * rows):
                gather(srcn_ref, r, other).start()
                scatter(jnp.where(g > 0, dstp_ref[r], (spare + 2 * tile + r) * ROW_SPLIT), r, other).start()
            x = _load_row_tiles(xbuf[slot], c * rows, rows).astype(BF16)
            hb = _dot(x, w1b[...]) + b1_ref[0, 0]
            gl = jnp.minimum(hb[:, :D_FF], SWIGLU_LIMIT)
            lin = jnp.clip(hb[:, D_FF:], -SWIGLU_LIMIT, SWIGLU_LIMIT)
            a = gl * _sigmoid(SWIGLU_ALPHA * gl) * (lin + 1.0)
            _store_row_tiles(ybuf[slot], c * rows, _dot(a.astype(BF16), w2b[...]) + b2_ref[0, 0])

        @pl.when(g == last)
        def _():
            def final(r, c):
                scatter(dstc_ref[r], r, slot).start()
                return c

            lax.fori_loop(0, tile, final, 0)
            wait_scatter(other)
            wait_scatter(slot)
            wait_gather(other)

    @pl.when(g % 2 == 1)
    def _():
        block(1, 0)

    @pl.when(g % 2 == 0)
    def _():
        block(0, 1)


def _experts(m, blk_e, src_tok, dst_row, layer, w1, b1, w2, b2):
    t, d = m.shape[0] // ROW_SPLIT, D_MODEL
    _, e, _, f2 = w1.shape
    nb = src_tok.shape[0] // MOE_TILE
    spare = TOP_K * t
    idx = lambda fn: pl.BlockSpec((MOE_TILE,), fn, memory_space=pltpu.SMEM)
    grid_spec = pltpu.PrefetchScalarGridSpec(
        num_scalar_prefetch=1,
        grid=(nb,),
        in_specs=[
            idx(lambda i, be: (0,)),
            idx(lambda i, be: (jnp.minimum(i + 1, nb - 1),)),
            idx(lambda i, be: (jnp.maximum(i - 1, 0),)),
            idx(lambda i, be: (i,)),
            pl.BlockSpec(memory_space=pl.ANY),
            pl.BlockSpec((1, 1, d, f2), lambda i, be: (layer, be[i], 0, 0)),
            pl.BlockSpec((1, 1, 1, f2), lambda i, be: (layer, be[i], 0, 0)),
            pl.BlockSpec((1, 1, f2 // 2, d), lambda i, be: (layer, be[i], 0, 0)),
            pl.BlockSpec((1, 1, 1, d), lambda i, be: (layer, be[i], 0, 0)),
        ],
        out_specs=pl.BlockSpec(memory_space=pl.ANY),
        scratch_shapes=[pltpu.VMEM((MOE_TILE * ROW_SPLIT, LANES), F32)] * 4 + [
                        pltpu.VMEM((d, f2), BF16), pltpu.VMEM((f2 // 2, d), BF16),
                        pltpu.SemaphoreType.DMA((2,)), pltpu.SemaphoreType.DMA((2,))],
    )
    return pl.pallas_call(
        functools.partial(_experts_kernel, spare=spare),
        grid_spec=grid_spec,
        out_shape=jax.ShapeDtypeStruct(((spare + SPARE_BLOCKS * MOE_TILE) * ROW_SPLIT, LANES), F32),
        compiler_params=_cp(("arbitrary",)),
    )(blk_e, src_tok, src_tok, dst_row, dst_row, m, w1, b1.reshape(-1, e, 1, f2), w2,
      b2.reshape(-1, e, 1, d))


def _combine_kernel(y0_ref, y1_ref, y2_ref, y3_ref, gate_ref, h_ref, mod_ref, o_ref):
    gate = gate_ref[...]
    rows = h_ref.shape[1]
    f = gate[:, 0:1] * _load_row_tiles(y0_ref, 0, rows)
    for k, y_ref in ((1, y1_ref), (2, y2_ref), (3, y3_ref)):
        f = f + gate[:, k:k + 1] * _load_row_tiles(y_ref, 0, rows)
    o_ref[0] = h_ref[0] + mod_ref[0, 0][5:6] * f


def _combine(yu, gate, h1, modtab, sel, out_rows, out_map):
    b, r, d = h1.shape
    nblk = r // ROW_TILE
    tblk = b * nblk
    ysp = lambda k: pl.BlockSpec((ROW_TILE * ROW_SPLIT, LANES), lambda i, j, k=k: (k * tblk + i * nblk + j, 0))
    return pl.pallas_call(
        _combine_kernel,
        grid=(b, nblk),
        in_specs=[
            ysp(0), ysp(1), ysp(2), ysp(3),
            pl.BlockSpec((ROW_TILE, LANES), lambda i, j: (i * nblk + j, 0)),
            pl.BlockSpec((1, ROW_TILE, d), lambda i, j: (i, j, 0)),
            pl.BlockSpec((1, 1, 6, d), lambda i, j: (i, sel(j), 0, 0)),
        ],
        out_specs=pl.BlockSpec((1, ROW_TILE, d), lambda i, j: (i, out_map(j), 0)),
        out_shape=jax.ShapeDtypeStruct((b, out_rows, d), F32),
        compiler_params=_cp(("arbitrary", "arbitrary")),
    )(yu, yu, yu, yu, gate, h1, modtab)


def _moe(m, logits, h1, modtab, sel, layer, w1, b1, w2, b2, out_rows, out_map):
    b, r, d = h1.shape
    t = b * r
    n_assign = t * TOP_K
    idx, gate, counts = _route(logits.reshape(t, N_EXPERTS))
    counts = counts[0].astype(jnp.int32)
    n_blocks = n_assign // MOE_TILE + N_EXPERTS
    n_slots = n_blocks * MOE_TILE
    padded = (counts + MOE_TILE - 1) // MOE_TILE * MOE_TILE
    pad_end = jnp.cumsum(padded)
    fill_end = jnp.cumsum(padded - counts)
    fill = jnp.arange(n_slots - n_assign, dtype=jnp.int32)
    fill_e = jnp.sum(fill[:, None] >= fill_end[None, :], axis=1).astype(jnp.int32)
    keys = jnp.concatenate([idx[:, :TOP_K].reshape(-1), fill_e])
    order = jnp.argsort(keys, stable=True).astype(jnp.int32)
    real = order < n_assign
    src_tok = jnp.where(real, order // TOP_K, 0)
    spare_row = n_assign + jnp.arange(n_slots, dtype=jnp.int32) % (2 * MOE_TILE)
    dst_row = jnp.where(real, (order % TOP_K) * t + order // TOP_K, spare_row)
    blk_start = jnp.arange(n_blocks, dtype=jnp.int32) * MOE_TILE
    blk_e = jnp.minimum(jnp.sum(blk_start[:, None] >= pad_end[None, :], axis=1), N_EXPERTS - 1).astype(jnp.int32)
    yu = _experts(m, blk_e, src_tok * ROW_SPLIT, dst_row * ROW_SPLIT, layer, w1, b1, w2, b2)
    return _combine(yu, gate, h1, modtab, sel, out_rows, out_map)


def _conv_kernel(a_ref, dw_ref, db_ref, lw_ref, lb_ref, o_ref, hs_ref, *, ll):
    hs_ref[0:CONV_PAD, :] = jnp.zeros((CONV_PAD, CONV_WIDTH), F32)
    hs_ref[CONV_PAD + ll:2 * CONV_PAD + ll, :] = jnp.zeros((CONV_PAD, CONV_WIDTH), F32)
    hs_ref[CONV_PAD:CONV_PAD + ll, :] = a_ref[0, :, :CONV_WIDTH] * _sigmoid(a_ref[0, :, CONV_WIDTH:])
    first = CONV_PAD - CONV_K // 2

    def tile(i, c):
        r0 = pl.multiple_of(i * ROW_TILE, ROW_TILE)
        win = hs_ref[pl.ds(r0, ROW_TILE + 2 * CONV_PAD), :]
        acc = jnp.zeros((ROW_TILE, CONV_WIDTH), F32)
        for k in range(CONV_K):
            acc = acc + dw_ref[k:k + 1, :] * win[first + k:first + k + ROW_TILE]
        acc = acc + db_ref[...]
        mu = jnp.mean(acc, axis=-1, keepdims=True)
        dlt = acc - mu
        var = jnp.mean(dlt * dlt, axis=-1, keepdims=True)
        y = dlt * lax.rsqrt(var + EPS) * lw_ref[...] + lb_ref[...]
        o_ref[0, pl.ds(r0, ROW_TILE), :] = _silu(y)
        return c

    lax.fori_loop(0, ll // ROW_TILE, tile, 0)


def _conv(p, dw_w, dw_b, ln_w, ln_b, ll):
    b = p.shape[0]
    row = lambda v: v.reshape(1, CONV_WIDTH)
    vec = pl.BlockSpec((1, CONV_WIDTH), lambda i: (0, 0))
    return pl.pallas_call(
        functools.partial(_conv_kernel, ll=ll),
        grid=(b,),
        in_specs=[pl.BlockSpec((1, ll, 2 * CONV_WIDTH), lambda i: (i, 0, 0)),
                  pl.BlockSpec((CONV_K, CONV_WIDTH), lambda i: (0, 0)), vec, vec, vec],
        out_specs=pl.BlockSpec((1, ll, CONV_WIDTH), lambda i: (i, 0, 0)),
        out_shape=jax.ShapeDtypeStruct((b, ll, CONV_WIDTH), F32),
        scratch_shapes=[pltpu.VMEM((ll + 2 * CONV_PAD, CONV_WIDTH), F32)],
        compiler_params=_cp(("arbitrary",)),
    )(p, dw_w, row(dw_b), row(ln_w), row(ln_b))


def _mla_proj_kernel(cq_ref, ckv_ref, kr_ref, cs_ref, qa_ref, kva_ref, wq_ref, wqs_ref, wk_ref,
                     wv_ref, ek_ref, eks_ref, gq_ref, gqs_ref, gk_ref, gks_ref,
                     q_ref, k_ref, v_ref):
    cos = cs_ref[0]
    sin = cs_ref[1]
    cq = cq_ref[0]
    cqn = (cq * _rms_scale(cq, MLA_Q_RANK) * qa_ref[...]).astype(BF16)
    qx = _dot(cqn, wq_ref[...])
    qs = _dot(cqn, wqs_ref[...])
    ckv = ckv_ref[0]
    ckvn = (ckv * _rms_scale(ckv, MLA_KV_RANK) * kva_ref[...]).astype(BF16)
    kr = kr_ref[0]
    rh = kr.astype(BF16)
    rl = (kr - rh.astype(F32)).astype(BF16)
    kx = _dot(ckvn, wk_ref[...]) + _dot(rh, ek_ref[...]) + _dot(rl, ek_ref[...])
    ks = _dot(rh, eks_ref[...]) + _dot(rl, eks_ref[...])
    v_ref[0] = _dot(ckvn, wv_ref[...]).astype(BF16)
    scale = MLA_QK ** -0.5
    for h in range(MLA_HEADS):
        sl = slice(h * HEAD_PAD, (h + 1) * HEAD_PAD)
        qh = qx[:, sl]
        q_ref[0, :, sl] = (_rms_scale(qh, MLA_QK) * scale
                           * (qh * gq_ref[...] * cos + qs[:, sl] * gqs_ref[...] * sin)).astype(BF16)
        kh = kx[:, sl]
        k_ref[0, :, sl] = (_rms_scale(kh, MLA_QK)
                           * (kh * gk_ref[...] * cos + ks[:, sl] * gks_ref[...] * sin)).astype(BF16)


def _mla_proj(p, cs, qa, kva, wq, wqs, wk, wv, ek, eks, gq, gqs, gk, gks):
    b, s, _ = p.shape
    hw = MLA_HEADS * HEAD_PAD
    full = lambda a: pl.BlockSpec(a.shape, lambda i, j: (0,) * a.ndim)
    out = pl.BlockSpec((1, ROW_TILE, hw), lambda i, j: (i, j, 0))
    consts = (qa, kva, wq, wqs, wk, wv, ek, eks, gq, gqs, gk, gks)
    return pl.pallas_call(
        _mla_proj_kernel,
        grid=(b, s // ROW_TILE),
        in_specs=[pl.BlockSpec((1, ROW_TILE, 512), lambda i, j: (i, j, OD_CQ // 512)),
                  pl.BlockSpec((1, ROW_TILE, MLA_KV_RANK), lambda i, j: (i, j, OD_CKV // MLA_KV_RANK)),
                  pl.BlockSpec((1, ROW_TILE, LANES), lambda i, j: (i, j, OD_KR // LANES)),
                  pl.BlockSpec((2, ROW_TILE, HEAD_PAD), lambda i, j: (0, j, 0))]
                 + [full(a) for a in consts],
        out_specs=[out, out, out],
        out_shape=[jax.ShapeDtypeStruct((b, s, hw), BF16)] * 3,
        compiler_params=_cp(("arbitrary", "arbitrary")),
    )(p, p, p, cs, *consts)


def _attn_kernel(q_ref, k_ref, v_ref, o_ref):
    def scores(h):
        sl = slice(h * HEAD_PAD, (h + 1) * HEAD_PAD)
        return _dot_nt(q_ref[0, :, sl], k_ref[0, :, sl])

    s = scores(0)
    for h in range(MLA_HEADS):
        s_next = scores(h + 1) if h + 1 < MLA_HEADS else None
        sl = slice(h * HEAD_PAD, (h + 1) * HEAD_PAD)
        p = jnp.exp(s - jnp.max(s, axis=-1, keepdims=True))
        den = jnp.sum(p, axis=-1, keepdims=True)
        o_ref[0, :, sl] = (_dot(p.astype(BF16), v_ref[0, :, sl]) / den).astype(BF16)
        s = s_next


def _attention(q, k, v, ll):
    b, s, hw = k.shape
    kv = pl.BlockSpec((1, s, hw), lambda i, j: (i, 0, 0))
    qo = pl.BlockSpec((1, Q_TILE, hw), lambda i, j: (i, j, 0))
    return pl.pallas_call(
        _attn_kernel,
        grid=(b, ll // Q_TILE),
        in_specs=[qo, kv, kv],
        out_specs=qo,
        out_shape=jax.ShapeDtypeStruct((b, ll, hw), BF16),
        compiler_params=_cp(("arbitrary", "arbitrary")),
    )(q, k, v)


def _place(parts, lead):
    out = jnp.zeros(lead + (MLA_HEADS, HEAD_PAD), F32)
    for arr, off in parts:
        out = out.at[..., off:off + arr.shape[-1]].set(arr)
    return out.reshape(lead + (MLA_HEADS * HEAD_PAD,))


def _swap_pairs(a):
    g = a.reshape(a.shape[:-1] + (2, 2, ROPE_AXIS_HALF))
    return jnp.flip(g, axis=-2).reshape(a.shape)


def _rope_tables(rows_total, ll):
    grid_rows = ll // GRID_W
    t_row = jnp.repeat(jnp.arange(grid_rows, dtype=F32), GRID_W)
    t_col = jnp.tile(jnp.arange(GRID_W, dtype=F32), grid_rows)
    inv = 1.0 / (ROPE_BASE ** (jnp.arange(ROPE_AXIS_HALF, dtype=F32) / ROPE_AXIS_HALF))
    ang_r = t_row[:, None] * inv
    ang_c = t_col[:, None] * inv
    cos32 = jnp.concatenate([jnp.cos(ang_r), jnp.cos(ang_r), jnp.cos(ang_c), jnp.cos(ang_c)], axis=-1)
    sin32 = jnp.concatenate([-jnp.sin(ang_r), jnp.sin(ang_r), -jnp.sin(ang_c), jnp.sin(ang_c)], axis=-1)
    extra = rows_total - ll
    cos32 = jnp.concatenate([cos32, jnp.ones((extra, MLA_ROPE), F32)], axis=0)
    sin32 = jnp.concatenate([sin32, jnp.zeros((extra, MLA_ROPE), F32)], axis=0)
    cos_t = jnp.concatenate([jnp.ones((rows_total, MLA_NOPE), F32), cos32,
                             jnp.zeros((rows_total, HEAD_PAD - MLA_QK), F32)], axis=-1)
    sin_t = jnp.concatenate([jnp.zeros((rows_total, MLA_NOPE), F32), sin32,
                             jnp.zeros((rows_total, HEAD_PAD - MLA_QK), F32)], axis=-1)
    return jnp.stack([cos_t, sin_t])


def kernel(x, c, ctx, c_ctx, ada_w, ada_b, ev_w_in, hgrn_lb_logits, hgrn_norm_w, pool_w, pool_scale,
           ev_w_out, od_w_in, conv_dw_w, conv_dw_b, conv_ln_w, conv_ln_b, mla_q_a_norm, mla_w_uq,
           mla_kv_a_norm, mla_w_ukv, mla_q_norm, mla_k_norm, od_w_out, moe_router_w, moe_router_b,
           moe_w1, moe_b1, moe_w2, moe_b2):
    b, ll, d = x.shape
    lc = ctx.shape[1]
    s = ll + lc
    nc, nl = lc // ROW_TILE, ll // ROW_TILE
    assert ada_w.shape[0] == 2 and d == D_MODEL
    assert ll % Q_TILE == 0 and lc % ROW_TILE == 0 and (b * s) % ROUTE_TILE == 0 and (b * ll) % ROUTE_TILE == 0

    cond_rows = (b + 1 + 7) // 8 * 8
    cond = jnp.zeros((cond_rows, d), F32).at[:b].set(c).at[b].set(c_ctx)
    mod = _ada_table(cond, ada_w, ada_b)
    modtab = []
    for layer in range(2):
        lat = mod[layer, :b].reshape(b, 1, 6, d)
        con = jnp.broadcast_to(mod[layer, b].reshape(1, 1, 6, d), (b, 1, 6, d))
        modtab.append(jnp.concatenate([con, lat], axis=1))
    lb_all = jnp.cumsum(jax.nn.softmax(hgrn_lb_logits.astype(F32), axis=0), axis=0)

    def router_pieces(w):
        hi, mid, lo = _split3(w.T)
        return jnp.stack([hi, mid, lo])

    sel0 = lambda j: jnp.where(j < nc, 0, 1)
    h = jnp.concatenate([ctx, x], axis=1)
    p0 = _modproj(h, modtab[0], ev_w_in[0].astype(BF16), sel0)
    hg = _hgrn(p0, lb_all[0], hgrn_norm_w[0], nc, nl)
    yp = _pool(p0, pool_w[0], pool_scale[0], lc, ll)
    wo = ev_w_out[0].astype(BF16)
    h1, m, lg = _mixout(hg, yp, h, modtab[0], wo[:HG_W], wo[HG_W:], router_pieces(moe_router_w[0]),
                        moe_router_b[0].reshape(1, N_EXPERTS), sel0, 0, s)
    to_l1 = lambda j: jnp.where(j < nc, j + nl, j - nc)
    h = _moe(m, lg, h1, modtab[0], sel0, 0, moe_w1, moe_b1, moe_w2, moe_b2, s, to_l1)

    sel1 = lambda j: jnp.where(j < nl, 1, 0)
    w_in = od_w_in[0]
    kr_cols = w_in[:, 1664:1696]
    w1p = jnp.zeros((d, OD_IN_PAD), F32)
    w1p = w1p.at[:, :1024].set(w_in[:, :1024]).at[:, OD_CQ:OD_CQ + MLA_Q_RANK].set(w_in[:, 1024:1408])
    w1p = w1p.at[:, OD_CKV:OD_CKV + MLA_KV_RANK].set(w_in[:, 1408:1664])
    w1p = w1p.at[:, OD_KR:OD_KR + MLA_ROPE].set(kr_cols)
    w1p = w1p.at[:, OD_KR + MLA_ROPE:OD_KR + 2 * MLA_ROPE].set(_swap_pairs(kr_cols))
    p1 = _modproj(h, modtab[1], w1p.astype(BF16), sel1)
    hc = _conv(p1, conv_dw_w[0], conv_dw_b[0], conv_ln_w[0], conv_ln_b[0], ll)

    wuq = mla_w_uq[0].reshape(MLA_Q_RANK, MLA_HEADS, MLA_QK)
    wuq_rope = wuq[..., MLA_NOPE:]
    pad_rows = lambda w: jnp.zeros((512, w.shape[1]), F32).at[:MLA_Q_RANK].set(w)
    wq = pad_rows(_place([(wuq, 0)], (MLA_Q_RANK,))).astype(BF16)
    wqs = pad_rows(_place([(_swap_pairs(wuq_rope), MLA_NOPE)], (MLA_Q_RANK,))).astype(BF16)
    wukv = mla_w_ukv[0].reshape(MLA_KV_RANK, MLA_HEADS, MLA_NOPE + MLA_V)
    wk = _place([(wukv[..., :MLA_NOPE], 0)], (MLA_KV_RANK,)).astype(BF16)
    wv = _place([(wukv[..., MLA_NOPE:], 0)], (MLA_KV_RANK,)).astype(BF16)
    eye = jnp.broadcast_to(jnp.eye(MLA_ROPE, dtype=F32)[:, None, :], (MLA_ROPE, MLA_HEADS, MLA_ROPE))
    zero_rows = jnp.zeros((LANES - 2 * MLA_ROPE, MLA_HEADS * HEAD_PAD), F32)
    place_kr = _place([(eye, MLA_NOPE)], (MLA_ROPE,))
    zero_kr = jnp.zeros_like(place_kr)
    ek = jnp.concatenate([place_kr, zero_kr, zero_rows], axis=0).astype(BF16)
    eks = jnp.concatenate([zero_kr, place_kr, zero_rows], axis=0).astype(BF16)
    lane_gain = lambda g: jnp.concatenate([g, jnp.zeros((HEAD_PAD - MLA_QK,), F32)]).reshape(1, HEAD_PAD)
    swapped_gain = lambda g: jnp.concatenate(
        [jnp.zeros((MLA_NOPE,), F32), _swap_pairs(g[MLA_NOPE:]), jnp.zeros((HEAD_PAD - MLA_QK,), F32)]
    ).reshape(1, HEAD_PAD)
    qa = jnp.concatenate([mla_q_a_norm[0], jnp.zeros((512 - MLA_Q_RANK,), F32)]).reshape(1, 512)
    q, k, v = _mla_proj(p1, _rope_tables(s, ll), qa, mla_kv_a_norm[0].reshape(1, MLA_KV_RANK),
                        wq, wqs, wk, wv, ek, eks,
                        lane_gain(mla_q_norm[0]), swapped_gain(mla_q_norm[0]),
                        lane_gain(mla_k_norm[0]), swapped_gain(mla_k_norm[0]))
    attn = _attention(q, k, v, ll)

    wo1 = od_w_out[0]
    wo_attn = wo1[CONV_WIDTH:].reshape(MLA_HEADS, MLA_V, d)
    wo_attn = jnp.zeros((MLA_HEADS, HEAD_PAD, d), F32).at[:, :MLA_V].set(wo_attn).reshape(-1, d)
    h1, m, lg = _mixout(hc, attn, h, modtab[1], wo1[:CONV_WIDTH].astype(BF16), wo_attn.astype(BF16),
                        router_pieces(moe_router_w[1]), moe_router_b[1].reshape(1, N_EXPERTS),
                        lambda j: 1, 0, ll)
    return _moe(m, lg, h1, modtab[1], lambda j: 1, 1, moe_w1, moe_b1, moe_w2, moe_b2, ll, lambda j: j)
```

```python
import functools

import jax
import jax.numpy as jnp
from jax import lax
from jax.experimental import pallas as pl
from jax.experimental.pallas import tpu as pltpu

F32 = jnp.float32
BF16 = jnp.bfloat16

D_MODEL = 1024
GRID_W = 64
EPS = 1e-6
HG_HEADS = 4
HG_DIM = 128
HG_W = HG_HEADS * HG_DIM
HG_SUB = 32
POOL_WINDOWS = (2, 4, 8, 16)
POOL_GROUP = 128
POOL_PAD = 16
CONV_WIDTH = 512
CONV_K = 31
CONV_PAD = 16
MLA_HEADS = 8
MLA_Q_RANK = 384
MLA_KV_RANK = 256
MLA_NOPE = 64
MLA_ROPE = 32
MLA_V = 64
MLA_QK = MLA_NOPE + MLA_ROPE
HEAD_PAD = 128
ROPE_AXIS_HALF = MLA_ROPE // 4
ROPE_BASE = 10000.0
N_EXPERTS = 32
TOP_K = 4
D_FF = 1024
SWIGLU_LIMIT = 7.0
SWIGLU_ALPHA = 1.702

ROW_TILE = 256
ROUTE_TILE = 512
MOE_TILE = 512
EXPERT_CHUNKS = 4
SPARE_BLOCKS = 4
Q_TILE = 512
LANES = 128
ROW_SPLIT = D_MODEL // LANES
VMEM_LIMIT = 56 * 1024 * 1024

OD_CQ = 1024
OD_CKV = 1536
OD_KR = 1792
OD_IN_PAD = 1920


def _cp(sem, vmem=VMEM_LIMIT):
    return pltpu.CompilerParams(dimension_semantics=sem, vmem_limit_bytes=vmem)


def _dot(a, b):
    return jnp.dot(a, b, preferred_element_type=F32)


def _dot_nt(a, b):
    return lax.dot_general(a, b, (((1,), (1,)), ((), ())), preferred_element_type=F32)


def _dot_tn(a, b):
    return lax.dot_general(a, b, (((0,), (0,)), ((), ())), preferred_element_type=F32)


def _split3(x):
    hi = x.astype(BF16)
    r1 = x - hi.astype(F32)
    mid = r1.astype(BF16)
    lo = (r1 - mid.astype(F32)).astype(BF16)
    return hi, mid, lo


def _sigmoid(x):
    return 1.0 / (1.0 + jnp.exp(-x))


def _silu(x):
    return x * _sigmoid(x)


def _load_row_tiles(ref, row0, rows):
    return jnp.concatenate(
        [ref[pl.ds(row0 * ROW_SPLIT + s, rows, stride=ROW_SPLIT), :] for s in range(ROW_SPLIT)], axis=1)


def _store_row_tiles(ref, row0, x):
    for s in range(ROW_SPLIT):
        ref[pl.ds(row0 * ROW_SPLIT + s, x.shape[0], stride=ROW_SPLIT), :] = x[:, s * LANES:(s + 1) * LANES]


def _rms_scale(x, width):
    return lax.rsqrt(jnp.sum(x * x, axis=-1, keepdims=True) * (1.0 / width) + EPS)


def _ada_kernel(c_ref, w_ref, b_ref, o_ref):
    s = _silu(c_ref[...])
    sh, sm, sl = _split3(s)
    wh, wm, wl = _split3(w_ref[0])
    acc = _dot(sh, wh) + _dot(sh, wm) + _dot(sm, wh) + _dot(sh, wl) + _dot(sl, wh) + _dot(sm, wm)
    o_ref[0] = acc + b_ref[0]


def _ada_table(cond, ada_w, ada_b):
    depth, d, n = ada_w.shape
    rows = cond.shape[0]
    nb = n // d
    return pl.pallas_call(
        _ada_kernel,
        grid=(depth, nb),
        in_specs=[
            pl.BlockSpec((rows, d), lambda l, j: (0, 0)),
            pl.BlockSpec((1, d, d), lambda l, j: (l, 0, j)),
            pl.BlockSpec((1, 1, d), lambda l, j: (l, 0, j)),
        ],
        out_specs=pl.BlockSpec((1, rows, d), lambda l, j: (l, 0, j)),
        out_shape=jax.ShapeDtypeStruct((depth, rows, n), F32),
        compiler_params=_cp(("arbitrary", "arbitrary")),
    )(cond, ada_w, ada_b.reshape(depth, 1, n))


def _modproj_kernel(h_ref, mod_ref, w_ref, o_ref):
    x = h_ref[0]
    mod = mod_ref[0, 0]
    u = x * _rms_scale(x, x.shape[-1]) * (1.0 + mod[1:2]) + mod[0:1]
    o_ref[0] = _dot(u.astype(BF16), w_ref[...])


def _modproj(h, modtab, w, sel):
    b, s, d = h.shape
    n = w.shape[1]
    return pl.pallas_call(
        _modproj_kernel,
        grid=(b, s // ROW_TILE),
        in_specs=[
            pl.BlockSpec((1, ROW_TILE, d), lambda i, j: (i, j, 0)),
            pl.BlockSpec((1, 1, 6, d), lambda i, j: (i, sel(j), 0, 0)),
            pl.BlockSpec((d, n), lambda i, j: (0, 0)),
        ],
        out_specs=pl.BlockSpec((1, ROW_TILE, n), lambda i, j: (i, j, 0)),
        out_shape=jax.ShapeDtypeStruct((b, s, n), F32),
        compiler_params=_cp(("arbitrary", "arbitrary")),
    )(h, modtab, w)


def _hgrn_block(q, z, v, lb, st, reverse):
    rows = q.shape[0]
    nsub = rows // HG_SUB
    f = lb + (1.0 - lb) * _sigmoid(z)
    kin = 1.0 - f
    logf = jnp.log(f)
    ri = lax.broadcasted_iota(jnp.int32, (rows, rows), 0)
    ci = lax.broadcasted_iota(jnp.int32, (rows, rows), 1)
    same = (ri // HG_SUB) == (ci // HG_SUB)
    causal = same & ((ci >= ri) if reverse else (ci <= ri))
    tri = jnp.where(causal, 1.0, 0.0).astype(BF16)
    lh, lm, ll = _split3(logf)
    bcum = _dot(tri, lh) + _dot(tri, lm) + _dot(tri, ll)
    b3 = bcum.reshape(nsub, HG_SUB, HG_DIM)
    tot_row = 0 if reverse else HG_SUB - 1
    mid_row = HG_SUB // 2 if reverse else HG_SUB // 2 - 1
    tot = b3[:, tot_row:tot_row + 1, :]
    mid = b3[:, mid_row:mid_row + 1, :]
    q3 = q.reshape(nsub, HG_SUB, HG_DIM)
    k3 = kin.reshape(nsub, HG_SUB, HG_DIM)
    q_dec = (q3 * jnp.exp(b3)).astype(BF16)
    k_end = (k3 * jnp.exp(tot - b3)).astype(BF16)
    q_mid = (q3 * jnp.exp(b3 - mid)).reshape(rows, HG_DIM).astype(BF16)
    k_mid = (k3 * jnp.exp(mid - b3)).reshape(rows, HG_DIM).astype(BF16)
    vb = v.astype(BF16)
    att = jnp.where(causal, _dot_nt(q_mid, k_mid), 0.0).astype(BF16)
    o_intra = _dot(att, vb)
    v3 = vb.reshape(nsub, HG_SUB, HG_DIM)
    dec = jnp.exp(tot)
    upd = [_dot_tn(v3[j], k_end[j]) for j in range(nsub)]
    order = range(nsub - 1, -1, -1) if reverse else range(nsub)
    o_inter = [None] * nsub
    for j in order:
        o_inter[j] = _dot_nt(q_dec[j], st.astype(BF16))
        st = st * dec[j] + upd[j]
    o = o_intra + jnp.concatenate(o_inter, axis=0)
    return o, st


def _hgrn_kernel(q_ref, zf_ref, zb_ref, v_ref, og_ref, lb_ref, nw_ref, o_ref, acc_ref, *, nc, nl):
    blk = ROW_TILE
    nblk = nc + nl
    acc_ref[...] = jnp.zeros_like(acc_ref)
    lb_f = lb_ref[0:1, :]
    lb_b = lb_ref[1:2, :]

    def body(i, carry):
        st_f, st_b = carry
        rf = pl.multiple_of(i * blk, blk)
        ib = jnp.where(i < nc, nc - 1 - i, 2 * nc + nl - 1 - i)
        rb = pl.multiple_of(ib * blk, blk)
        qf = _silu(q_ref[0, pl.ds(rf, blk), :])
        o_f, st_f = _hgrn_block(qf, zf_ref[0, pl.ds(rf, blk), :], v_ref[0, pl.ds(rf, blk), :],
                                lb_f, st_f, False)
        acc_ref[pl.ds(rf, blk), :] += o_f
        qb = _silu(q_ref[0, pl.ds(rb, blk), :])
        o_b, st_b = _hgrn_block(qb, zb_ref[0, pl.ds(rb, blk), :], v_ref[0, pl.ds(rb, blk), :],
                                lb_b, st_b, True)
        acc_ref[pl.ds(rb, blk), :] += o_b
        return st_f, st_b

    zero = jnp.zeros((HG_DIM, HG_DIM), F32)
    lax.fori_loop(0, nblk, body, (zero, zero))
    o = acc_ref[...]
    og = og_ref[0]
    o_ref[0] = o * _rms_scale(o, HG_DIM) * nw_ref[...] * _silu(og)


def _hgrn(p, lb, norm_w, nc, nl):
    b, s, _ = p.shape
    sec = lambda k: pl.BlockSpec((1, s, HG_DIM), lambda i, h, k=k: (i, 0, HG_HEADS * k + h))
    return pl.pallas_call(
        functools.partial(_hgrn_kernel, nc=nc, nl=nl),
        grid=(b, HG_HEADS),
        in_specs=[sec(0), sec(1), sec(2), sec(3), sec(4),
                  pl.BlockSpec((2, HG_DIM), lambda i, h: (0, h)),
                  pl.BlockSpec((1, HG_DIM), lambda i, h: (0, 0))],
        out_specs=pl.BlockSpec((1, s, HG_DIM), lambda i, h: (i, 0, h)),
        out_shape=jax.ShapeDtypeStruct((b, s, HG_W), F32),
        scratch_shapes=[pltpu.VMEM((s, HG_DIM), F32)],
        compiler_params=_cp(("arbitrary", "arbitrary")),
    )(p, p, p, p, p, lb, norm_w.reshape(1, HG_DIM))


def _pool_kernel(x_ref, w_ref, sc_ref, o_ref, *, lc, ll):
    z = jnp.zeros((POOL_PAD, POOL_GROUP), F32)
    n = lc + ll + 3 * POOL_PAD
    pos_c = lax.broadcasted_iota(jnp.int32, (lc, 1), 0)
    pos_l = lax.broadcasted_iota(jnp.int32, (ll, 1), 0)
    for g, w in enumerate(POOL_WINDOWS):
        x = x_ref[0, :, g * POOL_GROUP:(g + 1) * POOL_GROUP]
        xp = jnp.concatenate([z, x[:lc], z, x[lc:], z], axis=0)
        acc = xp + pltpu.roll(xp, 1, 0)
        half = 1
        while 2 * half < w:
            acc = pltpu.roll(acc, half, 0) + pltpu.roll(acc, n - half, 0)
            half *= 2
        win = jnp.concatenate([acc[POOL_PAD:POOL_PAD + lc],
                               acc[2 * POOL_PAD + lc:2 * POOL_PAD + lc + ll]], axis=0)

        def count(pos, length):
            lo = jnp.maximum(pos - w // 2, 0)
            hi = jnp.minimum(pos + w - w // 2, length)
            return (hi - lo).astype(F32)

        cnt = jnp.concatenate([count(pos_c, lc), count(pos_l, ll)], axis=0)
        d = win / cnt - x
        y = _dot(d.astype(BF16), w_ref[g])
        o_ref[0, :, g * POOL_GROUP:(g + 1) * POOL_GROUP] = y * sc_ref[:, g * POOL_GROUP:(g + 1) * POOL_GROUP]


def _pool(p, pool_w, pool_scale, lc, ll):
    b, s, n = p.shape
    width = len(POOL_WINDOWS) * POOL_GROUP
    return pl.pallas_call(
        functools.partial(_pool_kernel, lc=lc, ll=ll),
        grid=(b,),
        in_specs=[pl.BlockSpec((1, s, width), lambda i: (i, 0, n // width - 1)),
                  pl.BlockSpec(pool_w.shape, lambda i: (0, 0, 0)),
                  pl.BlockSpec((1, width), lambda i: (0, 0))],
        out_specs=pl.BlockSpec((1, s, width), lambda i: (i, 0, 0)),
        out_shape=jax.ShapeDtypeStruct((b, s, width), F32),
        compiler_params=_cp(("arbitrary",)),
    )(p, pool_w.astype(BF16), pool_scale.reshape(1, width))


def _mixout_kernel(a_ref, b_ref, h_ref, mod_ref, wa_ref, wb_ref, wr_ref, br_ref,
                   h1_ref, m_ref, lg_ref):
    mod = mod_ref[0, 0]
    y = _dot(a_ref[0].astype(BF16), wa_ref[...]) + _dot(b_ref[0].astype(BF16), wb_ref[...])
    h1 = h_ref[0] + mod[2:3] * y
    h1_ref[0] = h1
    m = h1 * _rms_scale(h1, h1.shape[-1]) * (1.0 + mod[4:5]) + mod[3:4]
    _store_row_tiles(m_ref, 0, m)
    mh, mm, _ = _split3(m)
    hi = _dot_nt(mh, wr_ref[...])
    lg = hi[:, :N_EXPERTS] + hi[:, N_EXPERTS:] + _dot_nt(mm, wr_ref[0:N_EXPERTS, :])
    lg_ref[0] = lg + br_ref[...]


def _mixout(a, bmix, h, modtab, wa, wb, wr3, br, sel, h_off, rows):
    b = h.shape[0]
    d = h.shape[-1]
    ka, kb = a.shape[-1], bmix.shape[-1]
    out3 = lambda n: jax.ShapeDtypeStruct((b, rows, n), F32)
    return pl.pallas_call(
        _mixout_kernel,
        grid=(b, rows // ROW_TILE),
        in_specs=[
            pl.BlockSpec((1, ROW_TILE, ka), lambda i, j: (i, j, 0)),
            pl.BlockSpec((1, ROW_TILE, kb), lambda i, j: (i, j, 0)),
            pl.BlockSpec((1, ROW_TILE, d), lambda i, j: (i, j + h_off, 0)),
            pl.BlockSpec((1, 1, 6, d), lambda i, j: (i, sel(j), 0, 0)),
            pl.BlockSpec((ka, d), lambda i, j: (0, 0)),
            pl.BlockSpec((kb, d), lambda i, j: (0, 0)),
            pl.BlockSpec((2 * N_EXPERTS, d), lambda i, j: (0, 0)),
            pl.BlockSpec((1, N_EXPERTS), lambda i, j: (0, 0)),
        ],
        out_specs=[
            pl.BlockSpec((1, ROW_TILE, d), lambda i, j: (i, j, 0)),
            pl.BlockSpec((ROW_TILE * ROW_SPLIT, LANES), lambda i, j: (i * (rows // ROW_TILE) + j, 0)),
            pl.BlockSpec((1, ROW_TILE, N_EXPERTS), lambda i, j: (i, j, 0)),
        ],
        out_shape=[out3(d), jax.ShapeDtypeStruct((b * rows * ROW_SPLIT, LANES), F32), out3(N_EXPERTS)],
        compiler_params=_cp(("arbitrary", "arbitrary")),
    )(a, bmix, h, modtab, wa, wb, wr3, br)


def _route_kernel(lg_ref, idx_ref, gate_ref, cnt_ref):
    t = lg_ref.shape[0]

    @pl.when(pl.program_id(0) == 0)
    def _():
        cnt_ref[...] = jnp.zeros_like(cnt_ref)

    lg = lg_ref[...]
    lane = lax.broadcasted_iota(jnp.int32, (t, N_EXPERTS), 1).astype(F32)
    hot, vals, idxs = [], [], []
    for _ in range(TOP_K):
        mx = jnp.max(lg, axis=-1, keepdims=True)
        ix = jnp.min(jnp.where(lg == mx, lane, float(N_EXPERTS)), axis=-1, keepdims=True)
        oh = lane == ix
        hot.append(oh)
        vals.append(mx)
        idxs.append(ix.astype(jnp.int32))
        lg = jnp.where(oh, -jnp.inf, lg)
    ex = [jnp.exp(v - vals[0]) for v in vals]
    den = ex[0] + ex[1] + ex[2] + ex[3]
    sel = jnp.where(hot[0] | hot[1] | hot[2] | hot[3], 1.0, 0.0)
    out_lane = lax.broadcasted_iota(jnp.int32, (t, LANES), 1)
    idx_o = jnp.zeros((t, LANES), jnp.int32)
    gate_o = jnp.zeros((t, LANES), F32)
    for k in range(TOP_K):
        idx_o = jnp.where(out_lane == k, idxs[k], idx_o)
        gate_o = jnp.where(out_lane == k, ex[k] / den, gate_o)
    idx_ref[...] = idx_o
    gate_ref[...] = gate_o
    cnt_ref[...] += jnp.sum(sel, axis=0, keepdims=True)


def _route(logits):
    t = logits.shape[0]
    tile = pl.BlockSpec((ROUTE_TILE, LANES), lambda i: (i, 0))
    return pl.pallas_call(
        _route_kernel,
        grid=(t // ROUTE_TILE,),
        in_specs=[pl.BlockSpec((ROUTE_TILE, N_EXPERTS), lambda i: (i, 0))],
        out_specs=[tile, tile, pl.BlockSpec((1, N_EXPERTS), lambda i: (0, 0))],
        out_shape=[jax.ShapeDtypeStruct((t, LANES), jnp.int32),
                   jax.ShapeDtypeStruct((t, LANES), F32),
                   jax.ShapeDtypeStruct((1, N_EXPERTS), F32)],
        compiler_params=_cp(("arbitrary",)),
    )(logits)


def _experts_kernel(be_ref, src0_ref, srcn_ref, dstp_ref, dstc_ref, m_ref, w1_ref, b1_ref, w2_ref,
                    b2_ref, yu_ref, xa, xb, ya, yb, w1b, w2b, gsem, ssem, *, spare):
    g = pl.program_id(0)
    last = pl.num_programs(0) - 1
    tile = xa.shape[0] // ROW_SPLIT
    xbuf = (xa, xb)
    ybuf = (ya, yb)

    def tile_rows(start):
        return pl.ds(pl.multiple_of(start, ROW_SPLIT), ROW_SPLIT)

    def gather(idx_ref, r, s):
        return pltpu.make_async_copy(m_ref.at[tile_rows(idx_ref[r])], xbuf[s].at[tile_rows(r * ROW_SPLIT)],
                                     gsem.at[s])

    def scatter(row, r, s):
        return pltpu.make_async_copy(ybuf[s].at[tile_rows(r * ROW_SPLIT)], yu_ref.at[tile_rows(row)],
                                     ssem.at[s])

    def wait_gather(s):
        pltpu.make_async_copy(m_ref.at[pl.ds(0, tile * ROW_SPLIT)], xbuf[s], gsem.at[s]).wait()

    def wait_scatter(s):
        pltpu.make_async_copy(ybuf[s], yu_ref.at[pl.ds(0, tile * ROW_SPLIT)], ssem.at[s]).wait()

    @pl.when(g == 0)
    def _():
        ya[...] = jnp.zeros_like(ya)
        yb[...] = jnp.zeros_like(yb)

        def first(r, c):
            gather(src0_ref, r, 0).start()
            scatter((spare + 3 * tile + r) * ROW_SPLIT, r, 0).start()
            return c

        lax.fori_loop(0, tile, first, 0)

    changed = (g == 0) | (be_ref[g] != be_ref[jnp.maximum(g - 1, 0)])

    @pl.when(changed)
    def _():
        w1b[...] = w1_ref[0, 0].astype(BF16)
        w2b[...] = w2_ref[0, 0].astype(BF16)

    def block(slot, other):
        wait_gather(slot)
        wait_scatter(slot)
        rows = tile // EXPERT_CHUNKS
        for c in range(EXPERT_CHUNKS):
            for r in range(c * rows, (c + 1) * rows):
                gather(srcn_ref, r, other).start(priority=r % 2)
                scatter(jnp.where(g > 0, dstp_ref[r], (spare + 2 * tile + r) * ROW_SPLIT), r,
                        other).start(priority=(r + 1) % 2)
            x = _load_row_tiles(xbuf[slot], c * rows, rows).astype(BF16)
            hb = _dot(x, w1b[...]) + b1_ref[0, 0]
            gl = jnp.minimum(hb[:, :D_FF], SWIGLU_LIMIT)
            lin = jnp.clip(hb[:, D_FF:], -SWIGLU_LIMIT, SWIGLU_LIMIT)
            a = gl * _sigmoid(SWIGLU_ALPHA * gl) * (lin + 1.0)
            _store_row_tiles(ybuf[slot], c * rows, _dot(a.astype(BF16), w2b[...]) + b2_ref[0, 0])

        @pl.when(g == last)
        def _():
            def final(r, c):
                scatter(dstc_ref[r], r, slot).start()
                return c

            lax.fori_loop(0, tile, final, 0)
            wait_scatter(other)
            wait_scatter(slot)
            wait_gather(other)

    @pl.when(g % 2 == 1)
    def _():
        block(1, 0)

    @pl.when(g % 2 == 0)
    def _():
        block(0, 1)


def _experts(m, blk_e, src_tok, dst_row, layer, w1, b1, w2, b2):
    t, d = m.shape[0] // ROW_SPLIT, D_MODEL
    _, e, _, f2 = w1.shape
    nb = src_tok.shape[0] // MOE_TILE
    spare = TOP_K * t
    idx = lambda fn: pl.BlockSpec((MOE_TILE,), fn, memory_space=pltpu.SMEM)
    grid_spec = pltpu.PrefetchScalarGridSpec(
        num_scalar_prefetch=1,
        grid=(nb,),
        in_specs=[
            idx(lambda i, be: (0,)),
            idx(lambda i, be: (jnp.minimum(i + 1, nb - 1),)),
            idx(lambda i, be: (jnp.maximum(i - 1, 0),)),
            idx(lambda i, be: (i,)),
            pl.BlockSpec(memory_space=pl.ANY),
            pl.BlockSpec((1, 1, d, f2), lambda i, be: (layer, be[i], 0, 0)),
            pl.BlockSpec((1, 1, 1, f2), lambda i, be: (layer, be[i], 0, 0)),
            pl.BlockSpec((1, 1, f2 // 2, d), lambda i, be: (layer, be[i], 0, 0)),
            pl.BlockSpec((1, 1, 1, d), lambda i, be: (layer, be[i], 0, 0)),
        ],
        out_specs=pl.BlockSpec(memory_space=pl.ANY),
        scratch_shapes=[pltpu.VMEM((MOE_TILE * ROW_SPLIT, LANES), F32)] * 4 + [
                        pltpu.VMEM((d, f2), BF16), pltpu.VMEM((f2 // 2, d), BF16),
                        pltpu.SemaphoreType.DMA((2,)), pltpu.SemaphoreType.DMA((2,))],
    )
    return pl.pallas_call(
        functools.partial(_experts_kernel, spare=spare),
        grid_spec=grid_spec,
        out_shape=jax.ShapeDtypeStruct(((spare + SPARE_BLOCKS * MOE_TILE) * ROW_SPLIT, LANES), F32),
        compiler_params=_cp(("arbitrary",)),
    )(blk_e, src_tok, src_tok, dst_row, dst_row, m, w1, b1.reshape(-1, e, 1, f2), w2,
      b2.reshape(-1, e, 1, d))


def _combine_kernel(y0_ref, y1_ref, y2_ref, y3_ref, gate_ref, h_ref, mod_ref, o_ref):
    gate = gate_ref[...]
    rows = h_ref.shape[1]
    f = gate[:, 0:1] * _load_row_tiles(y0_ref, 0, rows)
    for k, y_ref in ((1, y1_ref), (2, y2_ref), (3, y3_ref)):
        f = f + gate[:, k:k + 1] * _load_row_tiles(y_ref, 0, rows)
    o_ref[0] = h_ref[0] + mod_ref[0, 0][5:6] * f


def _combine(yu, gate, h1, modtab, sel, out_rows, out_map):
    b, r, d = h1.shape
    nblk = r // ROW_TILE
    tblk = b * nblk
    ysp = lambda k: pl.BlockSpec((ROW_TILE * ROW_SPLIT, LANES), lambda i, j, k=k: (k * tblk + i * nblk + j, 0))
    return pl.pallas_call(
        _combine_kernel,
        grid=(b, nblk),
        in_specs=[
            ysp(0), ysp(1), ysp(2), ysp(3),
            pl.BlockSpec((ROW_TILE, LANES), lambda i, j: (i * nblk + j, 0)),
            pl.BlockSpec((1, ROW_TILE, d), lambda i, j: (i, j, 0)),
            pl.BlockSpec((1, 1, 6, d), lambda i, j: (i, sel(j), 0, 0)),
        ],
        out_specs=pl.BlockSpec((1, ROW_TILE, d), lambda i, j: (i, out_map(j), 0)),
        out_shape=jax.ShapeDtypeStruct((b, out_rows, d), F32),
        compiler_params=_cp(("arbitrary", "arbitrary")),
    )(yu, yu, yu, yu, gate, h1, modtab)


def _moe(m, logits, h1, modtab, sel, layer, w1, b1, w2, b2, out_rows, out_map):
    b, r, d = h1.shape
    t = b * r
    n_assign = t * TOP_K
    idx, gate, counts = _route(logits.reshape(t, N_EXPERTS))
    counts = counts[0].astype(jnp.int32)
    n_blocks = n_assign // MOE_TILE + N_EXPERTS
    n_slots = n_blocks * MOE_TILE
    padded = (counts + MOE_TILE - 1) // MOE_TILE * MOE_TILE
    pad_end = jnp.cumsum(padded)
    fill_end = jnp.cumsum(padded - counts)
    fill = jnp.arange(n_slots - n_assign, dtype=jnp.int32)
    fill_e = jnp.sum(fill[:, None] >= fill_end[None, :], axis=1).astype(jnp.int32)
    keys = jnp.concatenate([idx[:, :TOP_K].reshape(-1), fill_e])
    order = jnp.argsort(keys, stable=True).astype(jnp.int32)
    real = order < n_assign
    src_tok = jnp.where(real, order // TOP_K, 0)
    spare_row = n_assign + jnp.arange(n_slots, dtype=jnp.int32) % (2 * MOE_TILE)
    dst_row = jnp.where(real, (order % TOP_K) * t + order // TOP_K, spare_row)
    blk_start = jnp.arange(n_blocks, dtype=jnp.int32) * MOE_TILE
    blk_e = jnp.minimum(jnp.sum(blk_start[:, None] >= pad_end[None, :], axis=1), N_EXPERTS - 1).astype(jnp.int32)
    yu = _experts(m, blk_e, src_tok * ROW_SPLIT, dst_row * ROW_SPLIT, layer, w1, b1, w2, b2)
    return _combine(yu, gate, h1, modtab, sel, out_rows, out_map)


def _conv_kernel(a_ref, dw_ref, db_ref, lw_ref, lb_ref, o_ref, hs_ref, *, ll):
    hs_ref[0:CONV_PAD, :] = jnp.zeros((CONV_PAD, CONV_WIDTH), F32)
    hs_ref[CONV_PAD + ll:2 * CONV_PAD + ll, :] = jnp.zeros((CONV_PAD, CONV_WIDTH), F32)
    hs_ref[CONV_PAD:CONV_PAD + ll, :] = a_ref[0, :, :CONV_WIDTH] * _sigmoid(a_ref[0, :, CONV_WIDTH:])
    first = CONV_PAD - CONV_K // 2

    def tile(i, c):
        r0 = pl.multiple_of(i * ROW_TILE, ROW_TILE)
        win = hs_ref[pl.ds(r0, ROW_TILE + 2 * CONV_PAD), :]
        acc = jnp.zeros((ROW_TILE, CONV_WIDTH), F32)
        for k in range(CONV_K):
            acc = acc + dw_ref[k:k + 1, :] * win[first + k:first + k + ROW_TILE]
        acc = acc + db_ref[...]
        mu = jnp.mean(acc, axis=-1, keepdims=True)
        dlt = acc - mu
        var = jnp.mean(dlt * dlt, axis=-1, keepdims=True)
        y = dlt * lax.rsqrt(var + EPS) * lw_ref[...] + lb_ref[...]
        o_ref[0, pl.ds(r0, ROW_TILE), :] = _silu(y)
        return c

    lax.fori_loop(0, ll // ROW_TILE, tile, 0)


def _conv(p, dw_w, dw_b, ln_w, ln_b, ll):
    b = p.shape[0]
    row = lambda v: v.reshape(1, CONV_WIDTH)
    vec = pl.BlockSpec((1, CONV_WIDTH), lambda i: (0, 0))
    return pl.pallas_call(
        functools.partial(_conv_kernel, ll=ll),
        grid=(b,),
        in_specs=[pl.BlockSpec((1, ll, 2 * CONV_WIDTH), lambda i: (i, 0, 0)),
                  pl.BlockSpec((CONV_K, CONV_WIDTH), lambda i: (0, 0)), vec, vec, vec],
        out_specs=pl.BlockSpec((1, ll, CONV_WIDTH), lambda i: (i, 0, 0)),
        out_shape=jax.ShapeDtypeStruct((b, ll, CONV_WIDTH), F32),
        scratch_shapes=[pltpu.VMEM((ll + 2 * CONV_PAD, CONV_WIDTH), F32)],
        compiler_params=_cp(("arbitrary",)),
    )(p, dw_w, row(dw_b), row(ln_w), row(ln_b))


def _mla_proj_kernel(cq_ref, ckv_ref, kr_ref, cs_ref, qa_ref, kva_ref, wq_ref, wqs_ref, wk_ref,
                     wv_ref, ek_ref, eks_ref, gq_ref, gqs_ref, gk_ref, gks_ref,
                     q_ref, k_ref, v_ref):
    cos = cs_ref[0]
    sin = cs_ref[1]
    cq = cq_ref[0]
    cqn = (cq * _rms_scale(cq, MLA_Q_RANK) * qa_ref[...]).astype(BF16)
    qx = _dot(cqn, wq_ref[...])
    qs = _dot(cqn, wqs_ref[...])
    ckv = ckv_ref[0]
    ckvn = (ckv * _rms_scale(ckv, MLA_KV_RANK) * kva_ref[...]).astype(BF16)
    kr = kr_ref[0]
    rh = kr.astype(BF16)
    rl = (kr - rh.astype(F32)).astype(BF16)
    kx = _dot(ckvn, wk_ref[...]) + _dot(rh, ek_ref[...]) + _dot(rl, ek_ref[...])
    ks = _dot(rh, eks_ref[...]) + _dot(rl, eks_ref[...])
    v_ref[0] = _dot(ckvn, wv_ref[...]).astype(BF16)
    scale = MLA_QK ** -0.5
    for h in range(MLA_HEADS):
        sl = slice(h * HEAD_PAD, (h + 1) * HEAD_PAD)
        qh = qx[:, sl]
        q_ref[0, :, sl] = (_rms_scale(qh, MLA_QK) * scale
                           * (qh * gq_ref[...] * cos + qs[:, sl] * gqs_ref[...] * sin)).astype(BF16)
        kh = kx[:, sl]
        k_ref[0, :, sl] = (_rms_scale(kh, MLA_QK)
                           * (kh * gk_ref[...] * cos + ks[:, sl] * gks_ref[...] * sin)).astype(BF16)


def _mla_proj(p, cs, qa, kva, wq, wqs, wk, wv, ek, eks, gq, gqs, gk, gks):
    b, s, _ = p.shape
    hw = MLA_HEADS * HEAD_PAD
    full = lambda a: pl.BlockSpec(a.shape, lambda i, j: (0,) * a.ndim)
    out = pl.BlockSpec((1, ROW_TILE, hw), lambda i, j: (i, j, 0))
    consts = (qa, kva, wq, wqs, wk, wv, ek, eks, gq, gqs, gk, gks)
    return pl.pallas_call(
        _mla_proj_kernel,
        grid=(b, s // ROW_TILE),
        in_specs=[pl.BlockSpec((1, ROW_TILE, 512), lambda i, j: (i, j, OD_CQ // 512)),
                  pl.BlockSpec((1, ROW_TILE, MLA_KV_RANK), lambda i, j: (i, j, OD_CKV // MLA_KV_RANK)),
                  pl.BlockSpec((1, ROW_TILE, LANES), lambda i, j: (i, j, OD_KR // LANES)),
                  pl.BlockSpec((2, ROW_TILE, HEAD_PAD), lambda i, j: (0, j, 0))]
                 + [full(a) for a in consts],
        out_specs=[out, out, out],
        out_shape=[jax.ShapeDtypeStruct((b, s, hw), BF16)] * 3,
        compiler_params=_cp(("arbitrary", "arbitrary")),
    )(p, p, p, cs, *consts)


def _attn_kernel(q_ref, k_ref, v_ref, o_ref):
    def scores(h):
        sl = slice(h * HEAD_PAD, (h + 1) * HEAD_PAD)
        return _dot_nt(q_ref[0, :, sl], k_ref[0, :, sl])

    s = scores(0)
    for h in range(MLA_HEADS):
        s_next = scores(h + 1) if h + 1 < MLA_HEADS else None
        sl = slice(h * HEAD_PAD, (h + 1) * HEAD_PAD)
        p = jnp.exp(s - jnp.max(s, axis=-1, keepdims=True))
        den = jnp.sum(p, axis=-1, keepdims=True)
        o_ref[0, :, sl] = (_dot(p.astype(BF16), v_ref[0, :, sl]) / den).astype(BF16)
        s = s_next


def _attention(q, k, v, ll):
    b, s, hw = k.shape
    kv = pl.BlockSpec((1, s, hw), lambda i, j: (i, 0, 0))
    qo = pl.BlockSpec((1, Q_TILE, hw), lambda i, j: (i, j, 0))
    return pl.pallas_call(
        _attn_kernel,
        grid=(b, ll // Q_TILE),
        in_specs=[qo, kv, kv],
        out_specs=qo,
        out_shape=jax.ShapeDtypeStruct((b, ll, hw), BF16),
        compiler_params=_cp(("arbitrary", "arbitrary")),
    )(q, k, v)


def _place(parts, lead):
    out = jnp.zeros(lead + (MLA_HEADS, HEAD_PAD), F32)
    for arr, off in parts:
        out = out.at[..., off:off + arr.shape[-1]].set(arr)
    return out.reshape(lead + (MLA_HEADS * HEAD_PAD,))


def _swap_pairs(a):
    g = a.reshape(a.shape[:-1] + (2, 2, ROPE_AXIS_HALF))
    return jnp.flip(g, axis=-2).reshape(a.shape)


def _rope_tables(rows_total, ll):
    grid_rows = ll // GRID_W
    t_row = jnp.repeat(jnp.arange(grid_rows, dtype=F32), GRID_W)
    t_col = jnp.tile(jnp.arange(GRID_W, dtype=F32), grid_rows)
    inv = 1.0 / (ROPE_BASE ** (jnp.arange(ROPE_AXIS_HALF, dtype=F32) / ROPE_AXIS_HALF))
    ang_r = t_row[:, None] * inv
    ang_c = t_col[:, None] * inv
    cos32 = jnp.concatenate([jnp.cos(ang_r), jnp.cos(ang_r), jnp.cos(ang_c), jnp.cos(ang_c)], axis=-1)
    sin32 = jnp.concatenate([-jnp.sin(ang_r), jnp.sin(ang_r), -jnp.sin(ang_c), jnp.sin(ang_c)], axis=-1)
    extra = rows_total - ll
    cos32 = jnp.concatenate([cos32, jnp.ones((extra, MLA_ROPE), F32)], axis=0)
    sin32 = jnp.concatenate([sin32, jnp.zeros((extra, MLA_ROPE), F32)], axis=0)
    cos_t = jnp.concatenate([jnp.ones((rows_total, MLA_NOPE), F32), cos32,
                             jnp.zeros((rows_total, HEAD_PAD - MLA_QK), F32)], axis=-1)
    sin_t = jnp.concatenate([jnp.zeros((rows_total, MLA_NOPE), F32), sin32,
                             jnp.zeros((rows_total, HEAD_PAD - MLA_QK), F32)], axis=-1)
    return jnp.stack([cos_t, sin_t])


def kernel(x, c, ctx, c_ctx, ada_w, ada_b, ev_w_in, hgrn_lb_logits, hgrn_norm_w, pool_w, pool_scale,
           ev_w_out, od_w_in, conv_dw_w, conv_dw_b, conv_ln_w, conv_ln_b, mla_q_a_norm, mla_w_uq,
           mla_kv_a_norm, mla_w_ukv, mla_q_norm, mla_k_norm, od_w_out, moe_router_w, moe_router_b,
           moe_w1, moe_b1, moe_w2, moe_b2):
    b, ll, d = x.shape
    lc = ctx.shape[1]
    s = ll + lc
    nc, nl = lc // ROW_TILE, ll // ROW_TILE
    assert ada_w.shape[0] == 2 and d == D_MODEL
    assert ll % Q_TILE == 0 and lc % ROW_TILE == 0 and (b * s) % ROUTE_TILE == 0 and (b * ll) % ROUTE_TILE == 0

    cond_rows = (b + 1 + 7) // 8 * 8
    cond = jnp.zeros((cond_rows, d), F32).at[:b].set(c).at[b].set(c_ctx)
    mod = _ada_table(cond, ada_w, ada_b)
    modtab = []
    for layer in range(2):
        lat = mod[layer, :b].reshape(b, 1, 6, d)
        con = jnp.broadcast_to(mod[layer, b].reshape(1, 1, 6, d), (b, 1, 6, d))
        modtab.append(jnp.concatenate([con, lat], axis=1))
    lb_all = jnp.cumsum(jax.nn.softmax(hgrn_lb_logits.astype(F32), axis=0), axis=0)

    def router_pieces(w):
        hi, mid, _ = _split3(w.T)
        return jnp.concatenate([hi, mid], axis=0)

    sel0 = lambda j: jnp.where(j < nc, 0, 1)
    h = jnp.concatenate([ctx, x], axis=1)
    p0 = _modproj(h, modtab[0], ev_w_in[0].astype(BF16), sel0)
    hg = _hgrn(p0, lb_all[0], hgrn_norm_w[0], nc, nl)
    yp = _pool(p0, pool_w[0], pool_scale[0], lc, ll)
    wo = ev_w_out[0].astype(BF16)
    h1, m, lg = _mixout(hg, yp, h, modtab[0], wo[:HG_W], wo[HG_W:], router_pieces(moe_router_w[0]),
                        moe_router_b[0].reshape(1, N_EXPERTS), sel0, 0, s)
    to_l1 = lambda j: jnp.where(j < nc, j + nl, j - nc)
    h = _moe(m, lg, h1, modtab[0], sel0, 0, moe_w1, moe_b1, moe_w2, moe_b2, s, to_l1)

    sel1 = lambda j: jnp.where(j < nl, 1, 0)
    w_in = od_w_in[0]
    kr_cols = w_in[:, 1664:1696]
    w1p = jnp.zeros((d, OD_IN_PAD), F32)
    w1p = w1p.at[:, :1024].set(w_in[:, :1024]).at[:, OD_CQ:OD_CQ + MLA_Q_RANK].set(w_in[:, 1024:1408])
    w1p = w1p.at[:, OD_CKV:OD_CKV + MLA_KV_RANK].set(w_in[:, 1408:1664])
    w1p = w1p.at[:, OD_KR:OD_KR + MLA_ROPE].set(kr_cols)
    w1p = w1p.at[:, OD_KR + MLA_ROPE:OD_KR + 2 * MLA_ROPE].set(_swap_pairs(kr_cols))
    p1 = _modproj(h, modtab[1], w1p.astype(BF16), sel1)
    hc = _conv(p1, conv_dw_w[0], conv_dw_b[0], conv_ln_w[0], conv_ln_b[0], ll)

    wuq = mla_w_uq[0].reshape(MLA_Q_RANK, MLA_HEADS, MLA_QK)
    wuq_rope = wuq[..., MLA_NOPE:]
    pad_rows = lambda w: jnp.zeros((512, w.shape[1]), F32).at[:MLA_Q_RANK].set(w)
    wq = pad_rows(_place([(wuq, 0)], (MLA_Q_RANK,))).astype(BF16)
    wqs = pad_rows(_place([(_swap_pairs(wuq_rope), MLA_NOPE)], (MLA_Q_RANK,))).astype(BF16)
    wukv = mla_w_ukv[0].reshape(MLA_KV_RANK, MLA_HEADS, MLA_NOPE + MLA_V)
    wk = _place([(wukv[..., :MLA_NOPE], 0)], (MLA_KV_RANK,)).astype(BF16)
    wv = _place([(wukv[..., MLA_NOPE:], 0)], (MLA_KV_RANK,)).astype(BF16)
    eye = jnp.broadcast_to(jnp.eye(MLA_ROPE, dtype=F32)[:, None, :], (MLA_ROPE, MLA_HEADS, MLA_ROPE))
    zero_rows = jnp.zeros((LANES - 2 * MLA_ROPE, MLA_HEADS * HEAD_PAD), F32)
    place_kr = _place([(eye, MLA_NOPE)], (MLA_ROPE,))
    zero_kr = jnp.zeros_like(place_kr)
    ek = jnp.concatenate([place_kr, zero_kr, zero_rows], axis=0).astype(BF16)
    eks = jnp.concatenate([zero_kr, place_kr, zero_rows], axis=0).astype(BF16)
    lane_gain = lambda g: jnp.concatenate([g, jnp.zeros((HEAD_PAD - MLA_QK,), F32)]).reshape(1, HEAD_PAD)
    swapped_gain = lambda g: jnp.concatenate(
        [jnp.zeros((MLA_NOPE,), F32), _swap_pairs(g[MLA_NOPE:]), jnp.zeros((HEAD_PAD - MLA_QK,), F32)]
    ).reshape(1, HEAD_PAD)
    qa = jnp.concatenate([mla_q_a_norm[0], jnp.zeros((512 - MLA_Q_RANK,), F32)]).reshape(1, 512)
    q, k, v = _mla_proj(p1, _rope_tables(s, ll), qa, mla_kv_a_norm[0].reshape(1, MLA_KV_RANK),
                        wq, wqs, wk, wv, ek, eks,
                        lane_gain(mla_q_norm[0]), swapped_gain(mla_q_norm[0]),
                        lane_gain(mla_k_norm[0]), swapped_gain(mla_k_norm[0]))
    attn = _attention(q, k, v, ll)

    wo1 = od_w_out[0]
    wo_attn = wo1[CONV_WIDTH:].reshape(MLA_HEADS, MLA_V, d)
    wo_attn = jnp.zeros((MLA_HEADS, HEAD_PAD, d), F32).at[:, :MLA_V].set(wo_attn).reshape(-1, d)
    h1, m, lg = _mixout(hc, attn, h, modtab[1], wo1[:CONV_WIDTH].astype(BF16), wo_attn.astype(BF16),
                        router_pieces(moe_router_w[1]), moe_router_b[1].reshape(1, N_EXPERTS),
                        lambda j: 1, 0, ll)
    return _moe(m, lg, h1, modtab[1], lambda j: 1, 1, moe_w1, moe_b1, moe_w2, moe_b2, ll, lambda j: j)
```

```python
import functools

import jax
import jax.numpy as jnp
from jax import lax
from jax.experimental import pallas as pl
from jax.experimental.pallas import tpu as pltpu

F32 = jnp.float32
BF16 = jnp.bfloat16

D_MODEL = 1024
GRID_W = 64
EPS = 1e-6
HG_HEADS = 4
HG_DIM = 128
HG_W = HG_HEADS * HG_DIM
HG_SUB = 32
POOL_WINDOWS = (2, 4, 8, 16)
POOL_GROUP = 128
POOL_PAD = 16
CONV_WIDTH = 512
CONV_K = 31
CONV_PAD = 16
MLA_HEADS = 8
MLA_Q_RANK = 384
MLA_KV_RANK = 256
MLA_NOPE = 64
MLA_ROPE = 32
MLA_V = 64
MLA_QK = MLA_NOPE + MLA_ROPE
HEAD_PAD = 128
ROPE_AXIS_HALF = MLA_ROPE // 4
ROPE_BASE = 10000.0
N_EXPERTS = 32
TOP_K = 4
D_FF = 1024
SWIGLU_LIMIT = 7.0
SWIGLU_ALPHA = 1.702

ROW_TILE = 256
ROUTE_TILE = 512
MOE_TILE = 512
EXPERT_CHUNKS = 4
SPARE_BLOCKS = 4
Q_TILE = 512
LANES = 128
ROW_SPLIT = D_MODEL // LANES
VMEM_LIMIT = 56 * 1024 * 1024

OD_CQ = 1024
OD_CKV = 1536
OD_KR = 1792
OD_IN_PAD = 1920


def _cp(sem, vmem=VMEM_LIMIT):
    return pltpu.CompilerParams(dimension_semantics=sem, vmem_limit_bytes=vmem)


def _dot(a, b):
    return jnp.dot(a, b, preferred_element_type=F32)


def _dot_nt(a, b):
    return lax.dot_general(a, b, (((1,), (1,)), ((), ())), preferred_element_type=F32)


def _dot_tn(a, b):
    return lax.dot_general(a, b, (((0,), (0,)), ((), ())), preferred_element_type=F32)


def _split3(x):
    hi = x.astype(BF16)
    r1 = x - hi.astype(F32)
    mid = r1.astype(BF16)
    lo = (r1 - mid.astype(F32)).astype(BF16)
    return hi, mid, lo


def _sigmoid(x):
    return 1.0 / (1.0 + jnp.exp(-x))


def _silu(x):
    return x * _sigmoid(x)


def _load_row_tiles(ref, row0, rows):
    return jnp.concatenate(
        [ref[pl.ds(row0 * ROW_SPLIT + s, rows, stride=ROW_SPLIT), :] for s in range(ROW_SPLIT)], axis=1)


def _store_row_tiles(ref, row0, x):
    for s in range(ROW_SPLIT):
        ref[pl.ds(row0 * ROW_SPLIT + s, x.shape[0], stride=ROW_SPLIT), :] = x[:, s * LANES:(s + 1) * LANES]


def _rms_scale(x, width):
    return lax.rsqrt(jnp.sum(x * x, axis=-1, keepdims=True) * (1.0 / width) + EPS)


def _ada_kernel(c_ref, w_ref, b_ref, o_ref):
    s = _silu(c_ref[...])
    sh, sm, sl = _split3(s)
    wh, wm, wl = _split3(w_ref[0])
    acc = _dot(sh, wh) + _dot(sh, wm) + _dot(sm, wh) + _dot(sh, wl) + _dot(sl, wh) + _dot(sm, wm)
    o_ref[0] = acc + b_ref[0]


def _ada_table(cond, ada_w, ada_b):
    depth, d, n = ada_w.shape
    rows = cond.shape[0]
    nb = n // d
    return pl.pallas_call(
        _ada_kernel,
        grid=(depth, nb),
        in_specs=[
            pl.BlockSpec((rows, d), lambda l, j: (0, 0)),
            pl.BlockSpec((1, d, d), lambda l, j: (l, 0, j)),
            pl.BlockSpec((1, 1, d), lambda l, j: (l, 0, j)),
        ],
        out_specs=pl.BlockSpec((1, rows, d), lambda l, j: (l, 0, j)),
        out_shape=jax.ShapeDtypeStruct((depth, rows, n), F32),
        compiler_params=_cp(("arbitrary", "arbitrary")),
    )(cond, ada_w, ada_b.reshape(depth, 1, n))


def _modproj_kernel(h_ref, mod_ref, w_ref, o_ref):
    x = h_ref[0]
    mod = mod_ref[0, 0]
    u = x * _rms_scale(x, x.shape[-1]) * (1.0 + mod[1:2]) + mod[0:1]
    o_ref[0] = _dot(u.astype(BF16), w_ref[...])


def _modproj(h, modtab, w, sel):
    b, s, d = h.shape
    n = w.shape[1]
    return pl.pallas_call(
        _modproj_kernel,
        grid=(b, s // ROW_TILE),
        in_specs=[
            pl.BlockSpec((1, ROW_TILE, d), lambda i, j: (i, j, 0)),
            pl.BlockSpec((1, 1, 6, d), lambda i, j: (i, sel(j), 0, 0)),
            pl.BlockSpec((d, n), lambda i, j: (0, 0)),
        ],
        out_specs=pl.BlockSpec((1, ROW_TILE, n), lambda i, j: (i, j, 0)),
        out_shape=jax.ShapeDtypeStruct((b, s, n), F32),
        compiler_params=_cp(("arbitrary", "arbitrary")),
    )(h, modtab, w)


def _hgrn_block(q, z, v, lb, st, reverse):
    rows = q.shape[0]
    nsub = rows // HG_SUB
    f = lb + (1.0 - lb) * _sigmoid(z)
    kin = 1.0 - f
    logf = jnp.log(f)
    ri = lax.broadcasted_iota(jnp.int32, (rows, rows), 0)
    ci = lax.broadcasted_iota(jnp.int32, (rows, rows), 1)
    same = (ri // HG_SUB) == (ci // HG_SUB)
    causal = same & ((ci >= ri) if reverse else (ci <= ri))
    tri = jnp.where(causal, 1.0, 0.0).astype(BF16)
    lh, lm, ll = _split3(logf)
    bcum = _dot(tri, lh) + _dot(tri, lm) + _dot(tri, ll)
    b3 = bcum.reshape(nsub, HG_SUB, HG_DIM)
    tot_row = 0 if reverse else HG_SUB - 1
    mid_row = HG_SUB // 2 if reverse else HG_SUB // 2 - 1
    tot = b3[:, tot_row:tot_row + 1, :]
    mid = b3[:, mid_row:mid_row + 1, :]
    q3 = q.reshape(nsub, HG_SUB, HG_DIM)
    k3 = kin.reshape(nsub, HG_SUB, HG_DIM)
    q_dec = (q3 * jnp.exp(b3)).astype(BF16)
    k_end = (k3 * jnp.exp(tot - b3)).astype(BF16)
    q_mid = (q3 * jnp.exp(b3 - mid)).reshape(rows, HG_DIM).astype(BF16)
    k_mid = (k3 * jnp.exp(mid - b3)).reshape(rows, HG_DIM).astype(BF16)
    vb = v.astype(BF16)
    att = jnp.where(causal, _dot_nt(q_mid, k_mid), 0.0).astype(BF16)
    o_intra = _dot(att, vb)
    v3 = vb.reshape(nsub, HG_SUB, HG_DIM)
    dec = jnp.exp(tot)
    upd = [_dot_tn(v3[j], k_end[j]) for j in range(nsub)]
    order = range(nsub - 1, -1, -1) if reverse else range(nsub)
    o_inter = [None] * nsub
    for j in order:
        o_inter[j] = _dot_nt(q_dec[j], st.astype(BF16))
        st = st * dec[j] + upd[j]
    o = o_intra + jnp.concatenate(o_inter, axis=0)
    return o, st


def _hgrn_kernel(q_ref, zf_ref, zb_ref, v_ref, og_ref, lb_ref, nw_ref, o_ref, acc_ref, *, nc, nl):
    blk = ROW_TILE
    nblk = nc + nl
    acc_ref[...] = jnp.zeros_like(acc_ref)
    lb_f = lb_ref[0:1, :]
    lb_b = lb_ref[1:2, :]

    def body(i, carry):
        st_f, st_b = carry
        rf = pl.multiple_of(i * blk, blk)
        ib = jnp.where(i < nc, nc - 1 - i, 2 * nc + nl - 1 - i)
        rb = pl.multiple_of(ib * blk, blk)
        qf = _silu(q_ref[0, pl.ds(rf, blk), :])
        o_f, st_f = _hgrn_block(qf, zf_ref[0, pl.ds(rf, blk), :], v_ref[0, pl.ds(rf, blk), :],
                                lb_f, st_f, False)
        acc_ref[pl.ds(rf, blk), :] += o_f
        qb = _silu(q_ref[0, pl.ds(rb, blk), :])
        o_b, st_b = _hgrn_block(qb, zb_ref[0, pl.ds(rb, blk), :], v_ref[0, pl.ds(rb, blk), :],
                                lb_b, st_b, True)
        acc_ref[pl.ds(rb, blk), :] += o_b
        return st_f, st_b

    zero = jnp.zeros((HG_DIM, HG_DIM), F32)
    lax.fori_loop(0, nblk, body, (zero, zero))
    o = acc_ref[...]
    og = og_ref[0]
    o_ref[0] = o * _rms_scale(o, HG_DIM) * nw_ref[...] * _silu(og)


def _hgrn(p, lb, norm_w, nc, nl):
    b, s, _ = p.shape
    sec = lambda k: pl.BlockSpec((1, s, HG_DIM), lambda i, h, k=k: (i, 0, HG_HEADS * k + h))
    return pl.pallas_call(
        functools.partial(_hgrn_kernel, nc=nc, nl=nl),
        grid=(b, HG_HEADS),
        in_specs=[sec(0), sec(1), sec(2), sec(3), sec(4),
                  pl.BlockSpec((2, HG_DIM), lambda i, h: (0, h)),
                  pl.BlockSpec((1, HG_DIM), lambda i, h: (0, 0))],
        out_specs=pl.BlockSpec((1, s, HG_DIM), lambda i, h: (i, 0, h)),
        out_shape=jax.ShapeDtypeStruct((b, s, HG_W), F32),
        scratch_shapes=[pltpu.VMEM((s, HG_DIM), F32)],
        compiler_params=_cp(("arbitrary", "arbitrary")),
    )(p, p, p, p, p, lb, norm_w.reshape(1, HG_DIM))


def _pool_kernel(x_ref, w_ref, sc_ref, o_ref, *, lc, ll):
    z = jnp.zeros((POOL_PAD, POOL_GROUP), F32)
    n = lc + ll + 3 * POOL_PAD
    pos_c = lax.broadcasted_iota(jnp.int32, (lc, 1), 0)
    pos_l = lax.broadcasted_iota(jnp.int32, (ll, 1), 0)
    for g, w in enumerate(POOL_WINDOWS):
        x = x_ref[0, :, g * POOL_GROUP:(g + 1) * POOL_GROUP]
        xp = jnp.concatenate([z, x[:lc], z, x[lc:], z], axis=0)
        acc = xp + pltpu.roll(xp, 1, 0)
        half = 1
        while 2 * half < w:
            acc = pltpu.roll(acc, half, 0) + pltpu.roll(acc, n - half, 0)
            half *= 2
        win = jnp.concatenate([acc[POOL_PAD:POOL_PAD + lc],
                               acc[2 * POOL_PAD + lc:2 * POOL_PAD + lc + ll]], axis=0)

        def count(pos, length):
            lo = jnp.maximum(pos - w // 2, 0)
            hi = jnp.minimum(pos + w - w // 2, length)
            return (hi - lo).astype(F32)

        cnt = jnp.concatenate([count(pos_c, lc), count(pos_l, ll)], axis=0)
        d = win / cnt - x
        y = _dot(d.astype(BF16), w_ref[g])
        o_ref[0, :, g * POOL_GROUP:(g + 1) * POOL_GROUP] = y * sc_ref[:, g * POOL_GROUP:(g + 1) * POOL_GROUP]


def _pool(p, pool_w, pool_scale, lc, ll):
    b, s, n = p.shape
    width = len(POOL_WINDOWS) * POOL_GROUP
    return pl.pallas_call(
        functools.partial(_pool_kernel, lc=lc, ll=ll),
        grid=(b,),
        in_specs=[pl.BlockSpec((1, s, width), lambda i: (i, 0, n // width - 1)),
                  pl.BlockSpec(pool_w.shape, lambda i: (0, 0, 0)),
                  pl.BlockSpec((1, width), lambda i: (0, 0))],
        out_specs=pl.BlockSpec((1, s, width), lambda i: (i, 0, 0)),
        out_shape=jax.ShapeDtypeStruct((b, s, width), F32),
        compiler_params=_cp(("arbitrary",)),
    )(p, pool_w.astype(BF16), pool_scale.reshape(1, width))


def _mixout_kernel(a_ref, b_ref, h_ref, mod_ref, wa_ref, wb_ref, wr_ref, br_ref,
                   h1_ref, m_ref, lg_ref):
    mod = mod_ref[0, 0]
    y = _dot(a_ref[0].astype(BF16), wa_ref[...]) + _dot(b_ref[0].astype(BF16), wb_ref[...])
    h1 = h_ref[0] + mod[2:3] * y
    h1_ref[0] = h1
    m = h1 * _rms_scale(h1, h1.shape[-1]) * (1.0 + mod[4:5]) + mod[3:4]
    _store_row_tiles(m_ref, 0, m)
    mh, mm, _ = _split3(m)
    hi = _dot_nt(mh, wr_ref[...])
    lg = hi[:, :N_EXPERTS] + hi[:, N_EXPERTS:] + _dot_nt(mm, wr_ref[0:N_EXPERTS, :])
    lg_ref[0] = lg + br_ref[...]


def _mixout(a, bmix, h, modtab, wa, wb, wr3, br, sel, h_off, rows):
    b = h.shape[0]
    d = h.shape[-1]
    ka, kb = a.shape[-1], bmix.shape[-1]
    out3 = lambda n: jax.ShapeDtypeStruct((b, rows, n), F32)
    return pl.pallas_call(
        _mixout_kernel,
        grid=(b, rows // ROW_TILE),
        in_specs=[
            pl.BlockSpec((1, ROW_TILE, ka), lambda i, j: (i, j, 0)),
            pl.BlockSpec((1, ROW_TILE, kb), lambda i, j: (i, j, 0)),
            pl.BlockSpec((1, ROW_TILE, d), lambda i, j: (i, j + h_off, 0)),
            pl.BlockSpec((1, 1, 6, d), lambda i, j: (i, sel(j), 0, 0)),
            pl.BlockSpec((ka, d), lambda i, j: (0, 0)),
            pl.BlockSpec((kb, d), lambda i, j: (0, 0)),
            pl.BlockSpec((2 * N_EXPERTS, d), lambda i, j: (0, 0)),
            pl.BlockSpec((1, N_EXPERTS), lambda i, j: (0, 0)),
        ],
        out_specs=[
            pl.BlockSpec((1, ROW_TILE, d), lambda i, j: (i, j, 0)),
            pl.BlockSpec((ROW_TILE * ROW_SPLIT, LANES), lambda i, j: (i * (rows // ROW_TILE) + j, 0)),
            pl.BlockSpec((1, ROW_TILE, N_EXPERTS), lambda i, j: (i, j, 0)),
        ],
        out_shape=[out3(d), jax.ShapeDtypeStruct((b * rows * ROW_SPLIT, LANES), F32), out3(N_EXPERTS)],
        compiler_params=_cp(("arbitrary", "arbitrary")),
    )(a, bmix, h, modtab, wa, wb, wr3, br)


def _route_kernel(lg_ref, idx_ref, gate_ref, cnt_ref):
    t = lg_ref.shape[0]

    @pl.when(pl.program_id(0) == 0)
    def _():
        cnt_ref[...] = jnp.zeros_like(cnt_ref)

    lg = lg_ref[...]
    lane = lax.broadcasted_iota(jnp.int32, (t, N_EXPERTS), 1).astype(F32)
    hot, vals, idxs = [], [], []
    for _ in range(TOP_K):
        mx = jnp.max(lg, axis=-1, keepdims=True)
        ix = jnp.min(jnp.where(lg == mx, lane, float(N_EXPERTS)), axis=-1, keepdims=True)
        oh = lane == ix
        hot.append(oh)
        vals.append(mx)
        idxs.append(ix.astype(jnp.int32))
        lg = jnp.where(oh, -jnp.inf, lg)
    ex = [jnp.exp(v - vals[0]) for v in vals]
    den = ex[0] + ex[1] + ex[2] + ex[3]
    sel = jnp.where(hot[0] | hot[1] | hot[2] | hot[3], 1.0, 0.0)
    out_lane = lax.broadcasted_iota(jnp.int32, (t, LANES), 1)
    idx_o = jnp.zeros((t, LANES), jnp.int32)
    gate_o = jnp.zeros((t, LANES), F32)
    for k in range(TOP_K):
        idx_o = jnp.where(out_lane == k, idxs[k], idx_o)
        gate_o = jnp.where(out_lane == k, ex[k] / den, gate_o)
    idx_ref[...] = idx_o
    gate_ref[...] = gate_o
    cnt_ref[...] += jnp.sum(sel, axis=0, keepdims=True)


def _route(logits):
    t = logits.shape[0]
    tile = pl.BlockSpec((ROUTE_TILE, LANES), lambda i: (i, 0))
    return pl.pallas_call(
        _route_kernel,
        grid=(t // ROUTE_TILE,),
        in_specs=[pl.BlockSpec((ROUTE_TILE, N_EXPERTS), lambda i: (i, 0))],
        out_specs=[tile, tile, pl.BlockSpec((1, N_EXPERTS), lambda i: (0, 0))],
        out_shape=[jax.ShapeDtypeStruct((t, LANES), jnp.int32),
                   jax.ShapeDtypeStruct((t, LANES), F32),
                   jax.ShapeDtypeStruct((1, N_EXPERTS), F32)],
        compiler_params=_cp(("arbitrary",)),
    )(logits)


def _experts_kernel(be_ref, src0_ref, srcn_ref, dstp_ref, dstc_ref, m_ref, w1_ref, b1_ref, w2_ref,
                    b2_ref, yu_ref, xa, xb, ya, yb, w1b, w2b, gsem, ssem, *, spare):
    g = pl.program_id(0)
    last = pl.num_programs(0) - 1
    tile = xa.shape[0] // ROW_SPLIT
    xbuf = (xa, xb)
    ybuf = (ya, yb)

    def tile_rows(start):
        return pl.ds(pl.multiple_of(start, ROW_SPLIT), ROW_SPLIT)

    rows = tile // EXPERT_CHUNKS

    def gather(idx_ref, r, s):
        return pltpu.make_async_copy(m_ref.at[tile_rows(idx_ref[r])], xbuf[s].at[tile_rows(r * ROW_SPLIT)],
                                     gsem.at[s, r // rows])

    def scatter(row, r, s):
        return pltpu.make_async_copy(ybuf[s].at[tile_rows(r * ROW_SPLIT)], yu_ref.at[tile_rows(row)],
                                     ssem.at[s, r // rows])

    def chunk(c):
        return pl.ds(c * rows * ROW_SPLIT, rows * ROW_SPLIT)

    def wait_gather(s, c):
        pltpu.make_async_copy(m_ref.at[chunk(0)], xbuf[s].at[chunk(c)], gsem.at[s, c]).wait()

    def wait_scatter(s, c):
        pltpu.make_async_copy(ybuf[s].at[chunk(c)], yu_ref.at[chunk(0)], ssem.at[s, c]).wait()

    @pl.when(g == 0)
    def _():
        ya[...] = jnp.zeros_like(ya)
        yb[...] = jnp.zeros_like(yb)

        def first(r, c):
            gather(src0_ref, r, 0).start()
            scatter((spare + 3 * tile + r) * ROW_SPLIT, r, 0).start()
            return c

        lax.fori_loop(0, tile, first, 0)

    changed = (g == 0) | (be_ref[g] != be_ref[jnp.maximum(g - 1, 0)])

    @pl.when(changed)
    def _():
        w1b[...] = w1_ref[0, 0].astype(BF16)
        w2b[...] = w2_ref[0, 0].astype(BF16)

    def block(slot, other):
        for c in range(EXPERT_CHUNKS):
            wait_gather(slot, c)
            wait_scatter(slot, c)
            for r in range(c * rows, (c + 1) * rows):
                gather(srcn_ref, r, other).start(priority=r % 2)
                scatter(jnp.where(g > 0, dstp_ref[r], (spare + 2 * tile + r) * ROW_SPLIT), r,
                        other).start(priority=(r + 1) % 2)
            x = _load_row_tiles(xbuf[slot], c * rows, rows).astype(BF16)
            hb = _dot(x, w1b[...]) + b1_ref[0, 0]
            gl = jnp.minimum(hb[:, :D_FF], SWIGLU_LIMIT)
            lin = jnp.clip(hb[:, D_FF:], -SWIGLU_LIMIT, SWIGLU_LIMIT)
            a = gl * _sigmoid(SWIGLU_ALPHA * gl) * (lin + 1.0)
            _store_row_tiles(ybuf[slot], c * rows, _dot(a.astype(BF16), w2b[...]) + b2_ref[0, 0])

        @pl.when(g == last)
        def _():
            def final(r, c):
                scatter(dstc_ref[r], r, slot).start()
                return c

            lax.fori_loop(0, tile, final, 0)
            for c in range(EXPERT_CHUNKS):
                wait_scatter(other, c)
                wait_scatter(slot, c)
                wait_gather(other, c)

    @pl.when(g % 2 == 1)
    def _():
        block(1, 0)

    @pl.when(g % 2 == 0)
    def _():
        block(0, 1)


def _experts(m, blk_e, src_tok, dst_row, layer, w1, b1, w2, b2):
    t, d = m.shape[0] // ROW_SPLIT, D_MODEL
    _, e, _, f2 = w1.shape
    nb = src_tok.shape[0] // MOE_TILE
    spare = TOP_K * t
    idx = lambda fn: pl.BlockSpec((MOE_TILE,), fn, memory_space=pltpu.SMEM)
    grid_spec = pltpu.PrefetchScalarGridSpec(
        num_scalar_prefetch=1,
        grid=(nb,),
        in_specs=[
            idx(lambda i, be: (0,)),
            idx(lambda i, be: (jnp.minimum(i + 1, nb - 1),)),
            idx(lambda i, be: (jnp.maximum(i - 1, 0),)),
            idx(lambda i, be: (i,)),
            pl.BlockSpec(memory_space=pl.ANY),
            pl.BlockSpec((1, 1, d, f2), lambda i, be: (layer, be[i], 0, 0)),
            pl.BlockSpec((1, 1, 1, f2), lambda i, be: (layer, be[i], 0, 0)),
            pl.BlockSpec((1, 1, f2 // 2, d), lambda i, be: (layer, be[i], 0, 0)),
            pl.BlockSpec((1, 1, 1, d), lambda i, be: (layer, be[i], 0, 0)),
        ],
        out_specs=pl.BlockSpec(memory_space=pl.ANY),
        scratch_shapes=[pltpu.VMEM((MOE_TILE * ROW_SPLIT, LANES), F32)] * 4 + [
                        pltpu.VMEM((d, f2), BF16), pltpu.VMEM((f2 // 2, d), BF16),
                        pltpu.SemaphoreType.DMA((2, EXPERT_CHUNKS)), pltpu.SemaphoreType.DMA((2, EXPERT_CHUNKS))],
    )
    return pl.pallas_call(
        functools.partial(_experts_kernel, spare=spare),
        grid_spec=grid_spec,
        out_shape=jax.ShapeDtypeStruct(((spare + SPARE_BLOCKS * MOE_TILE) * ROW_SPLIT, LANES), F32),
        compiler_params=_cp(("arbitrary",)),
    )(blk_e, src_tok, src_tok, dst_row, dst_row, m, w1, b1.reshape(-1, e, 1, f2), w2,
      b2.reshape(-1, e, 1, d))


def _combine_kernel(y0_ref, y1_ref, y2_ref, y3_ref, gate_ref, h_ref, mod_ref, o_ref):
    gate = gate_ref[...]
    rows = h_ref.shape[1]
    f = gate[:, 0:1] * _load_row_tiles(y0_ref, 0, rows)
    for k, y_ref in ((1, y1_ref), (2, y2_ref), (3, y3_ref)):
        f = f + gate[:, k:k + 1] * _load_row_tiles(y_ref, 0, rows)
    o_ref[0] = h_ref[0] + mod_ref[0, 0][5:6] * f


def _combine(yu, gate, h1, modtab, sel, out_rows, out_map):
    b, r, d = h1.shape
    nblk = r // ROW_TILE
    tblk = b * nblk
    ysp = lambda k: pl.BlockSpec((ROW_TILE * ROW_SPLIT, LANES), lambda i, j, k=k: (k * tblk + i * nblk + j, 0))
    return pl.pallas_call(
        _combine_kernel,
        grid=(b, nblk),
        in_specs=[
            ysp(0), ysp(1), ysp(2), ysp(3),
            pl.BlockSpec((ROW_TILE, LANES), lambda i, j: (i * nblk + j, 0)),
            pl.BlockSpec((1, ROW_TILE, d), lambda i, j: (i, j, 0)),
            pl.BlockSpec((1, 1, 6, d), lambda i, j: (i, sel(j), 0, 0)),
        ],
        out_specs=pl.BlockSpec((1, ROW_TILE, d), lambda i, j: (i, out_map(j), 0)),
        out_shape=jax.ShapeDtypeStruct((b, out_rows, d), F32),
        compiler_params=_cp(("arbitrary", "arbitrary")),
    )(yu, yu, yu, yu, gate, h1, modtab)


def _moe(m, logits, h1, modtab, sel, layer, w1, b1, w2, b2, out_rows, out_map):
    b, r, d = h1.shape
    t = b * r
    n_assign = t * TOP_K
    idx, gate, counts = _route(logits.reshape(t, N_EXPERTS))
    counts = counts[0].astype(jnp.int32)
    n_blocks = n_assign // MOE_TILE + N_EXPERTS
    n_slots = n_blocks * MOE_TILE
    padded = (counts + MOE_TILE - 1) // MOE_TILE * MOE_TILE
    pad_end = jnp.cumsum(padded)
    fill_end = jnp.cumsum(padded - counts)
    fill = jnp.arange(n_slots - n_assign, dtype=jnp.int32)
    fill_e = jnp.sum(fill[:, None] >= fill_end[None, :], axis=1).astype(jnp.int32)
    keys = jnp.concatenate([idx[:, :TOP_K].reshape(-1), fill_e])
    order = jnp.argsort(keys, stable=True).astype(jnp.int32)
    real = order < n_assign
    src_tok = jnp.where(real, order // TOP_K, 0)
    spare_row = n_assign + jnp.arange(n_slots, dtype=jnp.int32) % (2 * MOE_TILE)
    dst_row = jnp.where(real, (order % TOP_K) * t + order // TOP_K, spare_row)
    blk_start = jnp.arange(n_blocks, dtype=jnp.int32) * MOE_TILE
    blk_e = jnp.minimum(jnp.sum(blk_start[:, None] >= pad_end[None, :], axis=1), N_EXPERTS - 1).astype(jnp.int32)
    yu = _experts(m, blk_e, src_tok * ROW_SPLIT, dst_row * ROW_SPLIT, layer, w1, b1, w2, b2)
    return _combine(yu, gate, h1, modtab, sel, out_rows, out_map)


def _conv_kernel(a_ref, dw_ref, db_ref, lw_ref, lb_ref, o_ref, hs_ref, *, ll):
    hs_ref[0:CONV_PAD, :] = jnp.zeros((CONV_PAD, CONV_WIDTH), F32)
    hs_ref[CONV_PAD + ll:2 * CONV_PAD + ll, :] = jnp.zeros((CONV_PAD, CONV_WIDTH), F32)
    hs_ref[CONV_PAD:CONV_PAD + ll, :] = a_ref[0, :, :CONV_WIDTH] * _sigmoid(a_ref[0, :, CONV_WIDTH:])
    first = CONV_PAD - CONV_K // 2

    def tile(i, c):
        r0 = pl.multiple_of(i * ROW_TILE, ROW_TILE)
        win = hs_ref[pl.ds(r0, ROW_TILE + 2 * CONV_PAD), :]
        acc = jnp.zeros((ROW_TILE, CONV_WIDTH), F32)
        for k in range(CONV_K):
            acc = acc + dw_ref[k:k + 1, :] * win[first + k:first + k + ROW_TILE]
        acc = acc + db_ref[...]
        mu = jnp.mean(acc, axis=-1, keepdims=True)
        dlt = acc - mu
        var = jnp.mean(dlt * dlt, axis=-1, keepdims=True)
        y = dlt * lax.rsqrt(var + EPS) * lw_ref[...] + lb_ref[...]
        o_ref[0, pl.ds(r0, ROW_TILE), :] = _silu(y)
        return c

    lax.fori_loop(0, ll // ROW_TILE, tile, 0)


def _conv(p, dw_w, dw_b, ln_w, ln_b, ll):
    b = p.shape[0]
    row = lambda v: v.reshape(1, CONV_WIDTH)
    vec = pl.BlockSpec((1, CONV_WIDTH), lambda i: (0, 0))
    return pl.pallas_call(
        functools.partial(_conv_kernel, ll=ll),
        grid=(b,),
        in_specs=[pl.BlockSpec((1, ll, 2 * CONV_WIDTH), lambda i: (i, 0, 0)),
                  pl.BlockSpec((CONV_K, CONV_WIDTH), lambda i: (0, 0)), vec, vec, vec],
        out_specs=pl.BlockSpec((1, ll, CONV_WIDTH), lambda i: (i, 0, 0)),
        out_shape=jax.ShapeDtypeStruct((b, ll, CONV_WIDTH), F32),
        scratch_shapes=[pltpu.VMEM((ll + 2 * CONV_PAD, CONV_WIDTH), F32)],
        compiler_params=_cp(("arbitrary",)),
    )(p, dw_w, row(dw_b), row(ln_w), row(ln_b))


def _mla_proj_kernel(cq_ref, ckv_ref, kr_ref, cs_ref, qa_ref, kva_ref, wq_ref, wqs_ref, wk_ref,
                     wv_ref, ek_ref, eks_ref, gq_ref, gqs_ref, gk_ref, gks_ref,
                     q_ref, k_ref, v_ref):
    cos = cs_ref[0]
    sin = cs_ref[1]
    cq = cq_ref[0]
    cqn = (cq * _rms_scale(cq, MLA_Q_RANK) * qa_ref[...]).astype(BF16)
    qx = _dot(cqn, wq_ref[...])
    qs = _dot(cqn, wqs_ref[...])
    ckv = ckv_ref[0]
    ckvn = (ckv * _rms_scale(ckv, MLA_KV_RANK) * kva_ref[...]).astype(BF16)
    kr = kr_ref[0]
    rh = kr.astype(BF16)
    rl = (kr - rh.astype(F32)).astype(BF16)
    kx = _dot(ckvn, wk_ref[...]) + _dot(rh, ek_ref[...]) + _dot(rl, ek_ref[...])
    ks = _dot(rh, eks_ref[...]) + _dot(rl, eks_ref[...])
    v_ref[0] = _dot(ckvn, wv_ref[...]).astype(BF16)
    scale = MLA_QK ** -0.5
    for h in range(MLA_HEADS):
        sl = slice(h * HEAD_PAD, (h + 1) * HEAD_PAD)
        qh = qx[:, sl]
        q_ref[0, :, sl] = (_rms_scale(qh, MLA_QK) * scale
                           * (qh * gq_ref[...] * cos + qs[:, sl] * gqs_ref[...] * sin)).astype(BF16)
        kh = kx[:, sl]
        k_ref[0, :, sl] = (_rms_scale(kh, MLA_QK)
                           * (kh * gk_ref[...] * cos + ks[:, sl] * gks_ref[...] * sin)).astype(BF16)


def _mla_proj(p, cs, qa, kva, wq, wqs, wk, wv, ek, eks, gq, gqs, gk, gks):
    b, s, _ = p.shape
    hw = MLA_HEADS * HEAD_PAD
    full = lambda a: pl.BlockSpec(a.shape, lambda i, j: (0,) * a.ndim)
    out = pl.BlockSpec((1, ROW_TILE, hw), lambda i, j: (i, j, 0))
    consts = (qa, kva, wq, wqs, wk, wv, ek, eks, gq, gqs, gk, gks)
    return pl.pallas_call(
        _mla_proj_kernel,
        grid=(b, s // ROW_TILE),
        in_specs=[pl.BlockSpec((1, ROW_TILE, 512), lambda i, j: (i, j, OD_CQ // 512)),
                  pl.BlockSpec((1, ROW_TILE, MLA_KV_RANK), lambda i, j: (i, j, OD_CKV // MLA_KV_RANK)),
                  pl.BlockSpec((1, ROW_TILE, LANES), lambda i, j: (i, j, OD_KR // LANES)),
                  pl.BlockSpec((2, ROW_TILE, HEAD_PAD), lambda i, j: (0, j, 0))]
                 + [full(a) for a in consts],
        out_specs=[out, out, out],
        out_shape=[jax.ShapeDtypeStruct((b, s, hw), BF16)] * 3,
        compiler_params=_cp(("arbitrary", "arbitrary")),
    )(p, p, p, cs, *consts)


def _attn_kernel(q_ref, k_ref, v_ref, o_ref):
    def scores(h):
        sl = slice(h * HEAD_PAD, (h + 1) * HEAD_PAD)
        return _dot_nt(q_ref[0, :, sl], k_ref[0, :, sl])

    s = scores(0)
    for h in range(MLA_HEADS):
        s_next = scores(h + 1) if h + 1 < MLA_HEADS else None
        sl = slice(h * HEAD_PAD, (h + 1) * HEAD_PAD)
        p = jnp.exp(s - jnp.max(s, axis=-1, keepdims=True))
        den = jnp.sum(p, axis=-1, keepdims=True)
        o_ref[0, :, sl] = (_dot(p.astype(BF16), v_ref[0, :, sl]) / den).astype(BF16)
        s = s_next


def _attention(q, k, v, ll):
    b, s, hw = k.shape
    kv = pl.BlockSpec((1, s, hw), lambda i, j: (i, 0, 0))
    qo = pl.BlockSpec((1, Q_TILE, hw), lambda i, j: (i, j, 0))
    return pl.pallas_call(
        _attn_kernel,
        grid=(b, ll // Q_TILE),
        in_specs=[qo, kv, kv],
        out_specs=qo,
        out_shape=jax.ShapeDtypeStruct((b, ll, hw), BF16),
        compiler_params=_cp(("arbitrary", "arbitrary")),
    )(q, k, v)


def _place(parts, lead):
    out = jnp.zeros(lead + (MLA_HEADS, HEAD_PAD), F32)
    for arr, off in parts:
        out = out.at[..., off:off + arr.shape[-1]].set(arr)
    return out.reshape(lead + (MLA_HEADS * HEAD_PAD,))


def _swap_pairs(a):
    g = a.reshape(a.shape[:-1] + (2, 2, ROPE_AXIS_HALF))
    return jnp.flip(g, axis=-2).reshape(a.shape)


def _rope_tables(rows_total, ll):
    grid_rows = ll // GRID_W
    t_row = jnp.repeat(jnp.arange(grid_rows, dtype=F32), GRID_W)
    t_col = jnp.tile(jnp.arange(GRID_W, dtype=F32), grid_rows)
    inv = 1.0 / (ROPE_BASE ** (jnp.arange(ROPE_AXIS_HALF, dtype=F32) / ROPE_AXIS_HALF))
    ang_r = t_row[:, None] * inv
    ang_c = t_col[:, None] * inv
    cos32 = jnp.concatenate([jnp.cos(ang_r), jnp.cos(ang_r), jnp.cos(ang_c), jnp.cos(ang_c)], axis=-1)
    sin32 = jnp.concatenate([-jnp.sin(ang_r), jnp.sin(ang_r), -jnp.sin(ang_c), jnp.sin(ang_c)], axis=-1)
    extra = rows_total - ll
    cos32 = jnp.concatenate([cos32, jnp.ones((extra, MLA_ROPE), F32)], axis=0)
    sin32 = jnp.concatenate([sin32, jnp.zeros((extra, MLA_ROPE), F32)], axis=0)
    cos_t = jnp.concatenate([jnp.ones((rows_total, MLA_NOPE), F32), cos32,
                             jnp.zeros((rows_total, HEAD_PAD - MLA_QK), F32)], axis=-1)
    sin_t = jnp.concatenate([jnp.zeros((rows_total, MLA_NOPE), F32), sin32,
                             jnp.zeros((rows_total, HEAD_PAD - MLA_QK), F32)], axis=-1)
    return jnp.stack([cos_t, sin_t])


def kernel(x, c, ctx, c_ctx, ada_w, ada_b, ev_w_in, hgrn_lb_logits, hgrn_norm_w, pool_w, pool_scale,
           ev_w_out, od_w_in, conv_dw_w, conv_dw_b, conv_ln_w, conv_ln_b, mla_q_a_norm, mla_w_uq,
           mla_kv_a_norm, mla_w_ukv, mla_q_norm, mla_k_norm, od_w_out, moe_router_w, moe_router_b,
           moe_w1, moe_b1, moe_w2, moe_b2):
    b, ll, d = x.shape
    lc = ctx.shape[1]
    s = ll + lc
    nc, nl = lc // ROW_TILE, ll // ROW_TILE
    assert ada_w.shape[0] == 2 and d == D_MODEL
    assert ll % Q_TILE == 0 and lc % ROW_TILE == 0 and (b * s) % ROUTE_TILE == 0 and (b * ll) % ROUTE_TILE == 0

    cond_rows = (b + 1 + 7) // 8 * 8
    cond = jnp.zeros((cond_rows, d), F32).at[:b].set(c).at[b].set(c_ctx)
    mod = _ada_table(cond, ada_w, ada_b)
    modtab = []
    for layer in range(2):
        lat = mod[layer, :b].reshape(b, 1, 6, d)
        con = jnp.broadcast_to(mod[layer, b].reshape(1, 1, 6, d), (b, 1, 6, d))
        modtab.append(jnp.concatenate([con, lat], axis=1))
    lb_all = jnp.cumsum(jax.nn.softmax(hgrn_lb_logits.astype(F32), axis=0), axis=0)

    def router_pieces(w):
        hi, mid, _ = _split3(w.T)
        return jnp.concatenate([hi, mid], axis=0)

    sel0 = lambda j: jnp.where(j < nc, 0, 1)
    h = jnp.concatenate([ctx, x], axis=1)
    p0 = _modproj(h, modtab[0], ev_w_in[0].astype(BF16), sel0)
    hg = _hgrn(p0, lb_all[0], hgrn_norm_w[0], nc, nl)
    yp = _pool(p0, pool_w[0], pool_scale[0], lc, ll)
    wo = ev_w_out[0].astype(BF16)
    h1, m, lg = _mixout(hg, yp, h, modtab[0], wo[:HG_W], wo[HG_W:], router_pieces(moe_router_w[0]),
                        moe_router_b[0].reshape(1, N_EXPERTS), sel0, 0, s)
    to_l1 = lambda j: jnp.where(j < nc, j + nl, j - nc)
    h = _moe(m, lg, h1, modtab[0], sel0, 0, moe_w1, moe_b1, moe_w2, moe_b2, s, to_l1)

    sel1 = lambda j: jnp.where(j < nl, 1, 0)
    w_in = od_w_in[0]
    kr_cols = w_in[:, 1664:1696]
    w1p = jnp.zeros((d, OD_IN_PAD), F32)
    w1p = w1p.at[:, :1024].set(w_in[:, :1024]).at[:, OD_CQ:OD_CQ + MLA_Q_RANK].set(w_in[:, 1024:1408])
    w1p = w1p.at[:, OD_CKV:OD_CKV + MLA_KV_RANK].set(w_in[:, 1408:1664])
    w1p = w1p.at[:, OD_KR:OD_KR + MLA_ROPE].set(kr_cols)
    w1p = w1p.at[:, OD_KR + MLA_ROPE:OD_KR + 2 * MLA_ROPE].set(_swap_pairs(kr_cols))
    p1 = _modproj(h, modtab[1], w1p.astype(BF16), sel1)
    hc = _conv(p1, conv_dw_w[0], conv_dw_b[0], conv_ln_w[0], conv_ln_b[0], ll)

    wuq = mla_w_uq[0].reshape(MLA_Q_RANK, MLA_HEADS, MLA_QK)
    wuq_rope = wuq[..., MLA_NOPE:]
    pad_rows = lambda w: jnp.zeros((512, w.shape[1]), F32).at[:MLA_Q_RANK].set(w)
    wq = pad_rows(_place([(wuq, 0)], (MLA_Q_RANK,))).astype(BF16)
    wqs = pad_rows(_place([(_swap_pairs(wuq_rope), MLA_NOPE)], (MLA_Q_RANK,))).astype(BF16)
    wukv = mla_w_ukv[0].reshape(MLA_KV_RANK, MLA_HEADS, MLA_NOPE + MLA_V)
    wk = _place([(wukv[..., :MLA_NOPE], 0)], (MLA_KV_RANK,)).astype(BF16)
    wv = _place([(wukv[..., MLA_NOPE:], 0)], (MLA_KV_RANK,)).astype(BF16)
    eye = jnp.broadcast_to(jnp.eye(MLA_ROPE, dtype=F32)[:, None, :], (MLA_ROPE, MLA_HEADS, MLA_ROPE))
    zero_rows = jnp.zeros((LANES - 2 * MLA_ROPE, MLA_HEADS * HEAD_PAD), F32)
    place_kr = _place([(eye, MLA_NOPE)], (MLA_ROPE,))
    zero_kr = jnp.zeros_like(place_kr)
    ek = jnp.concatenate([place_kr, zero_kr, zero_rows], axis=0).astype(BF16)
    eks = jnp.concatenate([zero_kr, place_kr, zero_rows], axis=0).astype(BF16)
    lane_gain = lambda g: jnp.concatenate([g, jnp.zeros((HEAD_PAD - MLA_QK,), F32)]).reshape(1, HEAD_PAD)
    swapped_gain = lambda g: jnp.concatenate(
        [jnp.zeros((MLA_NOPE,), F32), _swap_pairs(g[MLA_NOPE:]), jnp.zeros((HEAD_PAD - MLA_QK,), F32)]
    ).reshape(1, HEAD_PAD)
    qa = jnp.concatenate([mla_q_a_norm[0], jnp.zeros((512 - MLA_Q_RANK,), F32)]).reshape(1, 512)
    q, k, v = _mla_proj(p1, _rope_tables(s, ll), qa, mla_kv_a_norm[0].reshape(1, MLA_KV_RANK),
                        wq, wqs, wk, wv, ek, eks,
                        lane_gain(mla_q_norm[0]), swapped_gain(mla_q_norm[0]),
                        lane_gain(mla_k_norm[0]), swapped_gain(mla_k_norm[0]))
    attn = _attention(q, k, v, ll)

    wo1 = od_w_out[0]
    wo_attn = wo1[CONV_WIDTH:].reshape(MLA_HEADS, MLA_V, d)
    wo_attn = jnp.zeros((MLA_HEADS, HEAD_PAD, d), F32).at[:, :MLA_V].set(wo_attn).reshape(-1, d)
    h1, m, lg = _mixout(hc, attn, h, modtab[1], wo1[:CONV_WIDTH].astype(BF16), wo_attn.astype(BF16),
                        router_pieces(moe_router_w[1]), moe_router_b[1].reshape(1, N_EXPERTS),
                        lambda j: 1, 0, ll)
    return _moe(m, lg, h1, modtab[1], lambda j: 1, 1, moe_w1, moe_b1, moe_w2, moe_b2, ll, lambda j: j)
```

```python
import functools

import jax
import jax.numpy as jnp
from jax import lax
from jax.experimental import pallas as pl
from jax.experimental.pallas import tpu as pltpu

F32 = jnp.float32
BF16 = jnp.bfloat16

D_MODEL = 1024
GRID_W = 64
EPS = 1e-6
HG_HEADS = 4
HG_DIM = 128
HG_W = HG_HEADS * HG_DIM
HG_SUB = 32
HG_HEADS_PER_STEP = 2
POOL_WINDOWS = (2, 4, 8, 16)
POOL_GROUP = 128
POOL_PAD = 16
CONV_WIDTH = 512
CONV_K = 31
CONV_PAD = 16
MLA_HEADS = 8
MLA_Q_RANK = 384
MLA_KV_RANK = 256
MLA_NOPE = 64
MLA_ROPE = 32
MLA_V = 64
MLA_QK = MLA_NOPE + MLA_ROPE
HEAD_PAD = 128
ROPE_AXIS_HALF = MLA_ROPE // 4
ROPE_BASE = 10000.0
N_EXPERTS = 32
TOP_K = 4
D_FF = 1024
SWIGLU_LIMIT = 7.0
SWIGLU_ALPHA = 1.702

ROW_TILE = 256
ROUTE_TILE = 512
MOE_TILE = 512
EXPERT_CHUNKS = 4
SPARE_BLOCKS = 4
Q_TILE = 512
LANES = 128
ROW_SPLIT = D_MODEL // LANES
VMEM_LIMIT = 56 * 1024 * 1024

OD_CQ = 1024
OD_CKV = 1536
OD_KR = 1792
OD_IN_PAD = 1920


def _cp(sem, vmem=VMEM_LIMIT):
    return pltpu.CompilerParams(dimension_semantics=sem, vmem_limit_bytes=vmem)


def _dot(a, b):
    return jnp.dot(a, b, preferred_element_type=F32)


def _dot_nt(a, b):
    return lax.dot_general(a, b, (((1,), (1,)), ((), ())), preferred_element_type=F32)


def _dot_tn(a, b):
    return lax.dot_general(a, b, (((0,), (0,)), ((), ())), preferred_element_type=F32)


def _split3(x):
    hi = x.astype(BF16)
    r1 = x - hi.astype(F32)
    mid = r1.astype(BF16)
    lo = (r1 - mid.astype(F32)).astype(BF16)
    return hi, mid, lo


def _sigmoid(x):
    return 1.0 / (1.0 + jnp.exp(-x))


def _silu(x):
    return x * _sigmoid(x)


def _load_row_tiles(ref, row0, rows):
    return jnp.concatenate(
        [ref[pl.ds(row0 * ROW_SPLIT + s, rows, stride=ROW_SPLIT), :] for s in range(ROW_SPLIT)], axis=1)


def _store_row_tiles(ref, row0, x):
    for s in range(ROW_SPLIT):
        ref[pl.ds(row0 * ROW_SPLIT + s, x.shape[0], stride=ROW_SPLIT), :] = x[:, s * LANES:(s + 1) * LANES]


def _rms_scale(x, width):
    return lax.rsqrt(jnp.sum(x * x, axis=-1, keepdims=True) * (1.0 / width) + EPS)


def _ada_kernel(c_ref, w_ref, b_ref, o_ref):
    s = _silu(c_ref[...])
    sh, sm, sl = _split3(s)
    wh, wm, wl = _split3(w_ref[0])
    acc = _dot(sh, wh) + _dot(sh, wm) + _dot(sm, wh) + _dot(sh, wl) + _dot(sl, wh) + _dot(sm, wm)
    o_ref[0] = acc + b_ref[0]


def _ada_table(cond, ada_w, ada_b):
    depth, d, n = ada_w.shape
    rows = cond.shape[0]
    nb = n // d
    return pl.pallas_call(
        _ada_kernel,
        grid=(depth, nb),
        in_specs=[
            pl.BlockSpec((rows, d), lambda l, j: (0, 0)),
            pl.BlockSpec((1, d, d), lambda l, j: (l, 0, j)),
            pl.BlockSpec((1, 1, d), lambda l, j: (l, 0, j)),
        ],
        out_specs=pl.BlockSpec((1, rows, d), lambda l, j: (l, 0, j)),
        out_shape=jax.ShapeDtypeStruct((depth, rows, n), F32),
        compiler_params=_cp(("arbitrary", "arbitrary")),
    )(cond, ada_w, ada_b.reshape(depth, 1, n))


def _token_rows(h_refs, nc):
    if len(h_refs) == 1:
        return h_refs[0][0]
    return jnp.where(pl.program_id(1) < nc, h_refs[0][0], h_refs[1][0])


def _token_specs(h, nc, d, off=0):
    if not isinstance(h, tuple):
        return [pl.BlockSpec((1, ROW_TILE, d), lambda i, j: (i, j + off, 0))]
    return [pl.BlockSpec((1, ROW_TILE, d), lambda i, j: (i, jnp.minimum(j, nc - 1), 0)),
            pl.BlockSpec((1, ROW_TILE, d), lambda i, j: (i, jnp.maximum(j - nc, 0), 0))]


def _modproj_kernel(*refs, nc):
    *h_refs, mod_ref, w_ref, o_ref = refs
    x = _token_rows(h_refs, nc)
    mod = mod_ref[0, 0]
    u = x * _rms_scale(x, x.shape[-1]) * (1.0 + mod[1:2]) + mod[0:1]
    o_ref[0] = _dot(u.astype(BF16), w_ref[...])


def _modproj(h, modtab, w, sel, nc=0):
    parts = h if isinstance(h, tuple) else (h,)
    b, d = parts[0].shape[0], parts[0].shape[2]
    s = sum(p.shape[1] for p in parts)
    n = w.shape[1]
    return pl.pallas_call(
        functools.partial(_modproj_kernel, nc=nc),
        grid=(b, s // ROW_TILE),
        in_specs=_token_specs(h, nc, d) + [
            pl.BlockSpec((1, 1, 6, d), lambda i, j: (i, sel(j), 0, 0)),
            pl.BlockSpec((d, n), lambda i, j: (0, 0)),
        ],
        out_specs=pl.BlockSpec((1, ROW_TILE, n), lambda i, j: (i, j, 0)),
        out_shape=jax.ShapeDtypeStruct((b, s, n), F32),
        compiler_params=_cp(("arbitrary", "arbitrary")),
    )(*parts, modtab, w)


def _hgrn_masks(rows, reverse):
    ri = lax.broadcasted_iota(jnp.int32, (rows, rows), 0)
    ci = lax.broadcasted_iota(jnp.int32, (rows, rows), 1)
    same = (ri // HG_SUB) == (ci // HG_SUB)
    causal = same & ((ci >= ri) if reverse else (ci <= ri))
    tri_f = jnp.where(causal, 1.0, 0.0)
    return tri_f, tri_f.astype(BF16)


def _hgrn_block(q, z, v, lb, st, reverse, masks):
    rows = q.shape[0]
    nsub = rows // HG_SUB
    f = lb + (1.0 - lb) * _sigmoid(z)
    kin = 1.0 - f
    logf = jnp.log(f)
    tri_f, tri = masks
    causal = tri_f > 0.0
    lh, lm, ll = _split3(logf)
    bcum = _dot(tri, lh) + _dot(tri, lm) + _dot(tri, ll)
    b3 = bcum.reshape(nsub, HG_SUB, HG_DIM)
    tot_row = 0 if reverse else HG_SUB - 1
    mid_row = HG_SUB // 2 if reverse else HG_SUB // 2 - 1
    tot = b3[:, tot_row:tot_row + 1, :]
    mid = b3[:, mid_row:mid_row + 1, :]
    q3 = q.reshape(nsub, HG_SUB, HG_DIM)
    k3 = kin.reshape(nsub, HG_SUB, HG_DIM)
    q_dec = (q3 * jnp.exp(b3)).astype(BF16)
    k_end = (k3 * jnp.exp(tot - b3)).astype(BF16)
    q_mid = (q3 * jnp.exp(b3 - mid)).reshape(rows, HG_DIM).astype(BF16)
    k_mid = (k3 * jnp.exp(mid - b3)).reshape(rows, HG_DIM).astype(BF16)
    vb = v.astype(BF16)
    att = jnp.where(causal, _dot_nt(q_mid, k_mid), 0.0).astype(BF16)
    o_intra = _dot(att, vb)
    v3 = vb.reshape(nsub, HG_SUB, HG_DIM)
    dec = jnp.exp(tot)
    upd = [_dot_tn(v3[j], k_end[j]) for j in range(nsub)]
    order = range(nsub - 1, -1, -1) if reverse else range(nsub)
    o_inter = [None] * nsub
    for j in order:
        o_inter[j] = _dot_nt(q_dec[j], st.astype(BF16))
        st = st * dec[j] + upd[j]
    o = o_intra + jnp.concatenate(o_inter, axis=0)
    return o, st


def _hgrn_kernel(q_ref, zf_ref, zb_ref, v_ref, og_ref, lb_ref, nw_ref, o_ref, acc_ref, *, nc, nl):
    blk = ROW_TILE
    nblk = nc + nl
    acc_ref[...] = jnp.zeros_like(acc_ref)
    masks_f = _hgrn_masks(blk, False)
    masks_b = _hgrn_masks(blk, True)
    heads = [slice(k * HG_DIM, (k + 1) * HG_DIM) for k in range(HG_HEADS_PER_STEP)]

    def body(i, carry):
        rf = pl.multiple_of(i * blk, blk)
        ib = jnp.where(i < nc, nc - 1 - i, 2 * nc + nl - 1 - i)
        rb = pl.multiple_of(ib * blk, blk)
        out = []
        for k, sl in enumerate(heads):
            st_f, st_b = carry[2 * k], carry[2 * k + 1]
            qf = _silu(q_ref[0, pl.ds(rf, blk), sl])
            o_f, st_f = _hgrn_block(qf, zf_ref[0, pl.ds(rf, blk), sl], v_ref[0, pl.ds(rf, blk), sl],
                                    lb_ref[0:1, sl], st_f, False, masks_f)
            acc_ref[pl.ds(rf, blk), sl] += o_f
            qb = _silu(q_ref[0, pl.ds(rb, blk), sl])
            o_b, st_b = _hgrn_block(qb, zb_ref[0, pl.ds(rb, blk), sl], v_ref[0, pl.ds(rb, blk), sl],
                                    lb_ref[1:2, sl], st_b, True, masks_b)
            acc_ref[pl.ds(rb, blk), sl] += o_b
            out += [st_f, st_b]
        return tuple(out)

    zero = jnp.zeros((HG_DIM, HG_DIM), F32)
    lax.fori_loop(0, nblk, body, (zero,) * (2 * HG_HEADS_PER_STEP))
    for sl in heads:
        o = acc_ref[:, sl]
        o_ref[0, :, sl] = o * _rms_scale(o, HG_DIM) * nw_ref[...] * _silu(og_ref[0, :, sl])


def _hgrn(p, lb, norm_w, nc, nl):
    b, s, _ = p.shape
    width = HG_HEADS_PER_STEP * HG_DIM
    steps = HG_HEADS // HG_HEADS_PER_STEP
    sec = lambda k: pl.BlockSpec((1, s, width), lambda i, h, k=k: (i, 0, steps * k + h))
    return pl.pallas_call(
        functools.partial(_hgrn_kernel, nc=nc, nl=nl),
        grid=(b, steps),
        in_specs=[sec(0), sec(1), sec(2), sec(3), sec(4),
                  pl.BlockSpec((2, width), lambda i, h: (0, h)),
                  pl.BlockSpec((1, HG_DIM), lambda i, h: (0, 0))],
        out_specs=pl.BlockSpec((1, s, width), lambda i, h: (i, 0, h)),
        out_shape=jax.ShapeDtypeStruct((b, s, HG_W), F32),
        scratch_shapes=[pltpu.VMEM((s, width), F32)],
        compiler_params=_cp(("arbitrary", "arbitrary")),
    )(p, p, p, p, p, lb, norm_w.reshape(1, HG_DIM))


def _pool_kernel(x_ref, w_ref, sc_ref, o_ref, *, lc, ll):
    z = jnp.zeros((POOL_PAD, POOL_GROUP), F32)
    n = lc + ll + 3 * POOL_PAD
    pos_c = lax.broadcasted_iota(jnp.int32, (lc, 1), 0)
    pos_l = lax.broadcasted_iota(jnp.int32, (ll, 1), 0)
    for g, w in enumerate(POOL_WINDOWS):
        x = x_ref[0, :, g * POOL_GROUP:(g + 1) * POOL_GROUP]
        xp = jnp.concatenate([z, x[:lc], z, x[lc:], z], axis=0)
        acc = xp + pltpu.roll(xp, 1, 0)
        half = 1
        while 2 * half < w:
            acc = pltpu.roll(acc, half, 0) + pltpu.roll(acc, n - half, 0)
            half *= 2
        win = jnp.concatenate([acc[POOL_PAD:POOL_PAD + lc],
                               acc[2 * POOL_PAD + lc:2 * POOL_PAD + lc + ll]], axis=0)

        def count(pos, length):
            lo = jnp.maximum(pos - w // 2, 0)
            hi = jnp.minimum(pos + w - w // 2, length)
            return (hi - lo).astype(F32)

        cnt = jnp.concatenate([count(pos_c, lc), count(pos_l, ll)], axis=0)
        d = win / cnt - x
        y = _dot(d.astype(BF16), w_ref[g])
        o_ref[0, :, g * POOL_GROUP:(g + 1) * POOL_GROUP] = y * sc_ref[:, g * POOL_GROUP:(g + 1) * POOL_GROUP]


def _pool(p, pool_w, pool_scale, lc, ll):
    b, s, n = p.shape
    width = len(POOL_WINDOWS) * POOL_GROUP
    return pl.pallas_call(
        functools.partial(_pool_kernel, lc=lc, ll=ll),
        grid=(b,),
        in_specs=[pl.BlockSpec((1, s, width), lambda i: (i, 0, n // width - 1)),
                  pl.BlockSpec(pool_w.shape, lambda i: (0, 0, 0)),
                  pl.BlockSpec((1, width), lambda i: (0, 0))],
        out_specs=pl.BlockSpec((1, s, width), lambda i: (i, 0, 0)),
        out_shape=jax.ShapeDtypeStruct((b, s, width), F32),
        compiler_params=_cp(("arbitrary",)),
    )(p, pool_w.astype(BF16), pool_scale.reshape(1, width))


def _mixout_kernel(a_ref, b_ref, *refs, nc):
    *h_refs, mod_ref, wa_ref, wb_ref, wr_ref, br_ref, h1_ref, m_ref, lg_ref = refs
    mod = mod_ref[0, 0]
    y = _dot(a_ref[0].astype(BF16), wa_ref[...]) + _dot(b_ref[0].astype(BF16), wb_ref[...])
    h1 = _token_rows(h_refs, nc) + mod[2:3] * y
    h1_ref[0] = h1
    m = h1 * _rms_scale(h1, h1.shape[-1]) * (1.0 + mod[4:5]) + mod[3:4]
    _store_row_tiles(m_ref, 0, m)
    mh, mm, _ = _split3(m)
    hi = _dot_nt(mh, wr_ref[...])
    lg = hi[:, :N_EXPERTS] + hi[:, N_EXPERTS:] + _dot_nt(mm, wr_ref[0:N_EXPERTS, :])
    lg_ref[0] = lg + br_ref[...]


def _mixout(a, bmix, h, modtab, wa, wb, wr3, br, sel, h_off, rows, nc=0):
    parts = h if isinstance(h, tuple) else (h,)
    b = parts[0].shape[0]
    d = parts[0].shape[-1]
    ka, kb = a.shape[-1], bmix.shape[-1]
    out3 = lambda n: jax.ShapeDtypeStruct((b, rows, n), F32)
    return pl.pallas_call(
        functools.partial(_mixout_kernel, nc=nc),
        grid=(b, rows // ROW_TILE),
        in_specs=[
            pl.BlockSpec((1, ROW_TILE, ka), lambda i, j: (i, j, 0)),
            pl.BlockSpec((1, ROW_TILE, kb), lambda i, j: (i, j, 0)),
        ] + _token_specs(h, nc, d, h_off) + [
            pl.BlockSpec((1, 1, 6, d), lambda i, j: (i, sel(j), 0, 0)),
            pl.BlockSpec((ka, d), lambda i, j: (0, 0)),
            pl.BlockSpec((kb, d), lambda i, j: (0, 0)),
            pl.BlockSpec((2 * N_EXPERTS, d), lambda i, j: (0, 0)),
            pl.BlockSpec((1, N_EXPERTS), lambda i, j: (0, 0)),
        ],
        out_specs=[
            pl.BlockSpec((1, ROW_TILE, d), lambda i, j: (i, j, 0)),
            pl.BlockSpec((ROW_TILE * ROW_SPLIT, LANES), lambda i, j: (i * (rows // ROW_TILE) + j, 0)),
            pl.BlockSpec((1, ROW_TILE, N_EXPERTS), lambda i, j: (i, j, 0)),
        ],
        out_shape=[out3(d), jax.ShapeDtypeStruct((b * rows * ROW_SPLIT, LANES), F32), out3(N_EXPERTS)],
        compiler_params=_cp(("arbitrary", "arbitrary")),
    )(a, bmix, *parts, modtab, wa, wb, wr3, br)


def _route_kernel(lg_ref, idx_ref, gate_ref, cnt_ref):
    t = lg_ref.shape[0]

    @pl.when(pl.program_id(0) == 0)
    def _():
        cnt_ref[...] = jnp.zeros_like(cnt_ref)

    lg = lg_ref[...]
    lane = lax.broadcasted_iota(jnp.int32, (t, N_EXPERTS), 1).astype(F32)
    hot, vals, idxs = [], [], []
    for _ in range(TOP_K):
        mx = jnp.max(lg, axis=-1, keepdims=True)
        ix = jnp.min(jnp.where(lg == mx, lane, float(N_EXPERTS)), axis=-1, keepdims=True)
        oh = lane == ix
        hot.append(oh)
        vals.append(mx)
        idxs.append(ix.astype(jnp.int32))
        lg = jnp.where(oh, -jnp.inf, lg)
    ex = [jnp.exp(v - vals[0]) for v in vals]
    den = ex[0] + ex[1] + ex[2] + ex[3]
    sel = jnp.where(hot[0] | hot[1] | hot[2] | hot[3], 1.0, 0.0)
    out_lane = lax.broadcasted_iota(jnp.int32, (t, LANES), 1)
    idx_o = jnp.zeros((t, LANES), jnp.int32)
    gate_o = jnp.zeros((t, LANES), F32)
    for k in range(TOP_K):
        idx_o = jnp.where(out_lane == k, idxs[k], idx_o)
        gate_o = jnp.where(out_lane == k, ex[k] / den, gate_o)
    idx_ref[...] = idx_o
    gate_ref[...] = gate_o
    cnt_ref[...] += jnp.sum(sel, axis=0, keepdims=True)


def _route(logits):
    t = logits.shape[0]
    tile = pl.BlockSpec((ROUTE_TILE, LANES), lambda i: (i, 0))
    return pl.pallas_call(
        _route_kernel,
        grid=(t // ROUTE_TILE,),
        in_specs=[pl.BlockSpec((ROUTE_TILE, N_EXPERTS), lambda i: (i, 0))],
        out_specs=[tile, tile, pl.BlockSpec((1, N_EXPERTS), lambda i: (0, 0))],
        out_shape=[jax.ShapeDtypeStruct((t, LANES), jnp.int32),
                   jax.ShapeDtypeStruct((t, LANES), F32),
                   jax.ShapeDtypeStruct((1, N_EXPERTS), F32)],
        compiler_params=_cp(("arbitrary",)),
    )(logits)


def _experts_kernel(be_ref, src0_ref, srcn_ref, dstp_ref, dstc_ref, m_ref, w1_ref, b1_ref, w2_ref,
                    b2_ref, yu_ref, xa, xb, ya, yb, w1b, w2b, gsem, ssem, *, spare):
    g = pl.program_id(0)
    last = pl.num_programs(0) - 1
    tile = xa.shape[0] // ROW_SPLIT
    xbuf = (xa, xb)
    ybuf = (ya, yb)

    def tile_rows(start):
        return pl.ds(pl.multiple_of(start, ROW_SPLIT), ROW_SPLIT)

    rows = tile // EXPERT_CHUNKS

    def gather(idx_ref, r, s):
        return pltpu.make_async_copy(m_ref.at[tile_rows(idx_ref[r])], xbuf[s].at[tile_rows(r * ROW_SPLIT)],
                                     gsem.at[s, r // rows])

    def scatter(row, r, s):
        return pltpu.make_async_copy(ybuf[s].at[tile_rows(r * ROW_SPLIT)], yu_ref.at[tile_rows(row)],
                                     ssem.at[s, r // rows])

    def chunk(c):
        return pl.ds(c * rows * ROW_SPLIT, rows * ROW_SPLIT)

    def wait_gather(s, c):
        pltpu.make_async_copy(m_ref.at[chunk(0)], xbuf[s].at[chunk(c)], gsem.at[s, c]).wait()

    def wait_scatter(s, c):
        pltpu.make_async_copy(ybuf[s].at[chunk(c)], yu_ref.at[chunk(0)], ssem.at[s, c]).wait()

    @pl.when(g == 0)
    def _():
        ya[...] = jnp.zeros_like(ya)
        yb[...] = jnp.zeros_like(yb)

        def first(r, c):
            gather(src0_ref, r, 0).start()
            scatter((spare + 3 * tile + r) * ROW_SPLIT, r, 0).start()
            return c

        lax.fori_loop(0, tile, first, 0)

    changed = (g == 0) | (be_ref[g] != be_ref[jnp.maximum(g - 1, 0)])

    @pl.when(changed)
    def _():
        w1b[...] = w1_ref[0, 0].astype(BF16)
        w2b[...] = w2_ref[0, 0].astype(BF16)

    def block(slot, other):
        for c in range(EXPERT_CHUNKS):
            wait_gather(slot, c)
            wait_scatter(slot, c)
            for r in range(c * rows, (c + 1) * rows):
                gather(srcn_ref, r, other).start(priority=r % 2)
                scatter(jnp.where(g > 0, dstp_ref[r], (spare + 2 * tile + r) * ROW_SPLIT), r,
                        other).start(priority=(r + 1) % 2)
            x = _load_row_tiles(xbuf[slot], c * rows, rows).astype(BF16)
            hb = _dot(x, w1b[...]) + b1_ref[0, 0]
            gl = jnp.minimum(hb[:, :D_FF], SWIGLU_LIMIT)
            lin = jnp.clip(hb[:, D_FF:], -SWIGLU_LIMIT, SWIGLU_LIMIT)
            a = gl * _sigmoid(SWIGLU_ALPHA * gl) * (lin + 1.0)
            _store_row_tiles(ybuf[slot], c * rows, _dot(a.astype(BF16), w2b[...]) + b2_ref[0, 0])

        @pl.when(g == last)
        def _():
            def final(r, c):
                scatter(dstc_ref[r], r, slot).start()
                return c

            lax.fori_loop(0, tile, final, 0)
            for c in range(EXPERT_CHUNKS):
                wait_scatter(other, c)
                wait_scatter(slot, c)
                wait_gather(other, c)

    @pl.when(g % 2 == 1)
    def _():
        block(1, 0)

    @pl.when(g % 2 == 0)
    def _():
        block(0, 1)


def _experts(m, blk_e, src_tok, dst_row, layer, w1, b1, w2, b2):
    t, d = m.shape[0] // ROW_SPLIT, D_MODEL
    _, e, _, f2 = w1.shape
    nb = src_tok.shape[0] // MOE_TILE
    spare = TOP_K * t
    idx = lambda fn: pl.BlockSpec((MOE_TILE,), fn, memory_space=pltpu.SMEM)
    grid_spec = pltpu.PrefetchScalarGridSpec(
        num_scalar_prefetch=1,
        grid=(nb,),
        in_specs=[
            idx(lambda i, be: (0,)),
            idx(lambda i, be: (jnp.minimum(i + 1, nb - 1),)),
            idx(lambda i, be: (jnp.maximum(i - 1, 0),)),
            idx(lambda i, be: (i,)),
            pl.BlockSpec(memory_space=pl.ANY),
            pl.BlockSpec((1, 1, d, f2), lambda i, be: (layer, be[i], 0, 0)),
            pl.BlockSpec((1, 1, 1, f2), lambda i, be: (layer, be[i], 0, 0)),
            pl.BlockSpec((1, 1, f2 // 2, d), lambda i, be: (layer, be[i], 0, 0)),
            pl.BlockSpec((1, 1, 1, d), lambda i, be: (layer, be[i], 0, 0)),
        ],
        out_specs=pl.BlockSpec(memory_space=pl.ANY),
        scratch_shapes=[pltpu.VMEM((MOE_TILE * ROW_SPLIT, LANES), F32)] * 4 + [
                        pltpu.VMEM((d, f2), BF16), pltpu.VMEM((f2 // 2, d), BF16),
                        pltpu.SemaphoreType.DMA((2, EXPERT_CHUNKS)), pltpu.SemaphoreType.DMA((2, EXPERT_CHUNKS))],
    )
    return pl.pallas_call(
        functools.partial(_experts_kernel, spare=spare),
        grid_spec=grid_spec,
        out_shape=jax.ShapeDtypeStruct(((spare + SPARE_BLOCKS * MOE_TILE) * ROW_SPLIT, LANES), F32),
        compiler_params=_cp(("arbitrary",)),
    )(blk_e, src_tok, src_tok, dst_row, dst_row, m, w1, b1.reshape(-1, e, 1, f2), w2,
      b2.reshape(-1, e, 1, d))


def _combine_kernel(y0_ref, y1_ref, y2_ref, y3_ref, gate_ref, h_ref, mod_ref, o_ref):
    gate = gate_ref[...]
    rows = h_ref.shape[1]
    f = gate[:, 0:1] * _load_row_tiles(y0_ref, 0, rows)
    for k, y_ref in ((1, y1_ref), (2, y2_ref), (3, y3_ref)):
        f = f + gate[:, k:k + 1] * _load_row_tiles(y_ref, 0, rows)
    o_ref[0] = h_ref[0] + mod_ref[0, 0][5:6] * f


def _combine(yu, gate, h1, modtab, sel, out_rows, out_map):
    b, r, d = h1.shape
    nblk = r // ROW_TILE
    tblk = b * nblk
    ysp = lambda k: pl.BlockSpec((ROW_TILE * ROW_SPLIT, LANES), lambda i, j, k=k: (k * tblk + i * nblk + j, 0))
    return pl.pallas_call(
        _combine_kernel,
        grid=(b, nblk),
        in_specs=[
            ysp(0), ysp(1), ysp(2), ysp(3),
            pl.BlockSpec((ROW_TILE, LANES), lambda i, j: (i * nblk + j, 0)),
            pl.BlockSpec((1, ROW_TILE, d), lambda i, j: (i, j, 0)),
            pl.BlockSpec((1, 1, 6, d), lambda i, j: (i, sel(j), 0, 0)),
        ],
        out_specs=pl.BlockSpec((1, ROW_TILE, d), lambda i, j: (i, out_map(j), 0)),
        out_shape=jax.ShapeDtypeStruct((b, out_rows, d), F32),
        compiler_params=_cp(("arbitrary", "arbitrary")),
    )(yu, yu, yu, yu, gate, h1, modtab)


def _moe(m, logits, h1, modtab, sel, layer, w1, b1, w2, b2, out_rows, out_map):
    b, r, d = h1.shape
    t = b * r
    n_assign = t * TOP_K
    idx, gate, counts = _route(logits.reshape(t, N_EXPERTS))
    counts = counts[0].astype(jnp.int32)
    n_blocks = n_assign // MOE_TILE + N_EXPERTS
    n_slots = n_blocks * MOE_TILE
    padded = (counts + MOE_TILE - 1) // MOE_TILE * MOE_TILE
    pad_end = jnp.cumsum(padded)
    fill_end = jnp.cumsum(padded - counts)
    fill = jnp.arange(n_slots - n_assign, dtype=jnp.int32)
    fill_e = jnp.sum(fill[:, None] >= fill_end[None, :], axis=1).astype(jnp.int32)
    keys = jnp.concatenate([idx[:, :TOP_K].reshape(-1), fill_e])
    order = jnp.argsort(keys, stable=True).astype(jnp.int32)
    real = order < n_assign
    src_tok = jnp.where(real, order // TOP_K, 0)
    spare_row = n_assign + jnp.arange(n_slots, dtype=jnp.int32) % (2 * MOE_TILE)
    dst_row = jnp.where(real, (order % TOP_K) * t + order // TOP_K, spare_row)
    blk_start = jnp.arange(n_blocks, dtype=jnp.int32) * MOE_TILE
    blk_e = jnp.minimum(jnp.sum(blk_start[:, None] >= pad_end[None, :], axis=1), N_EXPERTS - 1).astype(jnp.int32)
    yu = _experts(m, blk_e, src_tok * ROW_SPLIT, dst_row * ROW_SPLIT, layer, w1, b1, w2, b2)
    return _combine(yu, gate, h1, modtab, sel, out_rows, out_map)


def _conv_kernel(a_ref, dw_ref, db_ref, lw_ref, lb_ref, o_ref, hs_ref, *, ll):
    hs_ref[0:CONV_PAD, :] = jnp.zeros((CONV_PAD, CONV_WIDTH), F32)
    hs_ref[CONV_PAD + ll:2 * CONV_PAD + ll, :] = jnp.zeros((CONV_PAD, CONV_WIDTH), F32)
    hs_ref[CONV_PAD:CONV_PAD + ll, :] = a_ref[0, :, :CONV_WIDTH] * _sigmoid(a_ref[0, :, CONV_WIDTH:])
    first = CONV_PAD - CONV_K // 2

    def tile(i, c):
        r0 = pl.multiple_of(i * ROW_TILE, ROW_TILE)
        win = hs_ref[pl.ds(r0, ROW_TILE + 2 * CONV_PAD), :]
        acc = jnp.zeros((ROW_TILE, CONV_WIDTH), F32)
        for k in range(CONV_K):
            acc = acc + dw_ref[k:k + 1, :] * win[first + k:first + k + ROW_TILE]
        acc = acc + db_ref[...]
        mu = jnp.mean(acc, axis=-1, keepdims=True)
        dlt = acc - mu
        var = jnp.mean(dlt * dlt, axis=-1, keepdims=True)
        y = dlt * lax.rsqrt(var + EPS) * lw_ref[...] + lb_ref[...]
        o_ref[0, pl.ds(r0, ROW_TILE), :] = _silu(y)
        return c

    lax.fori_loop(0, ll // ROW_TILE, tile, 0)


def _conv(p, dw_w, dw_b, ln_w, ln_b, ll):
    b = p.shape[0]
    row = lambda v: v.reshape(1, CONV_WIDTH)
    vec = pl.BlockSpec((1, CONV_WIDTH), lambda i: (0, 0))
    return pl.pallas_call(
        functools.partial(_conv_kernel, ll=ll),
        grid=(b,),
        in_specs=[pl.BlockSpec((1, ll, 2 * CONV_WIDTH), lambda i: (i, 0, 0)),
                  pl.BlockSpec((CONV_K, CONV_WIDTH), lambda i: (0, 0)), vec, vec, vec],
        out_specs=pl.BlockSpec((1, ll, CONV_WIDTH), lambda i: (i, 0, 0)),
        out_shape=jax.ShapeDtypeStruct((b, ll, CONV_WIDTH), F32),
        scratch_shapes=[pltpu.VMEM((ll + 2 * CONV_PAD, CONV_WIDTH), F32)],
        compiler_params=_cp(("arbitrary",)),
    )(p, dw_w, row(dw_b), row(ln_w), row(ln_b))


def _mla_proj_kernel(cq_ref, ckv_ref, kr_ref, cs_ref, qa_ref, kva_ref, wq_ref, wqs_ref, wk_ref,
                     wv_ref, ek_ref, eks_ref, gq_ref, gqs_ref, gk_ref, gks_ref,
                     q_ref, k_ref, v_ref):
    cos = cs_ref[0]
    sin = cs_ref[1]
    cq = cq_ref[0]
    cqn = (cq * _rms_scale(cq, MLA_Q_RANK) * qa_ref[...]).astype(BF16)
    qx = _dot(cqn, wq_ref[...])
    qs = _dot(cqn, wqs_ref[...])
    ckv = ckv_ref[0]
    ckvn = (ckv * _rms_scale(ckv, MLA_KV_RANK) * kva_ref[...]).astype(BF16)
    kr = kr_ref[0]
    rh = kr.astype(BF16)
    rl = (kr - rh.astype(F32)).astype(BF16)
    kx = _dot(ckvn, wk_ref[...]) + _dot(rh, ek_ref[...]) + _dot(rl, ek_ref[...])
    ks = _dot(rh, eks_ref[...]) + _dot(rl, eks_ref[...])
    v_ref[0] = _dot(ckvn, wv_ref[...]).astype(BF16)
    scale = MLA_QK ** -0.5
    for h in range(MLA_HEADS):
        sl = slice(h * HEAD_PAD, (h + 1) * HEAD_PAD)
        qh = qx[:, sl]
        q_ref[0, :, sl] = (_rms_scale(qh, MLA_QK) * scale
                           * (qh * gq_ref[...] * cos + qs[:, sl] * gqs_ref[...] * sin)).astype(BF16)
        kh = kx[:, sl]
        k_ref[0, :, sl] = (_rms_scale(kh, MLA_QK)
                           * (kh * gk_ref[...] * cos + ks[:, sl] * gks_ref[...] * sin)).astype(BF16)


def _mla_proj(p, cs, qa, kva, wq, wqs, wk, wv, ek, eks, gq, gqs, gk, gks):
    b, s, _ = p.shape
    hw = MLA_HEADS * HEAD_PAD
    full = lambda a: pl.BlockSpec(a.shape, lambda i, j: (0,) * a.ndim)
    out = pl.BlockSpec((1, ROW_TILE, hw), lambda i, j: (i, j, 0))
    consts = (qa, kva, wq, wqs, wk, wv, ek, eks, gq, gqs, gk, gks)
    return pl.pallas_call(
        _mla_proj_kernel,
        grid=(b, s // ROW_TILE),
        in_specs=[pl.BlockSpec((1, ROW_TILE, 512), lambda i, j: (i, j, OD_CQ // 512)),
                  pl.BlockSpec((1, ROW_TILE, MLA_KV_RANK), lambda i, j: (i, j, OD_CKV // MLA_KV_RANK)),
                  pl.BlockSpec((1, ROW_TILE, LANES), lambda i, j: (i, j, OD_KR // LANES)),
                  pl.BlockSpec((2, ROW_TILE, HEAD_PAD), lambda i, j: (0, j, 0))]
                 + [full(a) for a in consts],
        out_specs=[out, out, out],
        out_shape=[jax.ShapeDtypeStruct((b, s, hw), BF16)] * 3,
        compiler_params=_cp(("arbitrary", "arbitrary")),
    )(p, p, p, cs, *consts)


def _attn_kernel(q_ref, k_ref, v_ref, o_ref):
    def scores(h):
        sl = slice(h * HEAD_PAD, (h + 1) * HEAD_PAD)
        return _dot_nt(q_ref[0, :, sl], k_ref[0, :, sl])

    s = scores(0)
    for h in range(MLA_HEADS):
        s_next = scores(h + 1) if h + 1 < MLA_HEADS else None
        sl = slice(h * HEAD_PAD, (h + 1) * HEAD_PAD)
        p = jnp.exp(s - jnp.max(s, axis=-1, keepdims=True))
        den = jnp.sum(p, axis=-1, keepdims=True)
        o_ref[0, :, sl] = (_dot(p.astype(BF16), v_ref[0, :, sl]) / den).astype(BF16)
        s = s_next


def _attention(q, k, v, ll):
    b, s, hw = k.shape
    kv = pl.BlockSpec((1, s, hw), lambda i, j: (i, 0, 0))
    qo = pl.BlockSpec((1, Q_TILE, hw), lambda i, j: (i, j, 0))
    return pl.pallas_call(
        _attn_kernel,
        grid=(b, ll // Q_TILE),
        in_specs=[qo, kv, kv],
        out_specs=qo,
        out_shape=jax.ShapeDtypeStruct((b, ll, hw), BF16),
        compiler_params=_cp(("arbitrary", "arbitrary")),
    )(q, k, v)


def _place(parts, lead):
    out = jnp.zeros(lead + (MLA_HEADS, HEAD_PAD), F32)
    for arr, off in parts:
        out = out.at[..., off:off + arr.shape[-1]].set(arr)
    return out.reshape(lead + (MLA_HEADS * HEAD_PAD,))


def _swap_pairs(a):
    g = a.reshape(a.shape[:-1] + (2, 2, ROPE_AXIS_HALF))
    return jnp.flip(g, axis=-2).reshape(a.shape)


def _rope_tables(rows_total, ll):
    grid_rows = ll // GRID_W
    t_row = jnp.repeat(jnp.arange(grid_rows, dtype=F32), GRID_W)
    t_col = jnp.tile(jnp.arange(GRID_W, dtype=F32), grid_rows)
    inv = 1.0 / (ROPE_BASE ** (jnp.arange(ROPE_AXIS_HALF, dtype=F32) / ROPE_AXIS_HALF))
    ang_r = t_row[:, None] * inv
    ang_c = t_col[:, None] * inv
    cos32 = jnp.concatenate([jnp.cos(ang_r), jnp.cos(ang_r), jnp.cos(ang_c), jnp.cos(ang_c)], axis=-1)
    sin32 = jnp.concatenate([-jnp.sin(ang_r), jnp.sin(ang_r), -jnp.sin(ang_c), jnp.sin(ang_c)], axis=-1)
    extra = rows_total - ll
    cos32 = jnp.concatenate([cos32, jnp.ones((extra, MLA_ROPE), F32)], axis=0)
    sin32 = jnp.concatenate([sin32, jnp.zeros((extra, MLA_ROPE), F32)], axis=0)
    cos_t = jnp.concatenate([jnp.ones((rows_total, MLA_NOPE), F32), cos32,
                             jnp.zeros((rows_total, HEAD_PAD - MLA_QK), F32)], axis=-1)
    sin_t = jnp.concatenate([jnp.zeros((rows_total, MLA_NOPE), F32), sin32,
                             jnp.zeros((rows_total, HEAD_PAD - MLA_QK), F32)], axis=-1)
    return jnp.stack([cos_t, sin_t])


def kernel(x, c, ctx, c_ctx, ada_w, ada_b, ev_w_in, hgrn_lb_logits, hgrn_norm_w, pool_w, pool_scale,
           ev_w_out, od_w_in, conv_dw_w, conv_dw_b, conv_ln_w, conv_ln_b, mla_q_a_norm, mla_w_uq,
           mla_kv_a_norm, mla_w_ukv, mla_q_norm, mla_k_norm, od_w_out, moe_router_w, moe_router_b,
           moe_w1, moe_b1, moe_w2, moe_b2):
    b, ll, d = x.shape
    lc = ctx.shape[1]
    s = ll + lc
    nc, nl = lc // ROW_TILE, ll // ROW_TILE
    assert ada_w.shape[0] == 2 and d == D_MODEL
    assert ll % Q_TILE == 0 and lc % ROW_TILE == 0 and (b * s) % ROUTE_TILE == 0 and (b * ll) % ROUTE_TILE == 0

    cond_rows = (b + 1 + 7) // 8 * 8
    cond = jnp.zeros((cond_rows, d), F32).at[:b].set(c).at[b].set(c_ctx)
    mod = _ada_table(cond, ada_w, ada_b)
    modtab = []
    for layer in range(2):
        lat = mod[layer, :b].reshape(b, 1, 6, d)
        con = jnp.broadcast_to(mod[layer, b].reshape(1, 1, 6, d), (b, 1, 6, d))
        modtab.append(jnp.concatenate([con, lat], axis=1))
    lb_all = jnp.cumsum(jax.nn.softmax(hgrn_lb_logits.astype(F32), axis=0), axis=0)

    def router_pieces(w):
        hi, mid, _ = _split3(w.T)
        return jnp.concatenate([hi, mid], axis=0)

    sel0 = lambda j: jnp.where(j < nc, 0, 1)
    h = (ctx, x)
    p0 = _modproj(h, modtab[0], ev_w_in[0].astype(BF16), sel0, nc)
    hg = _hgrn(p0, lb_all[0], hgrn_norm_w[0], nc, nl)
    yp = _pool(p0, pool_w[0], pool_scale[0], lc, ll)
    wo = ev_w_out[0].astype(BF16)
    h1, m, lg = _mixout(hg, yp, h, modtab[0], wo[:HG_W], wo[HG_W:], router_pieces(moe_router_w[0]),
                        moe_router_b[0].reshape(1, N_EXPERTS), sel0, 0, s, nc)
    to_l1 = lambda j: jnp.where(j < nc, j + nl, j - nc)
    h = _moe(m, lg, h1, modtab[0], sel0, 0, moe_w1, moe_b1, moe_w2, moe_b2, s, to_l1)

    sel1 = lambda j: jnp.where(j < nl, 1, 0)
    w_in = od_w_in[0]
    kr_cols = w_in[:, 1664:1696]
    w1p = jnp.zeros((d, OD_IN_PAD), F32)
    w1p = w1p.at[:, :1024].set(w_in[:, :1024]).at[:, OD_CQ:OD_CQ + MLA_Q_RANK].set(w_in[:, 1024:1408])
    w1p = w1p.at[:, OD_CKV:OD_CKV + MLA_KV_RANK].set(w_in[:, 1408:1664])
    w1p = w1p.at[:, OD_KR:OD_KR + MLA_ROPE].set(kr_cols)
    w1p = w1p.at[:, OD_KR + MLA_ROPE:OD_KR + 2 * MLA_ROPE].set(_swap_pairs(kr_cols))
    p1 = _modproj(h, modtab[1], w1p.astype(BF16), sel1)
    hc = _conv(p1, conv_dw_w[0], conv_dw_b[0], conv_ln_w[0], conv_ln_b[0], ll)

    wuq = mla_w_uq[0].reshape(MLA_Q_RANK, MLA_HEADS, MLA_QK)
    wuq_rope = wuq[..., MLA_NOPE:]
    pad_rows = lambda w: jnp.zeros((512, w.shape[1]), F32).at[:MLA_Q_RANK].set(w)
    wq = pad_rows(_place([(wuq, 0)], (MLA_Q_RANK,))).astype(BF16)
    wqs = pad_rows(_place([(_swap_pairs(wuq_rope), MLA_NOPE)], (MLA_Q_RANK,))).astype(BF16)
    wukv = mla_w_ukv[0].reshape(MLA_KV_RANK, MLA_HEADS, MLA_NOPE + MLA_V)
    wk = _place([(wukv[..., :MLA_NOPE], 0)], (MLA_KV_RANK,)).astype(BF16)
    wv = _place([(wukv[..., MLA_NOPE:], 0)], (MLA_KV_RANK,)).astype(BF16)
    eye = jnp.broadcast_to(jnp.eye(MLA_ROPE, dtype=F32)[:, None, :], (MLA_ROPE, MLA_HEADS, MLA_ROPE))
    zero_rows = jnp.zeros((LANES - 2 * MLA_ROPE, MLA_HEADS * HEAD_PAD), F32)
    place_kr = _place([(eye, MLA_NOPE)], (MLA_ROPE,))
    zero_kr = jnp.zeros_like(place_kr)
    ek = jnp.concatenate([place_kr, zero_kr, zero_rows], axis=0).astype(BF16)
    eks = jnp.concatenate([zero_kr, place_kr, zero_rows], axis=0).astype(BF16)
    lane_gain = lambda g: jnp.concatenate([g, jnp.zeros((HEAD_PAD - MLA_QK,), F32)]).reshape(1, HEAD_PAD)
    swapped_gain = lambda g: jnp.concatenate(
        [jnp.zeros((MLA_NOPE,), F32), _swap_pairs(g[MLA_NOPE:]), jnp.zeros((HEAD_PAD - MLA_QK,), F32)]
    ).reshape(1, HEAD_PAD)
    qa = jnp.concatenate([mla_q_a_norm[0], jnp.zeros((512 - MLA_Q_RANK,), F32)]).reshape(1, 512)
    q, k, v = _mla_proj(p1, _rope_tables(s, ll), qa, mla_kv_a_norm[0].reshape(1, MLA_KV_RANK),
                        wq, wqs, wk, wv, ek, eks,
                        lane_gain(mla_q_norm[0]), swapped_gain(mla_q_norm[0]),
                        lane_gain(mla_k_norm[0]), swapped_gain(mla_k_norm[0]))
    attn = _attention(q, k, v, ll)

    wo1 = od_w_out[0]
    wo_attn = wo1[CONV_WIDTH:].reshape(MLA_HEADS, MLA_V, d)
    wo_attn = jnp.zeros((MLA_HEADS, HEAD_PAD, d), F32).at[:, :MLA_V].set(wo_attn).reshape(-1, d)
    h1, m, lg = _mixout(hc, attn, h, modtab[1], wo1[:CONV_WIDTH].astype(BF16), wo_attn.astype(BF16),
                        router_pieces(moe_router_w[1]), moe_router_b[1].reshape(1, N_EXPERTS),
                        lambda j: 1, 0, ll)
    return _moe(m, lg, h1, modtab[1], lambda j: 1, 1, moe_w1, moe_b1, moe_w2, moe_b2, ll, lambda j: j)
```

```python
import functools

import jax
import jax.numpy as jnp
from jax import lax
from jax.experimental import pallas as pl
from jax.experimental.pallas import tpu as pltpu

F32 = jnp.float32
BF16 = jnp.bfloat16

D_MODEL = 1024
GRID_W = 64
EPS = 1e-6
HG_HEADS = 4
HG_DIM = 128
HG_W = HG_HEADS * HG_DIM
HG_SUB = 32
HG_HEADS_PER_STEP = 2
POOL_WINDOWS = (2, 4, 8, 16)
POOL_GROUP = 128
POOL_PAD = 16
CONV_WIDTH = 512
CONV_K = 31
CONV_PAD = 16
MLA_HEADS = 8
MLA_Q_RANK = 384
MLA_KV_RANK = 256
MLA_NOPE = 64
MLA_ROPE = 32
MLA_V = 64
MLA_QK = MLA_NOPE + MLA_ROPE
HEAD_PAD = 128
ROPE_AXIS_HALF = MLA_ROPE // 4
ROPE_BASE = 10000.0
N_EXPERTS = 32
TOP_K = 4
D_FF = 1024
SWIGLU_LIMIT = 7.0
SWIGLU_ALPHA = 1.702

ROW_TILE = 256
ROUTE_TILE = 512
MOE_TILE = 512
EXPERT_CHUNKS = 4
SPARE_BLOCKS = 4
Q_TILE = 512
LANES = 128
ROW_SPLIT = D_MODEL // LANES
VMEM_LIMIT = 56 * 1024 * 1024

OD_CQ = 1024
OD_CKV = 1536
OD_KR = 1792
OD_IN_PAD = 1920


def _cp(sem, vmem=VMEM_LIMIT):
    return pltpu.CompilerParams(dimension_semantics=sem, vmem_limit_bytes=vmem)


def _dot(a, b):
    return jnp.dot(a, b, preferred_element_type=F32)


def _dot_nt(a, b):
    return lax.dot_general(a, b, (((1,), (1,)), ((), ())), preferred_element_type=F32)


def _dot_tn(a, b):
    return lax.dot_general(a, b, (((0,), (0,)), ((), ())), preferred_element_type=F32)


def _split3(x):
    hi = x.astype(BF16)
    r1 = x - hi.astype(F32)
    mid = r1.astype(BF16)
    lo = (r1 - mid.astype(F32)).astype(BF16)
    return hi, mid, lo


def _sigmoid(x):
    return 1.0 / (1.0 + jnp.exp(-x))


def _silu(x):
    return x * _sigmoid(x)


def _load_row_tiles(ref, row0, rows):
    return jnp.concatenate(
        [ref[pl.ds(row0 * ROW_SPLIT + s, rows, stride=ROW_SPLIT), :] for s in range(ROW_SPLIT)], axis=1)


def _store_row_tiles(ref, row0, x):
    for s in range(ROW_SPLIT):
        ref[pl.ds(row0 * ROW_SPLIT + s, x.shape[0], stride=ROW_SPLIT), :] = x[:, s * LANES:(s + 1) * LANES]


def _rms_scale(x, width):
    return lax.rsqrt(jnp.sum(x * x, axis=-1, keepdims=True) * (1.0 / width) + EPS)


def _ada_kernel(c_ref, w_ref, b_ref, o_ref):
    s = _silu(c_ref[...])
    sh, sm, sl = _split3(s)
    wh, wm, wl = _split3(w_ref[0])
    acc = _dot(sh, wh) + _dot(sh, wm) + _dot(sm, wh) + _dot(sh, wl) + _dot(sl, wh) + _dot(sm, wm)
    o_ref[0] = acc + b_ref[0]


def _ada_table(cond, ada_w, ada_b):
    depth, d, n = ada_w.shape
    rows = cond.shape[0]
    nb = n // d
    return pl.pallas_call(
        _ada_kernel,
        grid=(depth, nb),
        in_specs=[
            pl.BlockSpec((rows, d), lambda l, j: (0, 0)),
            pl.BlockSpec((1, d, d), lambda l, j: (l, 0, j)),
            pl.BlockSpec((1, 1, d), lambda l, j: (l, 0, j)),
        ],
        out_specs=pl.BlockSpec((1, rows, d), lambda l, j: (l, 0, j)),
        out_shape=jax.ShapeDtypeStruct((depth, rows, n), F32),
        compiler_params=_cp(("arbitrary", "arbitrary")),
    )(cond, ada_w, ada_b.reshape(depth, 1, n))


def _token_rows(h_refs, nc):
    if len(h_refs) == 1:
        return h_refs[0][0]
    return jnp.where(pl.program_id(1) < nc, h_refs[0][0], h_refs[1][0])


def _token_specs(h, nc, d, off=0):
    if not isinstance(h, tuple):
        return [pl.BlockSpec((1, ROW_TILE, d), lambda i, j: (i, j + off, 0))]
    return [pl.BlockSpec((1, ROW_TILE, d), lambda i, j: (i, jnp.minimum(j, nc - 1), 0)),
            pl.BlockSpec((1, ROW_TILE, d), lambda i, j: (i, jnp.maximum(j - nc, 0), 0))]


def _modproj_kernel(*refs, nc):
    *h_refs, mod_ref, w_ref, o_ref = refs
    x = _token_rows(h_refs, nc)
    mod = mod_ref[0, 0]
    u = x * _rms_scale(x, x.shape[-1]) * (1.0 + mod[1:2]) + mod[0:1]
    o_ref[0] = _dot(u.astype(BF16), w_ref[...])


def _modproj(h, modtab, w, sel, nc=0):
    parts = h if isinstance(h, tuple) else (h,)
    b, d = parts[0].shape[0], parts[0].shape[2]
    s = sum(p.shape[1] for p in parts)
    n = w.shape[1]
    return pl.pallas_call(
        functools.partial(_modproj_kernel, nc=nc),
        grid=(b, s // ROW_TILE),
        in_specs=_token_specs(h, nc, d) + [
            pl.BlockSpec((1, 1, 6, d), lambda i, j: (i, sel(j), 0, 0)),
            pl.BlockSpec((d, n), lambda i, j: (0, 0)),
        ],
        out_specs=pl.BlockSpec((1, ROW_TILE, n), lambda i, j: (i, j, 0)),
        out_shape=jax.ShapeDtypeStruct((b, s, n), F32),
        compiler_params=_cp(("arbitrary", "arbitrary")),
    )(*parts, modtab, w)


def _hgrn_masks(rows, reverse):
    ri = lax.broadcasted_iota(jnp.int32, (rows, rows), 0)
    ci = lax.broadcasted_iota(jnp.int32, (rows, rows), 1)
    same = (ri // HG_SUB) == (ci // HG_SUB)
    causal = same & ((ci >= ri) if reverse else (ci <= ri))
    tri_f = jnp.where(causal, 1.0, 0.0)
    return tri_f, tri_f.astype(BF16)


def _hgrn_block(q, z, v, lb, st, reverse, masks):
    rows = q.shape[0]
    nsub = rows // HG_SUB
    f = lb + (1.0 - lb) * _sigmoid(z)
    kin = 1.0 - f
    logf = jnp.log(f)
    tri_f, tri = masks
    causal = tri_f > 0.0
    lh, lm, ll = _split3(logf)
    bcum = _dot(tri, lh) + _dot(tri, lm) + _dot(tri, ll)
    b3 = bcum.reshape(nsub, HG_SUB, HG_DIM)
    tot_row = 0 if reverse else HG_SUB - 1
    mid_row = HG_SUB // 2 if reverse else HG_SUB // 2 - 1
    tot = b3[:, tot_row:tot_row + 1, :]
    mid = b3[:, mid_row:mid_row + 1, :]
    q3 = q.reshape(nsub, HG_SUB, HG_DIM)
    k3 = kin.reshape(nsub, HG_SUB, HG_DIM)
    q_dec = (q3 * jnp.exp(b3)).astype(BF16)
    k_end = (k3 * jnp.exp(tot - b3)).astype(BF16)
    q_mid = (q3 * jnp.exp(b3 - mid)).reshape(rows, HG_DIM).astype(BF16)
    k_mid = (k3 * jnp.exp(mid - b3)).reshape(rows, HG_DIM).astype(BF16)
    vb = v.astype(BF16)
    att = jnp.where(causal, _dot_nt(q_mid, k_mid), 0.0).astype(BF16)
    o_intra = _dot(att, vb)
    v3 = vb.reshape(nsub, HG_SUB, HG_DIM)
    dec = jnp.exp(tot)
    upd = [_dot_tn(v3[j], k_end[j]) for j in range(nsub)]
    order = range(nsub - 1, -1, -1) if reverse else range(nsub)
    o_inter = [None] * nsub
    for j in order:
        o_inter[j] = _dot_nt(q_dec[j], st.astype(BF16))
        st = st * dec[j] + upd[j]
    o = o_intra + jnp.concatenate(o_inter, axis=0)
    return o, st


def _hgrn_kernel(q_ref, zf_ref, zb_ref, v_ref, og_ref, lb_ref, nw_ref, o_ref, acc_ref, *, nc, nl):
    blk = ROW_TILE
    nblk = nc + nl
    acc_ref[...] = jnp.zeros_like(acc_ref)
    masks_f = _hgrn_masks(blk, False)
    masks_b = _hgrn_masks(blk, True)
    heads = [slice(k * HG_DIM, (k + 1) * HG_DIM) for k in range(HG_HEADS_PER_STEP)]

    def body(i, carry):
        rf = pl.multiple_of(i * blk, blk)
        ib = jnp.where(i < nc, nc - 1 - i, 2 * nc + nl - 1 - i)
        rb = pl.multiple_of(ib * blk, blk)
        out = []
        for k, sl in enumerate(heads):
            st_f, st_b = carry[2 * k], carry[2 * k + 1]
            qf = _silu(q_ref[0, pl.ds(rf, blk), sl])
            o_f, st_f = _hgrn_block(qf, zf_ref[0, pl.ds(rf, blk), sl], v_ref[0, pl.ds(rf, blk), sl],
                                    lb_ref[0:1, sl], st_f, False, masks_f)
            acc_ref[pl.ds(rf, blk), sl] += o_f
            qb = _silu(q_ref[0, pl.ds(rb, blk), sl])
            o_b, st_b = _hgrn_block(qb, zb_ref[0, pl.ds(rb, blk), sl], v_ref[0, pl.ds(rb, blk), sl],
                                    lb_ref[1:2, sl], st_b, True, masks_b)
            acc_ref[pl.ds(rb, blk), sl] += o_b
            out += [st_f, st_b]
        return tuple(out)

    zero = jnp.zeros((HG_DIM, HG_DIM), F32)
    lax.fori_loop(0, nblk, body, (zero,) * (2 * HG_HEADS_PER_STEP))
    for sl in heads:
        o = acc_ref[:, sl]
        o_ref[0, :, sl] = o * _rms_scale(o, HG_DIM) * nw_ref[...] * _silu(og_ref[0, :, sl])


def _hgrn(p, lb, norm_w, nc, nl):
    b, s, _ = p.shape
    width = HG_HEADS_PER_STEP * HG_DIM
    steps = HG_HEADS // HG_HEADS_PER_STEP
    sec = lambda k: pl.BlockSpec((1, s, width), lambda i, h, k=k: (i, 0, steps * k + h))
    return pl.pallas_call(
        functools.partial(_hgrn_kernel, nc=nc, nl=nl),
        grid=(b, steps),
        in_specs=[sec(0), sec(1), sec(2), sec(3), sec(4),
                  pl.BlockSpec((2, width), lambda i, h: (0, h)),
                  pl.BlockSpec((1, HG_DIM), lambda i, h: (0, 0))],
        out_specs=pl.BlockSpec((1, s, width), lambda i, h: (i, 0, h)),
        out_shape=jax.ShapeDtypeStruct((b, s, HG_W), F32),
        scratch_shapes=[pltpu.VMEM((s, width), F32)],
        compiler_params=_cp(("arbitrary", "arbitrary")),
    )(p, p, p, p, p, lb, norm_w.reshape(1, HG_DIM))


def _pool_kernel(x_ref, w_ref, sc_ref, o_ref, *, lc, ll):
    z = jnp.zeros((POOL_PAD, POOL_GROUP), F32)
    n = lc + ll + 3 * POOL_PAD
    pos_c = lax.broadcasted_iota(jnp.int32, (lc, 1), 0)
    pos_l = lax.broadcasted_iota(jnp.int32, (ll, 1), 0)
    for g, w in enumerate(POOL_WINDOWS):
        x = x_ref[0, :, g * POOL_GROUP:(g + 1) * POOL_GROUP]
        xp = jnp.concatenate([z, x[:lc], z, x[lc:], z], axis=0)
        acc = xp + pltpu.roll(xp, 1, 0)
        half = 1
        while 2 * half < w:
            acc = pltpu.roll(acc, half, 0) + pltpu.roll(acc, n - half, 0)
            half *= 2
        win = jnp.concatenate([acc[POOL_PAD:POOL_PAD + lc],
                               acc[2 * POOL_PAD + lc:2 * POOL_PAD + lc + ll]], axis=0)

        def count(pos, length):
            lo = jnp.maximum(pos - w // 2, 0)
            hi = jnp.minimum(pos + w - w // 2, length)
            return (hi - lo).astype(F32)

        cnt = jnp.concatenate([count(pos_c, lc), count(pos_l, ll)], axis=0)
        d = win / cnt - x
        y = _dot(d.astype(BF16), w_ref[g])
        o_ref[0, :, g * POOL_GROUP:(g + 1) * POOL_GROUP] = y * sc_ref[:, g * POOL_GROUP:(g + 1) * POOL_GROUP]


def _pool(p, pool_w, pool_scale, lc, ll):
    b, s, n = p.shape
    width = len(POOL_WINDOWS) * POOL_GROUP
    return pl.pallas_call(
        functools.partial(_pool_kernel, lc=lc, ll=ll),
        grid=(b,),
        in_specs=[pl.BlockSpec((1, s, width), lambda i: (i, 0, n // width - 1)),
                  pl.BlockSpec(pool_w.shape, lambda i: (0, 0, 0)),
                  pl.BlockSpec((1, width), lambda i: (0, 0))],
        out_specs=pl.BlockSpec((1, s, width), lambda i: (i, 0, 0)),
        out_shape=jax.ShapeDtypeStruct((b, s, width), F32),
        compiler_params=_cp(("arbitrary",)),
    )(p, pool_w.astype(BF16), pool_scale.reshape(1, width))


def _mixout_kernel(a_ref, b_ref, *refs, nc):
    *h_refs, mod_ref, wa_ref, wb_ref, wr_ref, br_ref, h1_ref, m_ref, lg_ref = refs
    mod = mod_ref[0, 0]
    y = _dot(a_ref[0].astype(BF16), wa_ref[...]) + _dot(b_ref[0].astype(BF16), wb_ref[...])
    h1 = _token_rows(h_refs, nc) + mod[2:3] * y
    h1_ref[0] = h1
    m = h1 * _rms_scale(h1, h1.shape[-1]) * (1.0 + mod[4:5]) + mod[3:4]
    _store_row_tiles(m_ref, 0, m)
    mh, mm, _ = _split3(m)
    hi = _dot_nt(mh, wr_ref[...])
    lg = hi[:, :N_EXPERTS] + hi[:, N_EXPERTS:] + _dot_nt(mm, wr_ref[0:N_EXPERTS, :])
    lg_ref[0] = lg + br_ref[...]


def _mixout(a, bmix, h, modtab, wa, wb, wr3, br, sel, h_off, rows, nc=0):
    parts = h if isinstance(h, tuple) else (h,)
    b = parts[0].shape[0]
    d = parts[0].shape[-1]
    ka, kb = a.shape[-1], bmix.shape[-1]
    out3 = lambda n: jax.ShapeDtypeStruct((b, rows, n), F32)
    return pl.pallas_call(
        functools.partial(_mixout_kernel, nc=nc),
        grid=(b, rows // ROW_TILE),
        in_specs=[
            pl.BlockSpec((1, ROW_TILE, ka), lambda i, j: (i, j, 0)),
            pl.BlockSpec((1, ROW_TILE, kb), lambda i, j: (i, j, 0)),
        ] + _token_specs(h, nc, d, h_off) + [
            pl.BlockSpec((1, 1, 6, d), lambda i, j: (i, sel(j), 0, 0)),
            pl.BlockSpec((ka, d), lambda i, j: (0, 0)),
            pl.BlockSpec((kb, d), lambda i, j: (0, 0)),
            pl.BlockSpec((2 * N_EXPERTS, d), lambda i, j: (0, 0)),
            pl.BlockSpec((1, N_EXPERTS), lambda i, j: (0, 0)),
        ],
        out_specs=[
            pl.BlockSpec((1, ROW_TILE, d), lambda i, j: (i, j, 0)),
            pl.BlockSpec((ROW_TILE * ROW_SPLIT, LANES), lambda i, j: (i * (rows // ROW_TILE) + j, 0)),
            pl.BlockSpec((1, ROW_TILE, N_EXPERTS), lambda i, j: (i, j, 0)),
        ],
        out_shape=[out3(d), jax.ShapeDtypeStruct((b * rows * ROW_SPLIT, LANES), F32), out3(N_EXPERTS)],
        compiler_params=_cp(("arbitrary", "arbitrary")),
    )(a, bmix, *parts, modtab, wa, wb, wr3, br)


def _route_kernel(lg_ref, idx_ref, gate_ref, cnt_ref):
    t = lg_ref.shape[0]

    @pl.when(pl.program_id(0) == 0)
    def _():
        cnt_ref[...] = jnp.zeros_like(cnt_ref)

    lg = lg_ref[...]
    lane = lax.broadcasted_iota(jnp.int32, (t, N_EXPERTS), 1).astype(F32)
    hot, vals, idxs = [], [], []
    for _ in range(TOP_K):
        mx = jnp.max(lg, axis=-1, keepdims=True)
        ix = jnp.min(jnp.where(lg == mx, lane, float(N_EXPERTS)), axis=-1, keepdims=True)
        oh = lane == ix
        hot.append(oh)
        vals.append(mx)
        idxs.append(ix.astype(jnp.int32))
        lg = jnp.where(oh, -jnp.inf, lg)
    ex = [jnp.exp(v - vals[0]) for v in vals]
    den = ex[0] + ex[1] + ex[2] + ex[3]
    sel = jnp.where(hot[0] | hot[1] | hot[2] | hot[3], 1.0, 0.0)
    out_lane = lax.broadcasted_iota(jnp.int32, (t, LANES), 1)
    idx_o = jnp.zeros((t, LANES), jnp.int32)
    gate_o = jnp.zeros((t, LANES), F32)
    for k in range(TOP_K):
        idx_o = jnp.where(out_lane == k, idxs[k], idx_o)
        gate_o = jnp.where(out_lane == k, ex[k] / den, gate_o)
    idx_ref[...] = idx_o
    gate_ref[...] = gate_o
    cnt_ref[...] += jnp.sum(sel, axis=0, keepdims=True)


def _route(logits):
    t = logits.shape[0]
    tile = pl.BlockSpec((ROUTE_TILE, LANES), lambda i: (i, 0))
    return pl.pallas_call(
        _route_kernel,
        grid=(t // ROUTE_TILE,),
        in_specs=[pl.BlockSpec((ROUTE_TILE, N_EXPERTS), lambda i: (i, 0))],
        out_specs=[tile, tile, pl.BlockSpec((1, N_EXPERTS), lambda i: (0, 0))],
        out_shape=[jax.ShapeDtypeStruct((t, LANES), jnp.int32),
                   jax.ShapeDtypeStruct((t, LANES), F32),
                   jax.ShapeDtypeStruct((1, N_EXPERTS), F32)],
        compiler_params=_cp(("arbitrary",)),
    )(logits)


def _experts_kernel(be_ref, src0_ref, srcn_ref, dstp_ref, dstc_ref, m_ref, w1_ref, b1_ref, w2_ref,
                    b2_ref, yu_ref, xa, xb, ya, yb, w1b, w2b, gsem, ssem, *, spare):
    g = pl.program_id(0)
    last = be_ref[pl.num_programs(0)] - 1
    tile = xa.shape[0] // ROW_SPLIT
    xbuf = (xa, xb)
    ybuf = (ya, yb)

    def tile_rows(start):
        return pl.ds(pl.multiple_of(start, ROW_SPLIT), ROW_SPLIT)

    rows = tile // EXPERT_CHUNKS

    def gather(idx_ref, r, s):
        return pltpu.make_async_copy(m_ref.at[tile_rows(idx_ref[r])], xbuf[s].at[tile_rows(r * ROW_SPLIT)],
                                     gsem.at[s, r // rows])

    def scatter(row, r, s):
        return pltpu.make_async_copy(ybuf[s].at[tile_rows(r * ROW_SPLIT)], yu_ref.at[tile_rows(row)],
                                     ssem.at[s, r // rows])

    def chunk(c):
        return pl.ds(c * rows * ROW_SPLIT, rows * ROW_SPLIT)

    def wait_gather(s, c):
        pltpu.make_async_copy(m_ref.at[chunk(0)], xbuf[s].at[chunk(c)], gsem.at[s, c]).wait()

    def wait_scatter(s, c):
        pltpu.make_async_copy(ybuf[s].at[chunk(c)], yu_ref.at[chunk(0)], ssem.at[s, c]).wait()

    @pl.when(g == 0)
    def _():
        ya[...] = jnp.zeros_like(ya)
        yb[...] = jnp.zeros_like(yb)

        def first(r, c):
            gather(src0_ref, r, 0).start()
            scatter((spare + 3 * tile + r) * ROW_SPLIT, r, 0).start()
            return c

        lax.fori_loop(0, tile, first, 0)

    changed = (g == 0) | (be_ref[g] != be_ref[jnp.maximum(g - 1, 0)])

    @pl.when(changed)
    def _():
        w1b[...] = w1_ref[0, 0].astype(BF16)
        w2b[...] = w2_ref[0, 0].astype(BF16)

    def block(slot, other):
        for c in range(EXPERT_CHUNKS):
            wait_gather(slot, c)
            wait_scatter(slot, c)
            for r in range(c * rows, (c + 1) * rows):
                gather(srcn_ref, r, other).start(priority=r % 2)
                scatter(jnp.where(g > 0, dstp_ref[r], (spare + 2 * tile + r) * ROW_SPLIT), r,
                        other).start(priority=(r + 1) % 2)
            x = _load_row_tiles(xbuf[slot], c * rows, rows).astype(BF16)
            hb = _dot(x, w1b[...]) + b1_ref[0, 0]
            gl = jnp.minimum(hb[:, :D_FF], SWIGLU_LIMIT)
            lin = jnp.clip(hb[:, D_FF:], -SWIGLU_LIMIT, SWIGLU_LIMIT)
            a = gl * _sigmoid(SWIGLU_ALPHA * gl) * (lin + 1.0)
            _store_row_tiles(ybuf[slot], c * rows, _dot(a.astype(BF16), w2b[...]) + b2_ref[0, 0])

        @pl.when(g == last)
        def _():
            def final(r, c):
                scatter(dstc_ref[r], r, slot).start()
                return c

            lax.fori_loop(0, tile, final, 0)
            for c in range(EXPERT_CHUNKS):
                wait_scatter(other, c)
                wait_scatter(slot, c)
                wait_gather(other, c)

    @pl.when((g % 2 == 1) & (g <= last))
    def _():
        block(1, 0)

    @pl.when((g % 2 == 0) & (g <= last))
    def _():
        block(0, 1)


def _experts(m, blk_e, src_tok, dst_row, layer, w1, b1, w2, b2):
    t, d = m.shape[0] // ROW_SPLIT, D_MODEL
    _, e, _, f2 = w1.shape
    nb = src_tok.shape[0] // MOE_TILE
    spare = TOP_K * t
    idx = lambda fn: pl.BlockSpec((MOE_TILE,), fn, memory_space=pltpu.SMEM)
    grid_spec = pltpu.PrefetchScalarGridSpec(
        num_scalar_prefetch=1,
        grid=(nb,),
        in_specs=[
            idx(lambda i, be: (0,)),
            idx(lambda i, be: (jnp.minimum(i + 1, nb - 1),)),
            idx(lambda i, be: (jnp.maximum(i - 1, 0),)),
            idx(lambda i, be: (i,)),
            pl.BlockSpec(memory_space=pl.ANY),
            pl.BlockSpec((1, 1, d, f2), lambda i, be: (layer, be[i], 0, 0)),
            pl.BlockSpec((1, 1, 1, f2), lambda i, be: (layer, be[i], 0, 0)),
            pl.BlockSpec((1, 1, f2 // 2, d), lambda i, be: (layer, be[i], 0, 0)),
            pl.BlockSpec((1, 1, 1, d), lambda i, be: (layer, be[i], 0, 0)),
        ],
        out_specs=pl.BlockSpec(memory_space=pl.ANY),
        scratch_shapes=[pltpu.VMEM((MOE_TILE * ROW_SPLIT, LANES), F32)] * 4 + [
                        pltpu.VMEM((d, f2), BF16), pltpu.VMEM((f2 // 2, d), BF16),
                        pltpu.SemaphoreType.DMA((2, EXPERT_CHUNKS)), pltpu.SemaphoreType.DMA((2, EXPERT_CHUNKS))],
    )
    return pl.pallas_call(
        functools.partial(_experts_kernel, spare=spare),
        grid_spec=grid_spec,
        out_shape=jax.ShapeDtypeStruct(((spare + SPARE_BLOCKS * MOE_TILE) * ROW_SPLIT, LANES), F32),
        compiler_params=_cp(("arbitrary",)),
    )(blk_e, src_tok, src_tok, dst_row, dst_row, m, w1, b1.reshape(-1, e, 1, f2), w2,
      b2.reshape(-1, e, 1, d))


def _combine_kernel(y0_ref, y1_ref, y2_ref, y3_ref, gate_ref, h_ref, mod_ref, o_ref):
    gate = gate_ref[...]
    rows = h_ref.shape[1]
    f = gate[:, 0:1] * _load_row_tiles(y0_ref, 0, rows)
    for k, y_ref in ((1, y1_ref), (2, y2_ref), (3, y3_ref)):
        f = f + gate[:, k:k + 1] * _load_row_tiles(y_ref, 0, rows)
    o_ref[0] = h_ref[0] + mod_ref[0, 0][5:6] * f


def _combine(yu, gate, h1, modtab, sel, out_rows, out_map):
    b, r, d = h1.shape
    nblk = r // ROW_TILE
    tblk = b * nblk
    ysp = lambda k: pl.BlockSpec((ROW_TILE * ROW_SPLIT, LANES), lambda i, j, k=k: (k * tblk + i * nblk + j, 0))
    return pl.pallas_call(
        _combine_kernel,
        grid=(b, nblk),
        in_specs=[
            ysp(0), ysp(1), ysp(2), ysp(3),
            pl.BlockSpec((ROW_TILE, LANES), lambda i, j: (i * nblk + j, 0)),
            pl.BlockSpec((1, ROW_TILE, d), lambda i, j: (i, j, 0)),
            pl.BlockSpec((1, 1, 6, d), lambda i, j: (i, sel(j), 0, 0)),
        ],
        out_specs=pl.BlockSpec((1, ROW_TILE, d), lambda i, j: (i, out_map(j), 0)),
        out_shape=jax.ShapeDtypeStruct((b, out_rows, d), F32),
        compiler_params=_cp(("arbitrary", "arbitrary")),
    )(yu, yu, yu, yu, gate, h1, modtab)


def _moe(m, logits, h1, modtab, sel, layer, w1, b1, w2, b2, out_rows, out_map):
    b, r, d = h1.shape
    t = b * r
    n_assign = t * TOP_K
    idx, gate, counts = _route(logits.reshape(t, N_EXPERTS))
    counts = counts[0].astype(jnp.int32)
    n_blocks = n_assign // MOE_TILE + N_EXPERTS
    n_slots = n_blocks * MOE_TILE
    padded = (counts + MOE_TILE - 1) // MOE_TILE * MOE_TILE
    pad_end = jnp.cumsum(padded)
    fill_end = jnp.cumsum(padded - counts)
    fill = jnp.arange(n_slots - n_assign, dtype=jnp.int32)
    fill_e = jnp.sum(fill[:, None] >= fill_end[None, :], axis=1).astype(jnp.int32)
    keys = jnp.concatenate([idx[:, :TOP_K].reshape(-1), fill_e])
    order = jnp.argsort(keys, stable=True).astype(jnp.int32)
    real = order < n_assign
    src_tok = jnp.where(real, order // TOP_K, 0)
    spare_row = n_assign + jnp.arange(n_slots, dtype=jnp.int32) % (2 * MOE_TILE)
    dst_row = jnp.where(real, (order % TOP_K) * t + order // TOP_K, spare_row)
    blk_start = jnp.arange(n_blocks, dtype=jnp.int32) * MOE_TILE
    blk_e = jnp.minimum(jnp.sum(blk_start[:, None] >= pad_end[None, :], axis=1), N_EXPERTS - 1).astype(jnp.int32)
    blk_e = jnp.concatenate([blk_e, (pad_end[-1:] // MOE_TILE).astype(jnp.int32)])
    yu = _experts(m, blk_e, src_tok * ROW_SPLIT, dst_row * ROW_SPLIT, layer, w1, b1, w2, b2)
    return _combine(yu, gate, h1, modtab, sel, out_rows, out_map)


def _conv_kernel(a_ref, dw_ref, db_ref, lw_ref, lb_ref, o_ref, hs_ref, *, ll):
    hs_ref[0:CONV_PAD, :] = jnp.zeros((CONV_PAD, CONV_WIDTH), F32)
    hs_ref[CONV_PAD + ll:2 * CONV_PAD + ll, :] = jnp.zeros((CONV_PAD, CONV_WIDTH), F32)
    hs_ref[CONV_PAD:CONV_PAD + ll, :] = a_ref[0, :, :CONV_WIDTH] * _sigmoid(a_ref[0, :, CONV_WIDTH:])
    first = CONV_PAD - CONV_K // 2

    def tile(i, c):
        r0 = pl.multiple_of(i * ROW_TILE, ROW_TILE)
        win = hs_ref[pl.ds(r0, ROW_TILE + 2 * CONV_PAD), :]
        acc = jnp.zeros((ROW_TILE, CONV_WIDTH), F32)
        for k in range(CONV_K):
            acc = acc + dw_ref[k:k + 1, :] * win[first + k:first + k + ROW_TILE]
        acc = acc + db_ref[...]
        mu = jnp.mean(acc, axis=-1, keepdims=True)
        dlt = acc - mu
        var = jnp.mean(dlt * dlt, axis=-1, keepdims=True)
        y = dlt * lax.rsqrt(var + EPS) * lw_ref[...] + lb_ref[...]
        o_ref[0, pl.ds(r0, ROW_TILE), :] = _silu(y)
        return c

    lax.fori_loop(0, ll // ROW_TILE, tile, 0)


def _conv(p, dw_w, dw_b, ln_w, ln_b, ll):
    b = p.shape[0]
    row = lambda v: v.reshape(1, CONV_WIDTH)
    vec = pl.BlockSpec((1, CONV_WIDTH), lambda i: (0, 0))
    return pl.pallas_call(
        functools.partial(_conv_kernel, ll=ll),
        grid=(b,),
        in_specs=[pl.BlockSpec((1, ll, 2 * CONV_WIDTH), lambda i: (i, 0, 0)),
                  pl.BlockSpec((CONV_K, CONV_WIDTH), lambda i: (0, 0)), vec, vec, vec],
        out_specs=pl.BlockSpec((1, ll, CONV_WIDTH), lambda i: (i, 0, 0)),
        out_shape=jax.ShapeDtypeStruct((b, ll, CONV_WIDTH), F32),
        scratch_shapes=[pltpu.VMEM((ll + 2 * CONV_PAD, CONV_WIDTH), F32)],
        compiler_params=_cp(("arbitrary",)),
    )(p, dw_w, row(dw_b), row(ln_w), row(ln_b))


def _mla_proj_kernel(cq_ref, ckv_ref, kr_ref, cs_ref, qa_ref, kva_ref, wq_ref, wqs_ref, wk_ref,
                     wv_ref, ek_ref, eks_ref, gq_ref, gqs_ref, gk_ref, gks_ref,
                     q_ref, k_ref, v_ref):
    cos = cs_ref[0]
    sin = cs_ref[1]
    cq = cq_ref[0]
    cqn = (cq * _rms_scale(cq, MLA_Q_RANK) * qa_ref[...]).astype(BF16)
    qx = _dot(cqn, wq_ref[...])
    qs = _dot(cqn, wqs_ref[...])
    ckv = ckv_ref[0]
    ckvn = (ckv * _rms_scale(ckv, MLA_KV_RANK) * kva_ref[...]).astype(BF16)
    kr = kr_ref[0]
    rh = kr.astype(BF16)
    rl = (kr - rh.astype(F32)).astype(BF16)
    kx = _dot(ckvn, wk_ref[...]) + _dot(rh, ek_ref[...]) + _dot(rl, ek_ref[...])
    ks = _dot(rh, eks_ref[...]) + _dot(rl, eks_ref[...])
    v_ref[0] = _dot(ckvn, wv_ref[...]).astype(BF16)
    scale = MLA_QK ** -0.5
    for h in range(MLA_HEADS):
        sl = slice(h * HEAD_PAD, (h + 1) * HEAD_PAD)
        qh = qx[:, sl]
        q_ref[0, :, sl] = (_rms_scale(qh, MLA_QK) * scale
                           * (qh * gq_ref[...] * cos + qs[:, sl] * gqs_ref[...] * sin)).astype(BF16)
        kh = kx[:, sl]
        k_ref[0, :, sl] = (_rms_scale(kh, MLA_QK)
                           * (kh * gk_ref[...] * cos + ks[:, sl] * gks_ref[...] * sin)).astype(BF16)


def _mla_proj(p, cs, qa, kva, wq, wqs, wk, wv, ek, eks, gq, gqs, gk, gks):
    b, s, _ = p.shape
    hw = MLA_HEADS * HEAD_PAD
    full = lambda a: pl.BlockSpec(a.shape, lambda i, j: (0,) * a.ndim)
    out = pl.BlockSpec((1, ROW_TILE, hw), lambda i, j: (i, j, 0))
    consts = (qa, kva, wq, wqs, wk, wv, ek, eks, gq, gqs, gk, gks)
    return pl.pallas_call(
        _mla_proj_kernel,
        grid=(b, s // ROW_TILE),
        in_specs=[pl.BlockSpec((1, ROW_TILE, 512), lambda i, j: (i, j, OD_CQ // 512)),
                  pl.BlockSpec((1, ROW_TILE, MLA_KV_RANK), lambda i, j: (i, j, OD_CKV // MLA_KV_RANK)),
                  pl.BlockSpec((1, ROW_TILE, LANES), lambda i, j: (i, j, OD_KR // LANES)),
                  pl.BlockSpec((2, ROW_TILE, HEAD_PAD), lambda i, j: (0, j, 0))]
                 + [full(a) for a in consts],
        out_specs=[out, out, out],
        out_shape=[jax.ShapeDtypeStruct((b, s, hw), BF16)] * 3,
        compiler_params=_cp(("arbitrary", "arbitrary")),
    )(p, p, p, cs, *consts)


def _attn_kernel(q_ref, k_ref, v_ref, o_ref):
    def scores(h):
        sl = slice(h * HEAD_PAD, (h + 1) * HEAD_PAD)
        return _dot_nt(q_ref[0, :, sl], k_ref[0, :, sl])

    s = scores(0)
    for h in range(MLA_HEADS):
        s_next = scores(h + 1) if h + 1 < MLA_HEADS else None
        sl = slice(h * HEAD_PAD, (h + 1) * HEAD_PAD)
        p = jnp.exp(s - jnp.max(s, axis=-1, keepdims=True))
        den = jnp.sum(p, axis=-1, keepdims=True)
        o_ref[0, :, sl] = (_dot(p.astype(BF16), v_ref[0, :, sl]) / den).astype(BF16)
        s = s_next


def _attention(q, k, v, ll):
    b, s, hw = k.shape
    kv = pl.BlockSpec((1, s, hw), lambda i, j: (i, 0, 0))
    qo = pl.BlockSpec((1, Q_TILE, hw), lambda i, j: (i, j, 0))
    return pl.pallas_call(
        _attn_kernel,
        grid=(b, ll // Q_TILE),
        in_specs=[qo, kv, kv],
        out_specs=qo,
        out_shape=jax.ShapeDtypeStruct((b, ll, hw), BF16),
        compiler_params=_cp(("arbitrary", "arbitrary")),
    )(q, k, v)


def _place(parts, lead):
    out = jnp.zeros(lead + (MLA_HEADS, HEAD_PAD), F32)
    for arr, off in parts:
        out = out.at[..., off:off + arr.shape[-1]].set(arr)
    return out.reshape(lead + (MLA_HEADS * HEAD_PAD,))


def _swap_pairs(a):
    g = a.reshape(a.shape[:-1] + (2, 2, ROPE_AXIS_HALF))
    return jnp.flip(g, axis=-2).reshape(a.shape)


def _rope_tables(rows_total, ll):
    grid_rows = ll // GRID_W
    t_row = jnp.repeat(jnp.arange(grid_rows, dtype=F32), GRID_W)
    t_col = jnp.tile(jnp.arange(GRID_W, dtype=F32), grid_rows)
    inv = 1.0 / (ROPE_BASE ** (jnp.arange(ROPE_AXIS_HALF, dtype=F32) / ROPE_AXIS_HALF))
    ang_r = t_row[:, None] * inv
    ang_c = t_col[:, None] * inv
    cos32 = jnp.concatenate([jnp.cos(ang_r), jnp.cos(ang_r), jnp.cos(ang_c), jnp.cos(ang_c)], axis=-1)
    sin32 = jnp.concatenate([-jnp.sin(ang_r), jnp.sin(ang_r), -jnp.sin(ang_c), jnp.sin(ang_c)], axis=-1)
    extra = rows_total - ll
    cos32 = jnp.concatenate([cos32, jnp.ones((extra, MLA_ROPE), F32)], axis=0)
    sin32 = jnp.concatenate([sin32, jnp.zeros((extra, MLA_ROPE), F32)], axis=0)
    cos_t = jnp.concatenate([jnp.ones((rows_total, MLA_NOPE), F32), cos32,
                             jnp.zeros((rows_total, HEAD_PAD - MLA_QK), F32)], axis=-1)
    sin_t = jnp.concatenate([jnp.zeros((rows_total, MLA_NOPE), F32), sin32,
                             jnp.zeros((rows_total, HEAD_PAD - MLA_QK), F32)], axis=-1)
    return jnp.stack([cos_t, sin_t])


def kernel(x, c, ctx, c_ctx, ada_w, ada_b, ev_w_in, hgrn_lb_logits, hgrn_norm_w, pool_w, pool_scale,
           ev_w_out, od_w_in, conv_dw_w, conv_dw_b, conv_ln_w, conv_ln_b, mla_q_a_norm, mla_w_uq,
           mla_kv_a_norm, mla_w_ukv, mla_q_norm, mla_k_norm, od_w_out, moe_router_w, moe_router_b,
           moe_w1, moe_b1, moe_w2, moe_b2):
    b, ll, d = x.shape
    lc = ctx.shape[1]
    s = ll + lc
    nc, nl = lc // ROW_TILE, ll // ROW_TILE
    assert ada_w.shape[0] == 2 and d == D_MODEL
    assert ll % Q_TILE == 0 and lc % ROW_TILE == 0 and (b * s) % ROUTE_TILE == 0 and (b * ll) % ROUTE_TILE == 0

    cond_rows = (b + 1 + 7) // 8 * 8
    cond = jnp.zeros((cond_rows, d), F32).at[:b].set(c).at[b].set(c_ctx)
    mod = _ada_table(cond, ada_w, ada_b)
    modtab = []
    for layer in range(2):
        lat = mod[layer, :b].reshape(b, 1, 6, d)
        con = jnp.broadcast_to(mod[layer, b].reshape(1, 1, 6, d), (b, 1, 6, d))
        modtab.append(jnp.concatenate([con, lat], axis=1))
    lb_all = jnp.cumsum(jax.nn.softmax(hgrn_lb_logits.astype(F32), axis=0), axis=0)

    def router_pieces(w):
        hi, mid, _ = _split3(w.T)
        return jnp.concatenate([hi, mid], axis=0)

    sel0 = lambda j: jnp.where(j < nc, 0, 1)
    h = (ctx, x)
    p0 = _modproj(h, modtab[0], ev_w_in[0].astype(BF16), sel0, nc)
    hg = _hgrn(p0, lb_all[0], hgrn_norm_w[0], nc, nl)
    yp = _pool(p0, pool_w[0], pool_scale[0], lc, ll)
    wo = ev_w_out[0].astype(BF16)
    h1, m, lg = _mixout(hg, yp, h, modtab[0], wo[:HG_W], wo[HG_W:], router_pieces(moe_router_w[0]),
                        moe_router_b[0].reshape(1, N_EXPERTS), sel0, 0, s, nc)
    to_l1 = lambda j: jnp.where(j < nc, j + nl, j - nc)
    h = _moe(m, lg, h1, modtab[0], sel0, 0, moe_w1, moe_b1, moe_w2, moe_b2, s, to_l1)

    sel1 = lambda j: jnp.where(j < nl, 1, 0)
    w_in = od_w_in[0]
    kr_cols = w_in[:, 1664:1696]
    w1p = jnp.zeros((d, OD_IN_PAD), F32)
    w1p = w1p.at[:, :1024].set(w_in[:, :1024]).at[:, OD_CQ:OD_CQ + MLA_Q_RANK].set(w_in[:, 1024:1408])
    w1p = w1p.at[:, OD_CKV:OD_CKV + MLA_KV_RANK].set(w_in[:, 1408:1664])
    w1p = w1p.at[:, OD_KR:OD_KR + MLA_ROPE].set(kr_cols)
    w1p = w1p.at[:, OD_KR + MLA_ROPE:OD_KR + 2 * MLA_ROPE].set(_swap_pairs(kr_cols))
    p1 = _modproj(h, modtab[1], w1p.astype(BF16), sel1)
    hc = _conv(p1, conv_dw_w[0], conv_dw_b[0], conv_ln_w[0], conv_ln_b[0], ll)

    wuq = mla_w_uq[0].reshape(MLA_Q_RANK, MLA_HEADS, MLA_QK)
    wuq_rope = wuq[..., MLA_NOPE:]
    pad_rows = lambda w: jnp.zeros((512, w.shape[1]), F32).at[:MLA_Q_RANK].set(w)
    wq = pad_rows(_place([(wuq, 0)], (MLA_Q_RANK,))).astype(BF16)
    wqs = pad_rows(_place([(_swap_pairs(wuq_rope), MLA_NOPE)], (MLA_Q_RANK,))).astype(BF16)
    wukv = mla_w_ukv[0].reshape(MLA_KV_RANK, MLA_HEADS, MLA_NOPE + MLA_V)
    wk = _place([(wukv[..., :MLA_NOPE], 0)], (MLA_KV_RANK,)).astype(BF16)
    wv = _place([(wukv[..., MLA_NOPE:], 0)], (MLA_KV_RANK,)).astype(BF16)
    eye = jnp.broadcast_to(jnp.eye(MLA_ROPE, dtype=F32)[:, None, :], (MLA_ROPE, MLA_HEADS, MLA_ROPE))
    zero_rows = jnp.zeros((LANES - 2 * MLA_ROPE, MLA_HEADS * HEAD_PAD), F32)
    place_kr = _place([(eye, MLA_NOPE)], (MLA_ROPE,))
    zero_kr = jnp.zeros_like(place_kr)
    ek = jnp.concatenate([place_kr, zero_kr, zero_rows], axis=0).astype(BF16)
    eks = jnp.concatenate([zero_kr, place_kr, zero_rows], axis=0).astype(BF16)
    lane_gain = lambda g: jnp.concatenate([g, jnp.zeros((HEAD_PAD - MLA_QK,), F32)]).reshape(1, HEAD_PAD)
    swapped_gain = lambda g: jnp.concatenate(
        [jnp.zeros((MLA_NOPE,), F32), _swap_pairs(g[MLA_NOPE:]), jnp.zeros((HEAD_PAD - MLA_QK,), F32)]
    ).reshape(1, HEAD_PAD)
    qa = jnp.concatenate([mla_q_a_norm[0], jnp.zeros((512 - MLA_Q_RANK,), F32)]).reshape(1, 512)
    q, k, v = _mla_proj(p1, _rope_tables(s, ll), qa, mla_kv_a_norm[0].reshape(1, MLA_KV_RANK),
                        wq, wqs, wk, wv, ek, eks,
                        lane_gain(mla_q_norm[0]), swapped_gain(mla_q_norm[0]),
                        lane_gain(mla_k_norm[0]), swapped_gain(mla_k_norm[0]))
    attn = _attention(q, k, v, ll)

    wo1 = od_w_out[0]
    wo_attn = wo1[CONV_WIDTH:].reshape(MLA_HEADS, MLA_V, d)
    wo_attn = jnp.zeros((MLA_HEADS, HEAD_PAD, d), F32).at[:, :MLA_V].set(wo_attn).reshape(-1, d)
    h1, m, lg = _mixout(hc, attn, h, modtab[1], wo1[:CONV_WIDTH].astype(BF16), wo_attn.astype(BF16),
                        router_pieces(moe_router_w[1]), moe_router_b[1].reshape(1, N_EXPERTS),
                        lambda j: 1, 0, ll)
    return _moe(m, lg, h1, modtab[1], lambda j: 1, 1, moe_w1, moe_b1, moe_w2, moe_b2, ll, lambda j: j)
```

```python
import functools

import jax
import jax.numpy as jnp
from jax import lax
from jax.experimental import pallas as pl
from jax.experimental.pallas import tpu as pltpu

F32 = jnp.float32
BF16 = jnp.bfloat16

D_MODEL = 1024
GRID_W = 64
EPS = 1e-6
HG_HEADS = 4
HG_DIM = 128
HG_W = HG_HEADS * HG_DIM
HG_SUB = 32
HG_HEADS_PER_STEP = 2
POOL_WINDOWS = (2, 4, 8, 16)
POOL_GROUP = 128
POOL_PAD = 16
CONV_WIDTH = 512
CONV_K = 31
CONV_PAD = 16
MLA_HEADS = 8
MLA_Q_RANK = 384
MLA_KV_RANK = 256
MLA_NOPE = 64
MLA_ROPE = 32
MLA_V = 64
MLA_QK = MLA_NOPE + MLA_ROPE
HEAD_PAD = 128
ROPE_AXIS_HALF = MLA_ROPE // 4
ROPE_BASE = 10000.0
N_EXPERTS = 32
TOP_K = 4
D_FF = 1024
SWIGLU_LIMIT = 7.0
SWIGLU_ALPHA = 1.702

ROW_TILE = 256
ROUTE_TILE = 512
MOE_TILE = 512
EXPERT_CHUNKS = 4
SPARE_BLOCKS = 4
Q_TILE = 512
LANES = 128
ROW_SPLIT = D_MODEL // LANES
VMEM_LIMIT = 56 * 1024 * 1024

OD_CQ = 1024
OD_CKV = 1536
OD_KR = 1792
OD_IN_PAD = 1920


def _cp(sem, vmem=VMEM_LIMIT):
    return pltpu.CompilerParams(dimension_semantics=sem, vmem_limit_bytes=vmem)


def _dot(a, b):
    return jnp.dot(a, b, preferred_element_type=F32)


def _dot_nt(a, b):
    return lax.dot_general(a, b, (((1,), (1,)), ((), ())), preferred_element_type=F32)


def _dot_tn(a, b):
    return lax.dot_general(a, b, (((0,), (0,)), ((), ())), preferred_element_type=F32)


def _split3(x):
    hi = x.astype(BF16)
    r1 = x - hi.astype(F32)
    mid = r1.astype(BF16)
    lo = (r1 - mid.astype(F32)).astype(BF16)
    return hi, mid, lo


def _sigmoid(x):
    return 1.0 / (1.0 + jnp.exp(-x))


def _silu(x):
    return x * _sigmoid(x)


def _load_row_tiles(ref, row0, rows):
    return jnp.concatenate(
        [ref[pl.ds(row0 * ROW_SPLIT + s, rows, stride=ROW_SPLIT), :] for s in range(ROW_SPLIT)], axis=1)


def _store_row_tiles(ref, row0, x):
    for s in range(ROW_SPLIT):
        ref[pl.ds(row0 * ROW_SPLIT + s, x.shape[0], stride=ROW_SPLIT), :] = x[:, s * LANES:(s + 1) * LANES]


def _rms_scale(x, width):
    return lax.rsqrt(jnp.sum(x * x, axis=-1, keepdims=True) * (1.0 / width) + EPS)


def _ada_kernel(c_ref, w_ref, b_ref, o_ref):
    s = _silu(c_ref[...])
    sh, sm, sl = _split3(s)
    wh, wm, wl = _split3(w_ref[0])
    acc = _dot(sh, wh) + _dot(sh, wm) + _dot(sm, wh) + _dot(sh, wl) + _dot(sl, wh) + _dot(sm, wm)
    o_ref[0] = acc + b_ref[0]


def _ada_table(cond, ada_w, ada_b):
    depth, d, n = ada_w.shape
    rows = cond.shape[0]
    nb = n // d
    return pl.pallas_call(
        _ada_kernel,
        grid=(depth, nb),
        in_specs=[
            pl.BlockSpec((rows, d), lambda l, j: (0, 0)),
            pl.BlockSpec((1, d, d), lambda l, j: (l, 0, j)),
            pl.BlockSpec((1, 1, d), lambda l, j: (l, 0, j)),
        ],
        out_specs=pl.BlockSpec((1, rows, d), lambda l, j: (l, 0, j)),
        out_shape=jax.ShapeDtypeStruct((depth, rows, n), F32),
        compiler_params=_cp(("arbitrary", "arbitrary")),
    )(cond, ada_w, ada_b.reshape(depth, 1, n))


def _token_rows(h_refs, nc):
    if len(h_refs) == 1:
        return h_refs[0][0]
    return jnp.where(pl.program_id(1) < nc, h_refs[0][0], h_refs[1][0])


def _token_specs(h, nc, d, off=0):
    if not isinstance(h, tuple):
        return [pl.BlockSpec((1, ROW_TILE, d), lambda i, j: (i, j + off, 0))]
    return [pl.BlockSpec((1, ROW_TILE, d), lambda i, j: (i, jnp.minimum(j, nc - 1), 0)),
            pl.BlockSpec((1, ROW_TILE, d), lambda i, j: (i, jnp.maximum(j - nc, 0), 0))]


def _modproj_kernel(*refs, nc):
    *h_refs, mod_ref, w_ref, o_ref = refs
    x = _token_rows(h_refs, nc)
    mod = mod_ref[0, 0]
    u = x * _rms_scale(x, x.shape[-1]) * (1.0 + mod[1:2]) + mod[0:1]
    o_ref[0] = _dot(u.astype(BF16), w_ref[...])


def _modproj(h, modtab, w, sel, nc=0):
    parts = h if isinstance(h, tuple) else (h,)
    b, d = parts[0].shape[0], parts[0].shape[2]
    s = sum(p.shape[1] for p in parts)
    n = w.shape[1]
    return pl.pallas_call(
        functools.partial(_modproj_kernel, nc=nc),
        grid=(b, s // ROW_TILE),
        in_specs=_token_specs(h, nc, d) + [
            pl.BlockSpec((1, 1, 6, d), lambda i, j: (i, sel(j), 0, 0)),
            pl.BlockSpec((d, n), lambda i, j: (0, 0)),
        ],
        out_specs=pl.BlockSpec((1, ROW_TILE, n), lambda i, j: (i, j, 0)),
        out_shape=jax.ShapeDtypeStruct((b, s, n), F32),
        compiler_params=_cp(("arbitrary", "arbitrary")),
    )(*parts, modtab, w)


def _hgrn_masks(rows, reverse):
    ri = lax.broadcasted_iota(jnp.int32, (rows, rows), 0)
    ci = lax.broadcasted_iota(jnp.int32, (rows, rows), 1)
    same = (ri // HG_SUB) == (ci // HG_SUB)
    causal = same & ((ci >= ri) if reverse else (ci <= ri))
    tri_f = jnp.where(causal, 1.0, 0.0)
    return tri_f, tri_f.astype(BF16)


def _hgrn_block(q, z, v, lb, st, reverse, masks):
    rows = q.shape[0]
    nsub = rows // HG_SUB
    f = lb + (1.0 - lb) * _sigmoid(z)
    kin = 1.0 - f
    logf = jnp.log(f)
    tri_f, tri = masks
    causal = tri_f > 0.0
    lh, lm, ll = _split3(logf)
    bcum = _dot(tri, lh) + _dot(tri, lm) + _dot(tri, ll)
    b3 = bcum.reshape(nsub, HG_SUB, HG_DIM)
    tot_row = 0 if reverse else HG_SUB - 1
    mid_row = HG_SUB // 2 if reverse else HG_SUB // 2 - 1
    tot = b3[:, tot_row:tot_row + 1, :]
    mid = b3[:, mid_row:mid_row + 1, :]
    q3 = q.reshape(nsub, HG_SUB, HG_DIM)
    k3 = kin.reshape(nsub, HG_SUB, HG_DIM)
    q_dec = (q3 * jnp.exp(b3)).astype(BF16)
    k_end = (k3 * jnp.exp(tot - b3)).astype(BF16)
    q_mid = (q3 * jnp.exp(b3 - mid)).reshape(rows, HG_DIM).astype(BF16)
    k_mid = (k3 * jnp.exp(mid - b3)).reshape(rows, HG_DIM).astype(BF16)
    vb = v.astype(BF16)
    att = jnp.where(causal, _dot_nt(q_mid, k_mid), 0.0).astype(BF16)
    o_intra = _dot(att, vb)
    v3 = vb.reshape(nsub, HG_SUB, HG_DIM)
    dec = jnp.exp(tot)
    upd = [_dot_tn(v3[j], k_end[j]) for j in range(nsub)]
    order = range(nsub - 1, -1, -1) if reverse else range(nsub)
    o_inter = [None] * nsub
    for j in order:
        o_inter[j] = _dot_nt(q_dec[j], st.astype(BF16))
        st = st * dec[j] + upd[j]
    o = o_intra + jnp.concatenate(o_inter, axis=0)
    return o, st


def _hgrn_kernel(q_ref, zf_ref, zb_ref, v_ref, og_ref, lb_ref, nw_ref, o_ref, acc_ref, *, nc, nl):
    blk = ROW_TILE
    nblk = nc + nl
    acc_ref[...] = jnp.zeros_like(acc_ref)
    masks_f = _hgrn_masks(blk, False)
    masks_b = _hgrn_masks(blk, True)
    heads = [slice(k * HG_DIM, (k + 1) * HG_DIM) for k in range(HG_HEADS_PER_STEP)]

    def body(i, carry):
        rf = pl.multiple_of(i * blk, blk)
        ib = jnp.where(i < nc, nc - 1 - i, 2 * nc + nl - 1 - i)
        rb = pl.multiple_of(ib * blk, blk)
        out = []
        for k, sl in enumerate(heads):
            st_f, st_b = carry[2 * k], carry[2 * k + 1]
            qf = _silu(q_ref[0, pl.ds(rf, blk), sl])
            o_f, st_f = _hgrn_block(qf, zf_ref[0, pl.ds(rf, blk), sl], v_ref[0, pl.ds(rf, blk), sl],
                                    lb_ref[0:1, sl], st_f, False, masks_f)
            acc_ref[pl.ds(rf, blk), sl] += o_f
            qb = _silu(q_ref[0, pl.ds(rb, blk), sl])
            o_b, st_b = _hgrn_block(qb, zb_ref[0, pl.ds(rb, blk), sl], v_ref[0, pl.ds(rb, blk), sl],
                                    lb_ref[1:2, sl], st_b, True, masks_b)
            acc_ref[pl.ds(rb, blk), sl] += o_b
            out += [st_f, st_b]
        return tuple(out)

    zero = jnp.zeros((HG_DIM, HG_DIM), F32)
    lax.fori_loop(0, nblk, body, (zero,) * (2 * HG_HEADS_PER_STEP))
    for sl in heads:
        o = acc_ref[:, sl]
        o_ref[0, :, sl] = o * _rms_scale(o, HG_DIM) * nw_ref[...] * _silu(og_ref[0, :, sl])


def _hgrn(p, lb, norm_w, nc, nl):
    b, s, _ = p.shape
    width = HG_HEADS_PER_STEP * HG_DIM
    steps = HG_HEADS // HG_HEADS_PER_STEP
    sec = lambda k: pl.BlockSpec((1, s, width), lambda i, h, k=k: (i, 0, steps * k + h))
    return pl.pallas_call(
        functools.partial(_hgrn_kernel, nc=nc, nl=nl),
        grid=(b, steps),
        in_specs=[sec(0), sec(1), sec(2), sec(3), sec(4),
                  pl.BlockSpec((2, width), lambda i, h: (0, h)),
                  pl.BlockSpec((1, HG_DIM), lambda i, h: (0, 0))],
        out_specs=pl.BlockSpec((1, s, width), lambda i, h: (i, 0, h)),
        out_shape=jax.ShapeDtypeStruct((b, s, HG_W), F32),
        scratch_shapes=[pltpu.VMEM((s, width), F32)],
        compiler_params=_cp(("arbitrary", "arbitrary")),
    )(p, p, p, p, p, lb, norm_w.reshape(1, HG_DIM))


def _pool_kernel(x_ref, w_ref, sc_ref, o_ref, *, lc, ll):
    z = jnp.zeros((POOL_PAD, POOL_GROUP), F32)
    n = lc + ll + 3 * POOL_PAD
    pos_c = lax.broadcasted_iota(jnp.int32, (lc, 1), 0)
    pos_l = lax.broadcasted_iota(jnp.int32, (ll, 1), 0)
    for g, w in enumerate(POOL_WINDOWS):
        x = x_ref[0, :, g * POOL_GROUP:(g + 1) * POOL_GROUP]
        xp = jnp.concatenate([z, x[:lc], z, x[lc:], z], axis=0)
        acc = xp + pltpu.roll(xp, 1, 0)
        half = 1
        while 2 * half < w:
            acc = pltpu.roll(acc, half, 0) + pltpu.roll(acc, n - half, 0)
            half *= 2
        win = jnp.concatenate([acc[POOL_PAD:POOL_PAD + lc],
                               acc[2 * POOL_PAD + lc:2 * POOL_PAD + lc + ll]], axis=0)

        def count(pos, length):
            lo = jnp.maximum(pos - w // 2, 0)
            hi = jnp.minimum(pos + w - w // 2, length)
            return (hi - lo).astype(F32)

        cnt = jnp.concatenate([count(pos_c, lc), count(pos_l, ll)], axis=0)
        d = win / cnt - x
        y = _dot(d.astype(BF16), w_ref[g])
        o_ref[0, :, g * POOL_GROUP:(g + 1) * POOL_GROUP] = y * sc_ref[:, g * POOL_GROUP:(g + 1) * POOL_GROUP]


def _pool(p, pool_w, pool_scale, lc, ll):
    b, s, n = p.shape
    width = len(POOL_WINDOWS) * POOL_GROUP
    return pl.pallas_call(
        functools.partial(_pool_kernel, lc=lc, ll=ll),
        grid=(b,),
        in_specs=[pl.BlockSpec((1, s, width), lambda i: (i, 0, n // width - 1)),
                  pl.BlockSpec(pool_w.shape, lambda i: (0, 0, 0)),
                  pl.BlockSpec((1, width), lambda i: (0, 0))],
        out_specs=pl.BlockSpec((1, s, width), lambda i: (i, 0, 0)),
        out_shape=jax.ShapeDtypeStruct((b, s, width), F32),
        compiler_params=_cp(("arbitrary",)),
    )(p, pool_w.astype(BF16), pool_scale.reshape(1, width))


def _mixout_kernel(a_ref, b_ref, *refs, nc):
    *h_refs, mod_ref, wa_ref, wb_ref, wr_ref, br_ref, h1_ref, m_ref, lg_ref = refs
    mod = mod_ref[0, 0]
    y = _dot(a_ref[0].astype(BF16), wa_ref[...]) + _dot(b_ref[0].astype(BF16), wb_ref[...])
    h1 = _token_rows(h_refs, nc) + mod[2:3] * y
    h1_ref[0] = h1
    m = h1 * _rms_scale(h1, h1.shape[-1]) * (1.0 + mod[4:5]) + mod[3:4]
    _store_row_tiles(m_ref, 0, m)
    mh, mm, _ = _split3(m)
    hi = _dot_nt(mh, wr_ref[...])
    lg = hi[:, :N_EXPERTS] + hi[:, N_EXPERTS:] + _dot_nt(mm, wr_ref[0:N_EXPERTS, :])
    lg_ref[0] = lg + br_ref[...]


def _mixout(a, bmix, h, modtab, wa, wb, wr3, br, sel, h_off, rows, nc=0):
    parts = h if isinstance(h, tuple) else (h,)
    b = parts[0].shape[0]
    d = parts[0].shape[-1]
    ka, kb = a.shape[-1], bmix.shape[-1]
    out3 = lambda n: jax.ShapeDtypeStruct((b, rows, n), F32)
    return pl.pallas_call(
        functools.partial(_mixout_kernel, nc=nc),
        grid=(b, rows // ROW_TILE),
        in_specs=[
            pl.BlockSpec((1, ROW_TILE, ka), lambda i, j: (i, j, 0)),
            pl.BlockSpec((1, ROW_TILE, kb), lambda i, j: (i, j, 0)),
        ] + _token_specs(h, nc, d, h_off) + [
            pl.BlockSpec((1, 1, 6, d), lambda i, j: (i, sel(j), 0, 0)),
            pl.BlockSpec((ka, d), lambda i, j: (0, 0)),
            pl.BlockSpec((kb, d), lambda i, j: (0, 0)),
            pl.BlockSpec((2 * N_EXPERTS, d), lambda i, j: (0, 0)),
            pl.BlockSpec((1, N_EXPERTS), lambda i, j: (0, 0)),
        ],
        out_specs=[
            pl.BlockSpec((1, ROW_TILE, d), lambda i, j: (i, j, 0)),
            pl.BlockSpec((ROW_TILE * ROW_SPLIT, LANES), lambda i, j: (i * (rows // ROW_TILE) + j, 0)),
            pl.BlockSpec((1, ROW_TILE, N_EXPERTS), lambda i, j: (i, j, 0)),
        ],
        out_shape=[out3(d), jax.ShapeDtypeStruct((b * rows * ROW_SPLIT, LANES), F32), out3(N_EXPERTS)],
        compiler_params=_cp(("arbitrary", "arbitrary")),
    )(a, bmix, *parts, modtab, wa, wb, wr3, br)


def _route_kernel(lg_ref, idx_ref, gate_ref, cnt_ref):
    t = lg_ref.shape[0]

    @pl.when(pl.program_id(0) == 0)
    def _():
        cnt_ref[...] = jnp.zeros_like(cnt_ref)

    lg = lg_ref[...]
    lane = lax.broadcasted_iota(jnp.int32, (t, N_EXPERTS), 1).astype(F32)
    hot, vals, idxs = [], [], []
    for _ in range(TOP_K):
        mx = jnp.max(lg, axis=-1, keepdims=True)
        ix = jnp.min(jnp.where(lg == mx, lane, float(N_EXPERTS)), axis=-1, keepdims=True)
        oh = lane == ix
        hot.append(oh)
        vals.append(mx)
        idxs.append(ix.astype(jnp.int32))
        lg = jnp.where(oh, -jnp.inf, lg)
    ex = [jnp.exp(v - vals[0]) for v in vals]
    den = ex[0] + ex[1] + ex[2] + ex[3]
    sel = jnp.where(hot[0] | hot[1] | hot[2] | hot[3], 1.0, 0.0)
    out_lane = lax.broadcasted_iota(jnp.int32, (t, LANES), 1)
    idx_o = jnp.zeros((t, LANES), jnp.int32)
    gate_o = jnp.zeros((t, LANES), F32)
    for k in range(TOP_K):
        idx_o = jnp.where(out_lane == k, idxs[k], idx_o)
        gate_o = jnp.where(out_lane == k, ex[k] / den, gate_o)
    idx_ref[...] = idx_o
    gate_ref[...] = gate_o
    cnt_ref[...] += jnp.sum(sel, axis=0, keepdims=True)


def _route(logits):
    t = logits.shape[0]
    tile = pl.BlockSpec((ROUTE_TILE, LANES), lambda i: (i, 0))
    return pl.pallas_call(
        _route_kernel,
        grid=(t // ROUTE_TILE,),
        in_specs=[pl.BlockSpec((ROUTE_TILE, N_EXPERTS), lambda i: (i, 0))],
        out_specs=[tile, tile, pl.BlockSpec((1, N_EXPERTS), lambda i: (0, 0))],
        out_shape=[jax.ShapeDtypeStruct((t, LANES), jnp.int32),
                   jax.ShapeDtypeStruct((t, LANES), F32),
                   jax.ShapeDtypeStruct((1, N_EXPERTS), F32)],
        compiler_params=_cp(("arbitrary",)),
    )(logits)


def _experts_kernel(be_ref, src0_ref, srcn_ref, dstp_ref, dstc_ref, m_ref, w1_ref, b1_ref, w2_ref,
                    b2_ref, yu_ref, xa, xb, ya, yb, w1b, w2b, gsem, ssem, *, spare):
    g = pl.program_id(0)
    last = be_ref[pl.num_programs(0)] - 1
    tile = xa.shape[0] // ROW_SPLIT
    xbuf = (xa, xb)
    ybuf = (ya, yb)

    def tile_rows(start):
        return pl.ds(pl.multiple_of(start, ROW_SPLIT), ROW_SPLIT)

    rows = tile // EXPERT_CHUNKS

    def gather(idx_ref, r, s):
        return pltpu.make_async_copy(m_ref.at[tile_rows(idx_ref[r])], xbuf[s].at[tile_rows(r * ROW_SPLIT)],
                                     gsem.at[s, r // rows])

    def scatter(row, r, s):
        return pltpu.make_async_copy(ybuf[s].at[tile_rows(r * ROW_SPLIT)], yu_ref.at[tile_rows(row)],
                                     ssem.at[s, r // rows])

    def chunk(c):
        return pl.ds(c * rows * ROW_SPLIT, rows * ROW_SPLIT)

    def wait_gather(s, c):
        pltpu.make_async_copy(m_ref.at[chunk(0)], xbuf[s].at[chunk(c)], gsem.at[s, c]).wait()

    def wait_scatter(s, c):
        pltpu.make_async_copy(ybuf[s].at[chunk(c)], yu_ref.at[chunk(0)], ssem.at[s, c]).wait()

    @pl.when(g == 0)
    def _():
        ya[...] = jnp.zeros_like(ya)
        yb[...] = jnp.zeros_like(yb)

        def first(r, c):
            gather(src0_ref, r, 0).start()
            scatter((spare + 3 * tile + r) * ROW_SPLIT, r, 0).start()
            return c

        lax.fori_loop(0, tile, first, 0)

    changed = (g == 0) | (be_ref[g] != be_ref[jnp.maximum(g - 1, 0)])

    @pl.when(changed)
    def _():
        w1b[...] = w1_ref[0, 0].astype(BF16)
        w2b[...] = w2_ref[0, 0].astype(BF16)

    def block(slot, other):
        for c in range(EXPERT_CHUNKS):
            wait_gather(slot, c)
            wait_scatter(slot, c)
            for r in range(c * rows, (c + 1) * rows):
                gather(srcn_ref, r, other).start(priority=r % 2)
                scatter(jnp.where(g > 0, dstp_ref[r], (spare + 2 * tile + r) * ROW_SPLIT), r,
                        other).start(priority=(r + 1) % 2)
            x = _load_row_tiles(xbuf[slot], c * rows, rows).astype(BF16)
            hb = _dot(x, w1b[...]) + b1_ref[0, 0]
            gl = jnp.minimum(hb[:, :D_FF], SWIGLU_LIMIT)
            lin = jnp.clip(hb[:, D_FF:], -SWIGLU_LIMIT, SWIGLU_LIMIT)
            a = gl * _sigmoid(SWIGLU_ALPHA * gl) * (lin + 1.0)
            _store_row_tiles(ybuf[slot], c * rows, _dot(a.astype(BF16), w2b[...]) + b2_ref[0, 0])

        @pl.when(g == last)
        def _():
            def final(r, c):
                scatter(dstc_ref[r], r, slot).start()
                return c

            lax.fori_loop(0, tile, final, 0)
            for c in range(EXPERT_CHUNKS):
                wait_scatter(other, c)
                wait_scatter(slot, c)
                wait_gather(other, c)

    @pl.when((g % 2 == 1) & (g <= last))
    def _():
        block(1, 0)

    @pl.when((g % 2 == 0) & (g <= last))
    def _():
        block(0, 1)


def _experts(m, blk_e, src_tok, dst_row, layer, w1, b1, w2, b2):
    t, d = m.shape[0] // ROW_SPLIT, D_MODEL
    _, e, _, f2 = w1.shape
    nb = src_tok.shape[0] // MOE_TILE
    spare = TOP_K * t
    idx = lambda fn: pl.BlockSpec((MOE_TILE,), fn, memory_space=pltpu.SMEM)
    grid_spec = pltpu.PrefetchScalarGridSpec(
        num_scalar_prefetch=1,
        grid=(nb,),
        in_specs=[
            idx(lambda i, be: (0,)),
            idx(lambda i, be: (jnp.minimum(i + 1, nb - 1),)),
            idx(lambda i, be: (jnp.maximum(i - 1, 0),)),
            idx(lambda i, be: (i,)),
            pl.BlockSpec(memory_space=pl.ANY),
            pl.BlockSpec((1, 1, d, f2), lambda i, be: (layer, be[i], 0, 0)),
            pl.BlockSpec((1, 1, 1, f2), lambda i, be: (layer, be[i], 0, 0)),
            pl.BlockSpec((1, 1, f2 // 2, d), lambda i, be: (layer, be[i], 0, 0)),
            pl.BlockSpec((1, 1, 1, d), lambda i, be: (layer, be[i], 0, 0)),
        ],
        out_specs=pl.BlockSpec(memory_space=pl.ANY),
        scratch_shapes=[pltpu.VMEM((MOE_TILE * ROW_SPLIT, LANES), F32)] * 4 + [
                        pltpu.VMEM((d, f2), BF16), pltpu.VMEM((f2 // 2, d), BF16),
                        pltpu.SemaphoreType.DMA((2, EXPERT_CHUNKS)), pltpu.SemaphoreType.DMA((2, EXPERT_CHUNKS))],
    )
    return pl.pallas_call(
        functools.partial(_experts_kernel, spare=spare),
        grid_spec=grid_spec,
        out_shape=jax.ShapeDtypeStruct(((spare + SPARE_BLOCKS * MOE_TILE) * ROW_SPLIT, LANES), F32),
        compiler_params=_cp(("arbitrary",)),
    )(blk_e, src_tok, src_tok, dst_row, dst_row, m, w1, b1.reshape(-1, e, 1, f2), w2,
      b2.reshape(-1, e, 1, d))


def _combine_kernel(y0_ref, y1_ref, y2_ref, y3_ref, gate_ref, h_ref, mod_ref, o_ref):
    gate = gate_ref[...]
    rows = h_ref.shape[1]
    f = gate[:, 0:1] * _load_row_tiles(y0_ref, 0, rows)
    for k, y_ref in ((1, y1_ref), (2, y2_ref), (3, y3_ref)):
        f = f + gate[:, k:k + 1] * _load_row_tiles(y_ref, 0, rows)
    o_ref[0] = h_ref[0] + mod_ref[0, 0][5:6] * f


def _combine(yu, gate, h1, modtab, sel, out_rows, out_map):
    b, r, d = h1.shape
    nblk = r // ROW_TILE
    tblk = b * nblk
    ysp = lambda k: pl.BlockSpec((ROW_TILE * ROW_SPLIT, LANES), lambda i, j, k=k: (k * tblk + i * nblk + j, 0))
    return pl.pallas_call(
        _combine_kernel,
        grid=(b, nblk),
        in_specs=[
            ysp(0), ysp(1), ysp(2), ysp(3),
            pl.BlockSpec((ROW_TILE, LANES), lambda i, j: (i * nblk + j, 0)),
            pl.BlockSpec((1, ROW_TILE, d), lambda i, j: (i, j, 0)),
            pl.BlockSpec((1, 1, 6, d), lambda i, j: (i, sel(j), 0, 0)),
        ],
        out_specs=pl.BlockSpec((1, ROW_TILE, d), lambda i, j: (i, out_map(j), 0)),
        out_shape=jax.ShapeDtypeStruct((b, out_rows, d), F32),
        compiler_params=_cp(("arbitrary", "arbitrary")),
    )(yu, yu, yu, yu, gate, h1, modtab)


def _moe(m, logits, h1, modtab, sel, layer, w1, b1, w2, b2, out_rows, out_map):
    b, r, d = h1.shape
    t = b * r
    n_assign = t * TOP_K
    idx, gate, counts = _route(logits.reshape(t, N_EXPERTS))
    counts = counts[0].astype(jnp.int32)
    n_blocks = n_assign // MOE_TILE + N_EXPERTS
    n_slots = n_blocks * MOE_TILE
    padded = (counts + MOE_TILE - 1) // MOE_TILE * MOE_TILE
    pad_end = jnp.cumsum(padded)
    fill_end = jnp.cumsum(padded - counts)
    fill = jnp.arange(n_slots - n_assign, dtype=jnp.int32)
    fill_e = jnp.sum(fill[:, None] >= fill_end[None, :], axis=1).astype(jnp.int32)
    keys = jnp.concatenate([idx[:, :TOP_K].reshape(-1), fill_e])
    order = jnp.argsort(keys, stable=True).astype(jnp.int32)
    real = order < n_assign
    src_tok = jnp.where(real, order // TOP_K, jnp.arange(n_slots, dtype=jnp.int32) % t)
    spare_row = n_assign + jnp.arange(n_slots, dtype=jnp.int32) % (2 * MOE_TILE)
    dst_row = jnp.where(real, (order % TOP_K) * t + order // TOP_K, spare_row)
    blk_start = jnp.arange(n_blocks, dtype=jnp.int32) * MOE_TILE
    blk_e = jnp.minimum(jnp.sum(blk_start[:, None] >= pad_end[None, :], axis=1), N_EXPERTS - 1).astype(jnp.int32)
    blk_e = jnp.concatenate([blk_e, (pad_end[-1:] // MOE_TILE).astype(jnp.int32)])
    yu = _experts(m, blk_e, src_tok * ROW_SPLIT, dst_row * ROW_SPLIT, layer, w1, b1, w2, b2)
    return _combine(yu, gate, h1, modtab, sel, out_rows, out_map)


def _conv_kernel(a_ref, dw_ref, db_ref, lw_ref, lb_ref, o_ref, hs_ref, *, ll):
    hs_ref[0:CONV_PAD, :] = jnp.zeros((CONV_PAD, CONV_WIDTH), F32)
    hs_ref[CONV_PAD + ll:2 * CONV_PAD + ll, :] = jnp.zeros((CONV_PAD, CONV_WIDTH), F32)
    hs_ref[CONV_PAD:CONV_PAD + ll, :] = a_ref[0, :, :CONV_WIDTH] * _sigmoid(a_ref[0, :, CONV_WIDTH:])
    first = CONV_PAD - CONV_K // 2

    def tile(i, c):
        r0 = pl.multiple_of(i * ROW_TILE, ROW_TILE)
        win = hs_ref[pl.ds(r0, ROW_TILE + 2 * CONV_PAD), :]
        acc = jnp.zeros((ROW_TILE, CONV_WIDTH), F32)
        for k in range(CONV_K):
            acc = acc + dw_ref[k:k + 1, :] * win[first + k:first + k + ROW_TILE]
        acc = acc + db_ref[...]
        mu = jnp.mean(acc, axis=-1, keepdims=True)
        dlt = acc - mu
        var = jnp.mean(dlt * dlt, axis=-1, keepdims=True)
        y = dlt * lax.rsqrt(var + EPS) * lw_ref[...] + lb_ref[...]
        o_ref[0, pl.ds(r0, ROW_TILE), :] = _silu(y)
        return c

    lax.fori_loop(0, ll // ROW_TILE, tile, 0)


def _conv(p, dw_w, dw_b, ln_w, ln_b, ll):
    b = p.shape[0]
    row = lambda v: v.reshape(1, CONV_WIDTH)
    vec = pl.BlockSpec((1, CONV_WIDTH), lambda i: (0, 0))
    return pl.pallas_call(
        functools.partial(_conv_kernel, ll=ll),
        grid=(b,),
        in_specs=[pl.BlockSpec((1, ll, 2 * CONV_WIDTH), lambda i: (i, 0, 0)),
                  pl.BlockSpec((CONV_K, CONV_WIDTH), lambda i: (0, 0)), vec, vec, vec],
        out_specs=pl.BlockSpec((1, ll, CONV_WIDTH), lambda i: (i, 0, 0)),
        out_shape=jax.ShapeDtypeStruct((b, ll, CONV_WIDTH), F32),
        scratch_shapes=[pltpu.VMEM((ll + 2 * CONV_PAD, CONV_WIDTH), F32)],
        compiler_params=_cp(("arbitrary",)),
    )(p, dw_w, row(dw_b), row(ln_w), row(ln_b))


def _mla_proj_kernel(cq_ref, ckv_ref, kr_ref, cs_ref, qa_ref, kva_ref, wq_ref, wqs_ref, wk_ref,
                     wv_ref, ek_ref, eks_ref, gq_ref, gqs_ref, gk_ref, gks_ref,
                     q_ref, k_ref, v_ref):
    cos = cs_ref[0]
    sin = cs_ref[1]
    cq = cq_ref[0]
    cqn = (cq * _rms_scale(cq, MLA_Q_RANK) * qa_ref[...]).astype(BF16)
    qx = _dot(cqn, wq_ref[...])
    qs = _dot(cqn, wqs_ref[...])
    ckv = ckv_ref[0]
    ckvn = (ckv * _rms_scale(ckv, MLA_KV_RANK) * kva_ref[...]).astype(BF16)
    kr = kr_ref[0]
    rh = kr.astype(BF16)
    rl = (kr - rh.astype(F32)).astype(BF16)
    kx = _dot(ckvn, wk_ref[...]) + _dot(rh, ek_ref[...]) + _dot(rl, ek_ref[...])
    ks = _dot(rh, eks_ref[...]) + _dot(rl, eks_ref[...])
    v_ref[0] = _dot(ckvn, wv_ref[...]).astype(BF16)
    scale = MLA_QK ** -0.5
    for h in range(MLA_HEADS):
        sl = slice(h * HEAD_PAD, (h + 1) * HEAD_PAD)
        qh = qx[:, sl]
        q_ref[0, :, sl] = (_rms_scale(qh, MLA_QK) * scale
                           * (qh * gq_ref[...] * cos + qs[:, sl] * gqs_ref[...] * sin)).astype(BF16)
        kh = kx[:, sl]
        k_ref[0, :, sl] = (_rms_scale(kh, MLA_QK)
                           * (kh * gk_ref[...] * cos + ks[:, sl] * gks_ref[...] * sin)).astype(BF16)


def _mla_proj(p, cs, qa, kva, wq, wqs, wk, wv, ek, eks, gq, gqs, gk, gks):
    b, s, _ = p.shape
    hw = MLA_HEADS * HEAD_PAD
    full = lambda a: pl.BlockSpec(a.shape, lambda i, j: (0,) * a.ndim)
    out = pl.BlockSpec((1, ROW_TILE, hw), lambda i, j: (i, j, 0))
    consts = (qa, kva, wq, wqs, wk, wv, ek, eks, gq, gqs, gk, gks)
    return pl.pallas_call(
        _mla_proj_kernel,
        grid=(b, s // ROW_TILE),
        in_specs=[pl.BlockSpec((1, ROW_TILE, 512), lambda i, j: (i, j, OD_CQ // 512)),
                  pl.BlockSpec((1, ROW_TILE, MLA_KV_RANK), lambda i, j: (i, j, OD_CKV // MLA_KV_RANK)),
                  pl.BlockSpec((1, ROW_TILE, LANES), lambda i, j: (i, j, OD_KR // LANES)),
                  pl.BlockSpec((2, ROW_TILE, HEAD_PAD), lambda i, j: (0, j, 0))]
                 + [full(a) for a in consts],
        out_specs=[out, out, out],
        out_shape=[jax.ShapeDtypeStruct((b, s, hw), BF16)] * 3,
        compiler_params=_cp(("arbitrary", "arbitrary")),
    )(p, p, p, cs, *consts)


def _attn_kernel(q_ref, k_ref, v_ref, o_ref):
    def scores(h):
        sl = slice(h * HEAD_PAD, (h + 1) * HEAD_PAD)
        return _dot_nt(q_ref[0, :, sl], k_ref[0, :, sl])

    s = scores(0)
    for h in range(MLA_HEADS):
        s_next = scores(h + 1) if h + 1 < MLA_HEADS else None
        sl = slice(h * HEAD_PAD, (h + 1) * HEAD_PAD)
        p = jnp.exp(s - jnp.max(s, axis=-1, keepdims=True))
        den = jnp.sum(p, axis=-1, keepdims=True)
        o_ref[0, :, sl] = (_dot(p.astype(BF16), v_ref[0, :, sl]) / den).astype(BF16)
        s = s_next


def _attention(q, k, v, ll):
    b, s, hw = k.shape
    kv = pl.BlockSpec((1, s, hw), lambda i, j: (i, 0, 0))
    qo = pl.BlockSpec((1, Q_TILE, hw), lambda i, j: (i, j, 0))
    return pl.pallas_call(
        _attn_kernel,
        grid=(b, ll // Q_TILE),
        in_specs=[qo, kv, kv],
        out_specs=qo,
        out_shape=jax.ShapeDtypeStruct((b, ll, hw), BF16),
        compiler_params=_cp(("arbitrary", "arbitrary")),
    )(q, k, v)


def _place(parts, lead):
    out = jnp.zeros(lead + (MLA_HEADS, HEAD_PAD), F32)
    for arr, off in parts:
        out = out.at[..., off:off + arr.shape[-1]].set(arr)
    return out.reshape(lead + (MLA_HEADS * HEAD_PAD,))


def _swap_pairs(a):
    g = a.reshape(a.shape[:-1] + (2, 2, ROPE_AXIS_HALF))
    return jnp.flip(g, axis=-2).reshape(a.shape)


def _rope_tables(rows_total, ll):
    grid_rows = ll // GRID_W
    t_row = jnp.repeat(jnp.arange(grid_rows, dtype=F32), GRID_W)
    t_col = jnp.tile(jnp.arange(GRID_W, dtype=F32), grid_rows)
    inv = 1.0 / (ROPE_BASE ** (jnp.arange(ROPE_AXIS_HALF, dtype=F32) / ROPE_AXIS_HALF))
    ang_r = t_row[:, None] * inv
    ang_c = t_col[:, None] * inv
    cos32 = jnp.concatenate([jnp.cos(ang_r), jnp.cos(ang_r), jnp.cos(ang_c), jnp.cos(ang_c)], axis=-1)
    sin32 = jnp.concatenate([-jnp.sin(ang_r), jnp.sin(ang_r), -jnp.sin(ang_c), jnp.sin(ang_c)], axis=-1)
    extra = rows_total - ll
    cos32 = jnp.concatenate([cos32, jnp.ones((extra, MLA_ROPE), F32)], axis=0)
    sin32 = jnp.concatenate([sin32, jnp.zeros((extra, MLA_ROPE), F32)], axis=0)
    cos_t = jnp.concatenate([jnp.ones((rows_total, MLA_NOPE), F32), cos32,
                             jnp.zeros((rows_total, HEAD_PAD - MLA_QK), F32)], axis=-1)
    sin_t = jnp.concatenate([jnp.zeros((rows_total, MLA_NOPE), F32), sin32,
                             jnp.zeros((rows_total, HEAD_PAD - MLA_QK), F32)], axis=-1)
    return jnp.stack([cos_t, sin_t])


def kernel(x, c, ctx, c_ctx, ada_w, ada_b, ev_w_in, hgrn_lb_logits, hgrn_norm_w, pool_w, pool_scale,
           ev_w_out, od_w_in, conv_dw_w, conv_dw_b, conv_ln_w, conv_ln_b, mla_q_a_norm, mla_w_uq,
           mla_kv_a_norm, mla_w_ukv, mla_q_norm, mla_k_norm, od_w_out, moe_router_w, moe_router_b,
           moe_w1, moe_b1, moe_w2, moe_b2):
    b, ll, d = x.shape
    lc = ctx.shape[1]
    s = ll + lc
    nc, nl = lc // ROW_TILE, ll // ROW_TILE
    assert ada_w.shape[0] == 2 and d == D_MODEL
    assert ll % Q_TILE == 0 and lc % ROW_TILE == 0 and (b * s) % ROUTE_TILE == 0 and (b * ll) % ROUTE_TILE == 0

    cond_rows = (b + 1 + 7) // 8 * 8
    cond = jnp.zeros((cond_rows, d), F32).at[:b].set(c).at[b].set(c_ctx)
    mod = _ada_table(cond, ada_w, ada_b)
    modtab = []
    for layer in range(2):
        lat = mod[layer, :b].reshape(b, 1, 6, d)
        con = jnp.broadcast_to(mod[layer, b].reshape(1, 1, 6, d), (b, 1, 6, d))
        modtab.append(jnp.concatenate([con, lat], axis=1))
    lb_all = jnp.cumsum(jax.nn.softmax(hgrn_lb_logits.astype(F32), axis=0), axis=0)

    def router_pieces(w):
        hi, mid, _ = _split3(w.T)
        return jnp.concatenate([hi, mid], axis=0)

    sel0 = lambda j: jnp.where(j < nc, 0, 1)
    h = (ctx, x)
    p0 = _modproj(h, modtab[0], ev_w_in[0].astype(BF16), sel0, nc)
    hg = _hgrn(p0, lb_all[0], hgrn_norm_w[0], nc, nl)
    yp = _pool(p0, pool_w[0], pool_scale[0], lc, ll)
    wo = ev_w_out[0].astype(BF16)
    h1, m, lg = _mixout(hg, yp, h, modtab[0], wo[:HG_W], wo[HG_W:], router_pieces(moe_router_w[0]),
                        moe_router_b[0].reshape(1, N_EXPERTS), sel0, 0, s, nc)
    to_l1 = lambda j: jnp.where(j < nc, j + nl, j - nc)
    h = _moe(m, lg, h1, modtab[0], sel0, 0, moe_w1, moe_b1, moe_w2, moe_b2, s, to_l1)

    sel1 = lambda j: jnp.where(j < nl, 1, 0)
    w_in = od_w_in[0]
    kr_cols = w_in[:, 1664:1696]
    w1p = jnp.zeros((d, OD_IN_PAD), F32)
    w1p = w1p.at[:, :1024].set(w_in[:, :1024]).at[:, OD_CQ:OD_CQ + MLA_Q_RANK].set(w_in[:, 1024:1408])
    w1p = w1p.at[:, OD_CKV:OD_CKV + MLA_KV_RANK].set(w_in[:, 1408:1664])
    w1p = w1p.at[:, OD_KR:OD_KR + MLA_ROPE].set(kr_cols)
    w1p = w1p.at[:, OD_KR + MLA_ROPE:OD_KR + 2 * MLA_ROPE].set(_swap_pairs(kr_cols))
    p1 = _modproj(h, modtab[1], w1p.astype(BF16), sel1)
    hc = _conv(p1, conv_dw_w[0], conv_dw_b[0], conv_ln_w[0], conv_ln_b[0], ll)

    wuq = mla_w_uq[0].reshape(MLA_Q_RANK, MLA_HEADS, MLA_QK)
    wuq_rope = wuq[..., MLA_NOPE:]
    pad_rows = lambda w: jnp.zeros((512, w.shape[1]), F32).at[:MLA_Q_RANK].set(w)
    wq = pad_rows(_place([(wuq, 0)], (MLA_Q_RANK,))).astype(BF16)
    wqs = pad_rows(_place([(_swap_pairs(wuq_rope), MLA_NOPE)], (MLA_Q_RANK,))).astype(BF16)
    wukv = mla_w_ukv[0].reshape(MLA_KV_RANK, MLA_HEADS, MLA_NOPE + MLA_V)
    wk = _place([(wukv[..., :MLA_NOPE], 0)], (MLA_KV_RANK,)).astype(BF16)
    wv = _place([(wukv[..., MLA_NOPE:], 0)], (MLA_KV_RANK,)).astype(BF16)
    eye = jnp.broadcast_to(jnp.eye(MLA_ROPE, dtype=F32)[:, None, :], (MLA_ROPE, MLA_HEADS, MLA_ROPE))
    zero_rows = jnp.zeros((LANES - 2 * MLA_ROPE, MLA_HEADS * HEAD_PAD), F32)
    place_kr = _place([(eye, MLA_NOPE)], (MLA_ROPE,))
    zero_kr = jnp.zeros_like(place_kr)
    ek = jnp.concatenate([place_kr, zero_kr, zero_rows], axis=0).astype(BF16)
    eks = jnp.concatenate([zero_kr, place_kr, zero_rows], axis=0).astype(BF16)
    lane_gain = lambda g: jnp.concatenate([g, jnp.zeros((HEAD_PAD - MLA_QK,), F32)]).reshape(1, HEAD_PAD)
    swapped_gain = lambda g: jnp.concatenate(
        [jnp.zeros((MLA_NOPE,), F32), _swap_pairs(g[MLA_NOPE:]), jnp.zeros((HEAD_PAD - MLA_QK,), F32)]
    ).reshape(1, HEAD_PAD)
    qa = jnp.concatenate([mla_q_a_norm[0], jnp.zeros((512 - MLA_Q_RANK,), F32)]).reshape(1, 512)
    q, k, v = _mla_proj(p1, _rope_tables(s, ll), qa, mla_kv_a_norm[0].reshape(1, MLA_KV_RANK),
                        wq, wqs, wk, wv, ek, eks,
                        lane_gain(mla_q_norm[0]), swapped_gain(mla_q_norm[0]),
                        lane_gain(mla_k_norm[0]), swapped_gain(mla_k_norm[0]))
    attn = _attention(q, k, v, ll)

    wo1 = od_w_out[0]
    wo_attn = wo1[CONV_WIDTH:].reshape(MLA_HEADS, MLA_V, d)
    wo_attn = jnp.zeros((MLA_HEADS, HEAD_PAD, d), F32).at[:, :MLA_V].set(wo_attn).reshape(-1, d)
    h1, m, lg = _mixout(hc, attn, h, modtab[1], wo1[:CONV_WIDTH].astype(BF16), wo_attn.astype(BF16),
                        router_pieces(moe_router_w[1]), moe_router_b[1].reshape(1, N_EXPERTS),
                        lambda j: 1, 0, ll)
    return _moe(m, lg, h1, modtab[1], lambda j: 1, 1, moe_w1, moe_b1, moe_w2, moe_b2, ll, lambda j: j)
```
